```python
import math
import jax, jax.numpy as jnp
from jax import lax
import numpy as np

D_MODEL = 4096
BATCH = 32
SEQ = 256
DEPTH = 2
DEC_BATCH = 2
DEC_SEQ = 4096
PAST_LEN = 256

GRID_W = 64
RMS_EPS = 1e-6
N_BRANCH = 4
BRANCH_W = 1024
HY_W = BRANCH_W
HY_POS_EMB = 33
HY_BANDS = (HY_POS_EMB - 1) // 2
HY_FFN = 64
HY_MIN_DECAY = math.log(1e-2) / 1.5
HY_MAX_DECAY = math.log(1e-2) / 0.3
S5_W = BRANCH_W
S5_GROUP = 16
S5_GROUPS = S5_W // S5_GROUP
S5_STATE = 64
MLA_HEADS = 8
MLA_NOPE = 128
MLA_ROPE = 64
MLA_V = BRANCH_W // MLA_HEADS
MLA_Q_RANK = 768
MLA_KV_RANK = 512
ROPE_BASE = 10000.0
Q_BLOCK = 128
GLA_HEADS = 4
GLA_DK = 128
GLA_DV = BRANCH_W // GLA_HEADS
GLA_GATE_RANK = 16
GLA_GATE_TEMP = 16.0
GLA_CHUNK = 64
N_EXPERTS = 64
TOP_K = 6
N_EXPERT_GROUPS = 8
EXPERTS_PER_GROUP = N_EXPERTS // N_EXPERT_GROUPS
TOPK_GROUPS = 4
D_EXPERT = 1024
D_SHARED = 1024
ROUTED_SCALE = 2.5
MOE_BLOCK = 128
IN_SPLITS = (3 * HY_W, S5_W, MLA_Q_RANK, MLA_KV_RANK, MLA_ROPE,
             GLA_HEADS * GLA_DK, GLA_HEADS * GLA_DK, GLA_HEADS * GLA_DV, GLA_HEADS * GLA_DV,
             GLA_GATE_RANK, GLA_GATE_RANK)
N_IN = 3 * HY_W + S5_W + MLA_Q_RANK + MLA_KV_RANK + MLA_ROPE + 2 * GLA_HEADS * GLA_DK + 2 * GLA_HEADS * GLA_DV + 2 * GLA_GATE_RANK

kernel_name = "hybrid_diffusion_prefix_trunk_step"


def rmsnorm(x, g):
    xf = x.astype(jnp.float32)
    y = xf * lax.rsqrt(jnp.mean(xf * xf, axis=-1, keepdims=True) + RMS_EPS)
    return (y * g.astype(jnp.float32)).astype(x.dtype)


def split_cols(z, sizes):
    out, o = [], 0
    for s in sizes:
        out.append(z[..., o:o + s])
        o += s
    return out


def adaln(cvec, w, b):
    return jnp.split(jax.nn.silu(cvec) @ w + b, 6, axis=-1)


def short_conv3(x, w, b):
    L = x.shape[1]
    xp = jnp.pad(x, ((0, 0), (1, 1), (0, 0)))
    return xp[:, :L] * w[0] + xp[:, 1:L + 1] * w[1] + xp[:, 2:] * w[2] + b


def hyena_filters(L, w1, b1, w2, b2, w3, freq, decay):
    f32 = jnp.float32
    t = jnp.linspace(0.0, 1.0, L, dtype=f32)[:, None]
    w = 2.0 * math.pi * jnp.arange(L, dtype=f32)[:, None] / L
    f = jnp.linspace(1e-4, HY_BANDS - 1, HY_BANDS, dtype=f32)[None, :]
    z = jnp.concatenate([t, jnp.cos(f * w), -jnp.sin(f * w)], axis=-1)
    freq = freq.astype(f32)
    h = jnp.sin(freq[0] * (z @ w1.astype(f32) + b1.astype(f32)))
    h = jnp.sin(freq[1] * (h @ w2.astype(f32) + b2.astype(f32)))
    h = (h @ w3.astype(f32)) * jnp.exp(-t * jnp.abs(decay.astype(f32)))
    h = h / (jnp.sum(jnp.abs(h), axis=0, keepdims=True) + 1e-6)
    return h.reshape(L, 2, 2, HY_W)


def long_conv_bidir(u, hf, hb, skip):
    L = u.shape[1]
    k = jnp.concatenate([hf, jnp.zeros_like(hf[:1]), hb[:0:-1]], axis=0)
    uf32 = u.astype(jnp.float32)
    uf = jnp.fft.rfft(uf32, n=2 * L, axis=1)
    kf = jnp.fft.rfft(k, n=2 * L, axis=0)
    y = jnp.fft.irfft(uf * kf[None], n=2 * L, axis=1)[:, :L]
    return (y + uf32 * skip.astype(jnp.float32)).astype(u.dtype)


def hyena_mixer(z, conv_w, conv_b, w1, b1, w2, b2, w3, freq, decay, skip):
    L = z.shape[1]
    z = short_conv3(z, conv_w, conv_b)
    x1, x2, v = jnp.split(z, 3, axis=-1)
    h = hyena_filters(L, w1, b1, w2, b2, w3, freq, decay)
    y = x1 * long_conv_bidir(v, h[:, 0, 0], h[:, 0, 1], skip[0])
    return x2 * long_conv_bidir(y, h[:, 1, 0], h[:, 1, 1], skip[1])


def _cplx_combine(e1, e2):
    ar1, ai1, br1, bi1 = e1
    ar2, ai2, br2, bi2 = e2
    return (ar2 * ar1 - ai2 * ai1, ar2 * ai1 + ai2 * ar1,
            ar2 * br1 - ai2 * bi1 + br2, ar2 * bi1 + ai2 * br1 + bi2)


def s5_direction(u, lam_re, lam_im, log_step, b_re, b_im, c_re, c_im, h0_re, h0_im):
    lr = jnp.minimum(lam_re, -1e-4)
    dt = jnp.exp(log_step)[:, None]
    mag, ang = jnp.exp(lr * dt), lam_im * dt
    ar, ai = mag * jnp.cos(ang), mag * jnp.sin(ang)
    den = lr * lr + lam_im * lam_im
    kr = ((ar - 1.0) * lr + ai * lam_im) / den
    ki = (ai * lr - (ar - 1.0) * lam_im) / den
    bb_re = kr[..., None] * b_re - ki[..., None] * b_im
    bb_im = kr[..., None] * b_im + ki[..., None] * b_re
    bu_re = jnp.einsum('blgs,gps->blgp', u, bb_re)
    bu_im = jnp.einsum('blgs,gps->blgp', u, bb_im)
    bu_re = bu_re.at[:, 0].add(ar * h0_re - ai * h0_im)
    bu_im = bu_im.at[:, 0].add(ar * h0_im + ai * h0_re)
    a_re = jnp.broadcast_to(ar, bu_re.shape)
    a_im = jnp.broadcast_to(ai, bu_im.shape)
    _, _, s_re, s_im = lax.associative_scan(_cplx_combine, (a_re, a_im, bu_re, bu_im), axis=1)
    y = jnp.einsum('gsp,blgp->blgs', c_re, s_re) - jnp.einsum('gsp,blgp->blgs', c_im, s_im)
    return y, s_re[:, -1], s_im[:, -1]


def s5_mixer(u, lam_re, lam_im, log_step, b_re, b_im, c_re, c_im, d, glu_w, glu_b, h0):
    B, L, _ = u.shape
    f32 = jnp.float32
    ug = u.astype(f32).reshape(B, L, S5_GROUPS, S5_GROUP)
    h0 = h0.astype(f32)
    prm = [t.astype(f32) for t in (lam_re, lam_im, log_step, b_re, b_im, c_re, c_im)]
    yf, fr, fi = s5_direction(ug, *[t[0] for t in prm], h0[:, 0, 0], h0[:, 0, 1])
    yb, br, bi = s5_direction(ug[:, ::-1], *[t[1] for t in prm], h0[:, 1, 0], h0[:, 1, 1])
    y = yf + yb[:, ::-1] + ug * d.astype(f32).reshape(S5_GROUPS, S5_GROUP)
    y = jax.nn.gelu(y.reshape(B, L, S5_W)).astype(u.dtype)
    y = y * jax.nn.sigmoid(y @ glu_w + glu_b)
    fin = jnp.stack([jnp.stack([fr, fi], 1), jnp.stack([br, bi], 1)], 1)
    return y, fin


def axial_rope_tables(L):
    n_rows = L // GRID_W
    rows = jnp.repeat(jnp.arange(n_rows, dtype=jnp.float32), GRID_W)
    cols = jnp.tile(jnp.arange(GRID_W, dtype=jnp.float32), n_rows)
    n_freq = MLA_ROPE // 4
    inv = ROPE_BASE ** (-jnp.arange(n_freq, dtype=jnp.float32) / n_freq)
    a_r, a_c = rows[:, None] * inv, cols[:, None] * inv
    ang = jnp.concatenate([a_r, a_r, a_c, a_c], axis=-1)
    return jnp.cos(ang), jnp.sin(ang)


def apply_rope(x, cos, sin):
    q = MLA_ROPE // 4
    x1, x2, x3, x4 = x[..., :q], x[..., q:2 * q], x[..., 2 * q:3 * q], x[..., 3 * q:]
    rot = jnp.concatenate([-x2, x1, -x4, x3], axis=-1)
    return (x * cos + rot * sin).astype(x.dtype)


def mla_queries(q_a, kv_a, qa_g, w_qb, kva_g):
    B, L, _ = q_a.shape
    q = (rmsnorm(q_a, qa_g) @ w_qb).reshape(B, L, MLA_HEADS, MLA_NOPE + MLA_ROPE)
    return q[..., :MLA_NOPE], q[..., MLA_NOPE:], rmsnorm(kv_a, kva_g)


def mla_expand(c_kv, w_kvb):
    B, L, _ = c_kv.shape
    kv = (c_kv @ w_kvb).reshape(B, L, MLA_HEADS, MLA_NOPE + MLA_V)
    return kv[..., :MLA_NOPE], kv[..., MLA_NOPE:]


def block_attention(q_nope, q_rope, k_nope, k_rope, v):
    B, Lq = q_nope.shape[:2]
    nb = Lq // Q_BLOCK
    scale = (MLA_NOPE + MLA_ROPE) ** -0.5
    qn = q_nope.reshape(B, nb, Q_BLOCK, MLA_HEADS, MLA_NOPE).swapaxes(0, 1)
    qr = q_rope.reshape(B, nb, Q_BLOCK, MLA_HEADS, MLA_ROPE).swapaxes(0, 1)

    def one_block(args):
        qn_b, qr_b = args
        s = jnp.einsum('bqhd,bkhd->bhqk', qn_b, k_nope) + jnp.einsum('bqhr,bkr->bhqk', qr_b, k_rope)
        p = jax.nn.softmax(s.astype(jnp.float32) * scale, axis=-1)
        return jnp.einsum('bhqk,bkhd->bqhd', p.astype(v.dtype), v)

    o = lax.map(one_block, (qn, qr))
    return o.swapaxes(0, 1).reshape(B, Lq, MLA_HEADS * MLA_V)


def gla_direction(q, k, v, log_a, S0):
    B, L, H, _ = q.shape
    n, C = L // GLA_CHUNK, GLA_CHUNK
    rs = lambda t: t.reshape(B, n, C, H, t.shape[-1])
    q, k, v, log_a = rs(q), rs(k), rs(v), rs(log_a)
    b = jnp.cumsum(log_a, axis=2)
    b_end = b[:, :, -1]
    q_in = q * jnp.exp(b)
    k_in = k * jnp.exp(-b)
    k_end = k * jnp.exp(b_end[:, :, None] - b)
    mask = jnp.tril(jnp.ones((C, C), dtype=bool))
    s = jnp.where(mask, jnp.einsum('bnthd,bnshd->bnhts', q_in, k_in), 0.0)
    o = jnp.einsum('bnhts,bnshe->bnthe', s, v)
    kv = jnp.einsum('bnshd,bnshe->bnhde', k_end, v)
    decay = jnp.exp(b_end)

    def step(S, inp):
        dec, kv_c = inp
        return dec[..., None] * S + kv_c, S

    S_fin, S_start = lax.scan(step, S0, (jnp.moveaxis(decay, 1, 0), jnp.moveaxis(kv, 1, 0)))
    o = o + jnp.einsum('bnthd,nbhde->bnthe', q_in, S_start)
    return o.reshape(B, L, H, GLA_DV), S_fin


def gla_mixer(q, k, v, r, g_af, g_ab, wa2, ba2, norm_g, S0):
    B, L, _ = q.shape
    f32 = jnp.float32
    heads = lambda t, dd: t.astype(f32).reshape(B, L, GLA_HEADS, dd)
    qh = heads(q, GLA_DK) * GLA_DK ** -0.5
    kh, vh = heads(k, GLA_DK), heads(v, GLA_DV)
    la_f = heads(jax.nn.log_sigmoid((g_af @ wa2[0] + ba2[0]).astype(f32)), GLA_DK) / GLA_GATE_TEMP
    la_b = heads(jax.nn.log_sigmoid((g_ab @ wa2[1] + ba2[1]).astype(f32)), GLA_DK) / GLA_GATE_TEMP
    S0 = S0.astype(f32)
    of, sf = gla_direction(qh, kh, vh, la_f, S0[:, 0])
    ob, sb = gla_direction(qh[:, ::-1], kh[:, ::-1], vh[:, ::-1], la_b[:, ::-1], S0[:, 1])
    o = rmsnorm(of + ob[:, ::-1], norm_g).reshape(B, L, GLA_HEADS * GLA_DV)
    o = o.astype(r.dtype) * jax.nn.silu(r)
    return o, jnp.stack([sf, sb], 1)


def merge_branches(h, ys, w_branch, w_bgate, b_bgate, w_out):
    m = 0.0
    for i, y in enumerate(ys):
        m = m + jax.nn.sigmoid(h @ w_bgate[i] + b_bgate[i]) * (y @ w_branch[i])
    return m @ w_out


def routed_experts(xf, idx, wts, w_gate, w_up, w_down):
    T, D = xf.shape
    n_asg = T * TOP_K
    e_flat = idx.reshape(-1)
    order = jnp.argsort(e_flat)
    e_s = e_flat[order]
    tok_s = (jnp.arange(n_asg, dtype=jnp.int32) // TOP_K)[order]
    w_s = wts.reshape(-1)[order]
    counts = jnp.bincount(e_flat, length=N_EXPERTS)
    start = jnp.cumsum(counts) - counts
    padded = (counts + MOE_BLOCK - 1) // MOE_BLOCK * MOE_BLOCK
    pad_end = jnp.cumsum(padded)
    pad_start = pad_end - padded
    dest = pad_start[e_s] + jnp.arange(n_asg, dtype=jnp.int32) - start[e_s]
    n_blocks = -(-n_asg // MOE_BLOCK) + N_EXPERTS
    n_slots = n_blocks * MOE_BLOCK
    slot_tok = jnp.zeros((n_slots,), jnp.int32).at[dest].set(tok_s)
    slot_w = jnp.zeros((n_slots,), xf.dtype).at[dest].set(w_s)
    blk_e = jnp.minimum(jnp.searchsorted(pad_end, jnp.arange(n_blocks, dtype=jnp.int32) * MOE_BLOCK, side='right'), N_EXPERTS - 1)

    def expert_block(args):
        tok, w, e = args
        xb = xf[tok]
        hb = jax.nn.silu(xb @ w_gate[e]) * (xb @ w_up[e])
        return (hb @ w_down[e]) * w[:, None]

    y = lax.map(expert_block, (slot_tok.reshape(n_blocks, MOE_BLOCK), slot_w.reshape(n_blocks, MOE_BLOCK), blk_e))
    return jnp.zeros_like(xf).at[slot_tok].add(y.reshape(n_slots, D))


def moe(h, router_w, router_b, w_gate, w_up, w_down, sh_gate, sh_up, sh_down):
    B, L, D = h.shape
    xf = h.reshape(B * L, D)
    T = xf.shape[0]
    scores = jax.nn.sigmoid((xf @ router_w).astype(jnp.float32))
    sel = scores + router_b.astype(jnp.float32)
    grp = lax.top_k(sel.reshape(T, N_EXPERT_GROUPS, EXPERTS_PER_GROUP), 2)[0].sum(-1)
    _, grp_idx = lax.top_k(grp, TOPK_GROUPS)
    grp_mask = jax.nn.one_hot(grp_idx, N_EXPERT_GROUPS, dtype=jnp.float32).sum(-2) > 0
    sel = jnp.where(jnp.repeat(grp_mask, EXPERTS_PER_GROUP, axis=-1), sel, -jnp.inf)
    _, idx = lax.top_k(sel, TOP_K)
    w = jnp.take_along_axis(scores, idx, axis=-1)
    w = w / jnp.sum(w, axis=-1, keepdims=True) * ROUTED_SCALE
    routed = routed_experts(xf, idx, w.astype(xf.dtype), w_gate, w_up, w_down)
    shared = (jax.nn.silu(xf @ sh_gate) * (xf @ sh_up)) @ sh_down
    return (routed + shared).reshape(B, L, D)


def setup_inputs(seed: int = 0) -> dict:
    key = jax.random.key(seed)
    ks = iter(jax.random.split(key, 64))
    f32 = jnp.float32

    def nrm(shape, scale=1.0):
        return jax.random.normal(next(ks), shape, f32) * scale

    def gain(shape):
        return 1.0 + nrm(shape, 0.1)

    Lg, D = DEPTH, D_MODEL
    G, P, S = S5_GROUPS, S5_STATE, S5_GROUP
    H = MLA_HEADS
    decay_base = jnp.abs(jnp.linspace(HY_MIN_DECAY, HY_MAX_DECAY, 4 * HY_W, dtype=f32))
    n_idx = jnp.arange(P, dtype=f32)
    return {
        'x_prompt': nrm((BATCH, SEQ, D)),
        'x_sample': nrm((DEC_BATCH, DEC_SEQ, D)),
        'c': nrm((DEC_BATCH, D)),
        'cache_mla_ckv': nrm((DEC_BATCH, Lg, PAST_LEN, MLA_KV_RANK)),
        'cache_mla_krope': nrm((DEC_BATCH, Lg, PAST_LEN, MLA_ROPE)),
        'state_s5': nrm((DEC_BATCH, Lg, 2, 2, G, P), 0.1),
        'state_gla': nrm((DEC_BATCH, Lg, 2, GLA_HEADS, GLA_DK, GLA_DV), 0.1),
        'c_ctx': nrm((D,)),
        'mod_w': nrm((Lg, D, 6 * D), 0.5 * D ** -0.5),
        'mod_b': nrm((Lg, 6 * D), 0.02),
        'norm1_g': gain((Lg, D)),
        'norm2_g': gain((Lg, D)),
        'w_in': nrm((Lg, D, N_IN), D ** -0.5),
        'hy_conv_w': nrm((Lg, 3, 3 * HY_W), 3 ** -0.5),
        'hy_conv_b': nrm((Lg, 3 * HY_W), 0.02),
        'hy_w1': nrm((Lg, HY_POS_EMB, HY_FFN), HY_POS_EMB ** -0.5),
        'hy_b1': nrm((Lg, HY_FFN), 0.1),
        'hy_w2': nrm((Lg, HY_FFN, HY_FFN), HY_FFN ** -0.5),
        'hy_b2': nrm((Lg, HY_FFN), 0.1),
        'hy_w3': nrm((Lg, HY_FFN, 4 * HY_W), HY_FFN ** -0.5),
        'hy_freq': 1.0 + nrm((Lg, 2, HY_FFN), 0.1),
        'hy_decay': decay_base * (1.0 + nrm((Lg, 4 * HY_W), 0.05)),
        'hy_skip': nrm((Lg, 2, HY_W)),
        's5_lam_re': -0.5 + nrm((Lg, 2, G, P), 0.01),
        's5_lam_im': math.pi * n_idx + nrm((Lg, 2, G, P), 0.01),
        's5_log_step': jax.random.uniform(next(ks), (Lg, 2, G), f32, math.log(1e-3), math.log(1e-1)),
        's5_b_re': nrm((Lg, 2, G, P, S), (2 * S) ** -0.5),
        's5_b_im': nrm((Lg, 2, G, P, S), (2 * S) ** -0.5),
        's5_c_re': nrm((Lg, 2, G, S, P), P ** -0.5),
        's5_c_im': nrm((Lg, 2, G, S, P), P ** -0.5),
        's5_d': nrm((Lg, S5_W)),
        's5_glu_w': nrm((Lg, S5_W, S5_W), S5_W ** -0.5),
        's5_glu_b': nrm((Lg, S5_W), 0.02),
        'mla_qa_g': gain((Lg, MLA_Q_RANK)),
        'mla_wqb': nrm((Lg, MLA_Q_RANK, H * (MLA_NOPE + MLA_ROPE)), MLA_Q_RANK ** -0.5),
        'mla_kva_g': gain((Lg, MLA_KV_RANK)),
        'mla_wkvb': nrm((Lg, MLA_KV_RANK, H * (MLA_NOPE + MLA_V)), MLA_KV_RANK ** -0.5),
        'gla_wa2': nrm((Lg, 2, GLA_GATE_RANK, GLA_HEADS * GLA_DK), GLA_GATE_RANK ** -0.5),
        'gla_ba2': nrm((Lg, 2, GLA_HEADS * GLA_DK), 0.1),
        'gla_norm_g': gain((Lg, GLA_DV)),
        'w_branch': nrm((Lg, N_BRANCH, BRANCH_W, D), BRANCH_W ** -0.5),
        'w_bgate': nrm((Lg, N_BRANCH, D, D), D ** -0.5),
        'b_bgate': nrm((Lg, N_BRANCH, D), 0.02),
        'w_out': nrm((Lg, D, D), D ** -0.5),
        'router_w': nrm((Lg, D, N_EXPERTS), D ** -0.5),
        'router_b': nrm((Lg, N_EXPERTS), 0.01),
        'exp_w_gate': nrm((Lg, N_EXPERTS, D, D_EXPERT), D ** -0.5),
        'exp_w_up': nrm((Lg, N_EXPERTS, D, D_EXPERT), D ** -0.5),
        'exp_w_down': nrm((Lg, N_EXPERTS, D_EXPERT, D), D_EXPERT ** -0.5),
        'sh_w_gate': nrm((Lg, D, D_SHARED), D ** -0.5),
        'sh_w_up': nrm((Lg, D, D_SHARED), D ** -0.5),
        'sh_w_down': nrm((Lg, D_SHARED, D), D_SHARED ** -0.5),
        'final_g': gain((D,)),
    }


def reference(x_prompt, x_sample, c, cache_mla_ckv, cache_mla_krope, state_s5, state_gla, c_ctx,
              mod_w, mod_b, norm1_g, norm2_g, w_in,
              hy_conv_w, hy_conv_b, hy_w1, hy_b1, hy_w2, hy_b2, hy_w3, hy_freq, hy_decay, hy_skip,
              s5_lam_re, s5_lam_im, s5_log_step, s5_b_re, s5_b_im, s5_c_re, s5_c_im, s5_d, s5_glu_w, s5_glu_b,
              mla_qa_g, mla_wqb, mla_kva_g, mla_wkvb,
              gla_wa2, gla_ba2, gla_norm_g,
              w_branch, w_bgate, b_bgate, w_out,
              router_w, router_b, exp_w_gate, exp_w_up, exp_w_down, sh_w_gate, sh_w_up, sh_w_down,
              final_g):

    def layer(x, cvec, l, ctx):
        B, L, _ = x.shape
        sh1, sc1, g1, sh2, sc2, g2 = adaln(cvec, mod_w[l], mod_b[l])
        h = rmsnorm(x, norm1_g[l]) * (1 + sc1) + sh1
        (z_hy, u_s5, q_a, kv_a, k_rope, g_q, g_k, g_v, g_r, g_af, g_ab) = split_cols(h @ w_in[l], IN_SPLITS)
        if ctx is None:
            s5_h0 = jnp.zeros((B, 2, 2, S5_GROUPS, S5_STATE), jnp.float32)
            gla_s0 = jnp.zeros((B, 2, GLA_HEADS, GLA_DK, GLA_DV), jnp.float32)
        else:
            ckv_ctx, kr_ctx, s5_h0, gla_s0 = ctx
        y_hy = hyena_mixer(z_hy, hy_conv_w[l], hy_conv_b[l], hy_w1[l], hy_b1[l], hy_w2[l], hy_b2[l],
                           hy_w3[l], hy_freq[l], hy_decay[l], hy_skip[l])
        y_s5, s5_fin = s5_mixer(u_s5, s5_lam_re[l], s5_lam_im[l], s5_log_step[l], s5_b_re[l], s5_b_im[l],
                                s5_c_re[l], s5_c_im[l], s5_d[l], s5_glu_w[l], s5_glu_b[l], s5_h0)
        q_nope, q_rope, c_kv = mla_queries(q_a, kv_a, mla_qa_g[l], mla_wqb[l], mla_kva_g[l])
        k_nope, v = mla_expand(c_kv, mla_wkvb[l])
        if ctx is None:
            y_mla = block_attention(q_nope, q_rope, k_nope, k_rope, v)
        else:
            cos, sin = axial_rope_tables(L)
            q_rot = apply_rope(q_rope, cos[:, None, :], sin[:, None, :])
            k_rot = apply_rope(k_rope, cos, sin)
            kn_ctx, v_ctx = mla_expand(ckv_ctx, mla_wkvb[l])
            y_mla = block_attention(q_nope, q_rot,
                                    jnp.concatenate([kn_ctx, k_nope], axis=1),
                                    jnp.concatenate([kr_ctx, k_rot], axis=1),
                                    jnp.concatenate([v_ctx, v], axis=1))
        y_gla, gla_fin = gla_mixer(g_q, g_k, g_v, g_r, g_af, g_ab, gla_wa2[l], gla_ba2[l], gla_norm_g[l], gla_s0)
        mix = merge_branches(h, (y_hy, y_s5, y_mla, y_gla), w_branch[l], w_bgate[l], b_bgate[l], w_out[l])
        x = x + g1 * mix
        h2 = rmsnorm(x, norm2_g[l]) * (1 + sc2) + sh2
        x = x + g2 * moe(h2, router_w[l], router_b[l], exp_w_gate[l], exp_w_up[l], exp_w_down[l],
                         sh_w_gate[l], sh_w_up[l], sh_w_down[l])
        return x, (c_kv, k_rope, s5_fin, gla_fin)

    xp = x_prompt
    ctx_vec = c_ctx[None, None, :]
    ckv_l, kr_l, s5_l, gla_l = [], [], [], []
    for l in range(DEPTH):
        xp, (ckv, kr, s5f, glaf) = layer(xp, ctx_vec, l, None)
        ckv_l.append(ckv)
        kr_l.append(kr)
        s5_l.append(s5f)
        gla_l.append(glaf)
    y_prompt = rmsnorm(xp, final_g)
    new_mla_ckv = jnp.stack(ckv_l, axis=1)
    new_mla_krope = jnp.stack(kr_l, axis=1)
    new_s5 = jnp.stack(s5_l, axis=1)
    new_gla = jnp.stack(gla_l, axis=1)

    xs = x_sample
    cvec = c[:, None, :]
    for l in range(DEPTH):
        xs, _ = layer(xs, cvec, l, (cache_mla_ckv[:, l], cache_mla_krope[:, l], state_s5[:, l], state_gla[:, l]))
    y_sample = rmsnorm(xs, final_g)

    return (y_prompt, y_sample, new_mla_ckv, new_mla_krope, new_s5, new_gla)
```

```python
import functools
import math

import jax
import jax.numpy as jnp
from jax import lax
from jax.experimental import pallas as pl
from jax.experimental.pallas import tpu as pltpu

D_MODEL = 4096
BATCH = 32
SEQ = 256
DEPTH = 2
DEC_BATCH = 2
DEC_SEQ = 4096
PAST_LEN = 256
GRID_W = 64
RMS_EPS = 1e-6
N_BRANCH = 4
BRANCH_W = 1024
HY_W = BRANCH_W
HY_POS_EMB = 33
HY_BANDS = (HY_POS_EMB - 1) // 2
HY_FFN = 64
S5_W = BRANCH_W
S5_GROUP = 16
S5_GROUPS = S5_W // S5_GROUP
S5_STATE = 64
MLA_HEADS = 8
MLA_NOPE = 128
MLA_ROPE = 64
MLA_V = BRANCH_W // MLA_HEADS
MLA_Q_RANK = 768
MLA_KV_RANK = 512
ROPE_BASE = 10000.0
Q_BLOCK = 128
GLA_HEADS = 4
GLA_DK = 128
GLA_DV = BRANCH_W // GLA_HEADS
GLA_GATE_RANK = 16
GLA_GATE_TEMP = 16.0
GLA_CHUNK = 64
N_EXPERTS = 64
TOP_K = 6
N_EXPERT_GROUPS = 8
EXPERTS_PER_GROUP = N_EXPERTS // N_EXPERT_GROUPS
TOPK_GROUPS = 4
D_EXPERT = 1024
D_SHARED = 1024
ROUTED_SCALE = 2.5
IN_SPLITS = (3 * HY_W, S5_W, MLA_Q_RANK, MLA_KV_RANK, MLA_ROPE,
             GLA_HEADS * GLA_DK, GLA_HEADS * GLA_DK, GLA_HEADS * GLA_DV, GLA_HEADS * GLA_DV,
             GLA_GATE_RANK, GLA_GATE_RANK)
N_IN = sum(IN_SPLITS)

T_CTX = BATCH * SEQ
T_LAT = DEC_BATCH * DEC_SEQ
T_ALL = T_CTX + T_LAT
N_MOD_GROUPS = 1 + DEC_BATCH
N_MOD = 6

F32 = jnp.float32
BF16 = jnp.bfloat16

VMEM_LIMIT_BYTES = 56 * 1024 * 1024
ROW_TILE = 1024
MOE_ROWS = 256
SHARED_ROWS = 512
ADALN_ROWS = 16


def _cparams(*sem):
    return pltpu.CompilerParams(dimension_semantics=sem, vmem_limit_bytes=VMEM_LIMIT_BYTES)


def _mod_group(i, tm):
    n_ctx = T_CTX // tm
    per_req = DEC_SEQ // tm
    return jnp.where(i < n_ctx, 0, 1 + (i - n_ctx) // per_req)


def _dot(a, b):
    return jnp.dot(a, b, preferred_element_type=F32)


def _mm_bias_kernel(a_ref, w_ref, b_ref, o_ref):
    o_ref[...] = _dot(a_ref[...], w_ref[...].astype(BF16)) + b_ref[...]


def _adaln(a, mod_w, mod_b3, l, tn=1024):
    m, k = a.shape
    n = mod_w.shape[-1]
    return pl.pallas_call(
        _mm_bias_kernel,
        grid=(n // tn,),
        in_specs=[pl.BlockSpec((m, k), lambda j: (0, 0)),
                  pl.BlockSpec((None, k, tn), lambda j: (l, 0, j)),
                  pl.BlockSpec((None, 1, tn), lambda j: (l, 0, j))],
        out_specs=pl.BlockSpec((m, tn), lambda j: (0, j)),
        out_shape=jax.ShapeDtypeStruct((m, n), F32),
        compiler_params=_cparams("parallel"),
        name="adaln",
    )(a, mod_w, mod_b3)


def _norm_mod_kernel(x_ref, g_ref, sc_ref, sh_ref, o_ref):
    x = x_ref[...]
    y = x * lax.rsqrt(jnp.mean(x * x, axis=-1, keepdims=True) + RMS_EPS) * g_ref[...]
    o_ref[...] = (y * (1.0 + sc_ref[...]) + sh_ref[...]).astype(o_ref.dtype)


def _norm_mod_router_kernel(x_ref, g_ref, sc_ref, sh_ref, rw_ref, o_ref, lg_ref):
    x = x_ref[...]
    y = x * lax.rsqrt(jnp.mean(x * x, axis=-1, keepdims=True) + RMS_EPS) * g_ref[...]
    h = y * (1.0 + sc_ref[...]) + sh_ref[...]
    o_ref[...] = h.astype(o_ref.dtype)
    lg_ref[...] = jnp.dot(h, rw_ref[...], preferred_element_type=F32, precision=lax.Precision.HIGHEST)


def _norm_mod(x, gain3, mod3, l, which_scale, which_shift, router_w=None, tm=256):
    t, d = x.shape
    in_specs = [pl.BlockSpec((tm, d), lambda i: (i, 0)),
                pl.BlockSpec((None, 1, d), lambda i: (l, 0, 0)),
                pl.BlockSpec((None, 1, d), lambda i: (_mod_group(i, tm) * N_MOD + which_scale, 0, 0)),
                pl.BlockSpec((None, 1, d), lambda i: (_mod_group(i, tm) * N_MOD + which_shift, 0, 0))]
    h_spec = pl.BlockSpec((tm, d), lambda i: (i, 0))
    h_shape = jax.ShapeDtypeStruct((t, d), BF16)
    if router_w is None:
        return pl.pallas_call(
            _norm_mod_kernel, grid=(t // tm,), in_specs=in_specs, out_specs=h_spec, out_shape=h_shape,
            compiler_params=_cparams("parallel"), name="norm_mod",
        )(x, gain3, mod3, mod3)
    return pl.pallas_call(
        _norm_mod_router_kernel, grid=(t // tm,),
        in_specs=in_specs + [pl.BlockSpec((None, d, N_EXPERTS), lambda i: (l, 0, 0))],
        out_specs=(h_spec, pl.BlockSpec((tm, N_EXPERTS), lambda i: (i, 0))),
        out_shape=(h_shape, jax.ShapeDtypeStruct((t, N_EXPERTS), F32)),
        compiler_params=_cparams("parallel"), name="norm_mod_router",
    )(x, gain3, mod3, mod3, router_w)


def _final_norm_kernel(x_ref, g_ref, o_ref):
    x = x_ref[...]
    o_ref[...] = x * lax.rsqrt(jnp.mean(x * x, axis=-1, keepdims=True) + RMS_EPS) * g_ref[...]


def _final_norm(x, g2, tm=256):
    t, d = x.shape
    return pl.pallas_call(
        _final_norm_kernel, grid=(t // tm,),
        in_specs=[pl.BlockSpec((tm, d), lambda i: (i, 0)), pl.BlockSpec((1, d), lambda i: (0, 0))],
        out_specs=pl.BlockSpec((tm, d), lambda i: (i, 0)),
        out_shape=jax.ShapeDtypeStruct((t, d), F32),
        compiler_params=_cparams("parallel"), name="final_norm",
    )(x, g2)


def _mm_kernel(a_ref, w_ref, o_ref):
    o_ref[...] = _dot(a_ref[...], w_ref[...].astype(BF16)).astype(o_ref.dtype)


def _in_proj(h, w_in, l, tm=ROW_TILE, tn=512):
    t, k = h.shape
    n = w_in.shape[-1]
    return pl.pallas_call(
        _mm_kernel, grid=(t // tm, pl.cdiv(n, tn)),
        in_specs=[pl.BlockSpec((tm, k), lambda i, j: (i, 0)),
                  pl.BlockSpec((None, k, tn), lambda i, j: (l, 0, j))],
        out_specs=pl.BlockSpec((tm, tn), lambda i, j: (i, j)),
        out_shape=jax.ShapeDtypeStruct((t, n), F32),
        compiler_params=_cparams("parallel", "parallel"), name="in_proj",
    )(h, w_in)


def _merge_kernel(h_ref, wg_ref, bg_ref, y_ref, wb_ref, o_ref, acc_ref):
    i = pl.program_id(2)
    gate = jax.nn.sigmoid(_dot(h_ref[...], wg_ref[...].astype(BF16)) + bg_ref[...])
    contrib = gate * _dot(y_ref[...], wb_ref[...].astype(BF16))

    @pl.when(i == 0)
    def _():
        acc_ref[...] = contrib

    @pl.when(i > 0)
    def _():
        acc_ref[...] += contrib

    @pl.when(i == N_BRANCH - 1)
    def _():
        o_ref[...] = acc_ref[...].astype(o_ref.dtype)


def _merge(h, ys, w_bgate, b_bgate3, w_branch, l, tm=ROW_TILE, tn=256):
    t, d = h.shape
    bw = ys.shape[-1]
    return pl.pallas_call(
        _merge_kernel, grid=(t // tm, d // tn, N_BRANCH),
        in_specs=[pl.BlockSpec((tm, d), lambda m, n, i: (m, 0)),
                  pl.BlockSpec((None, None, d, tn), lambda m, n, i: (l, i, 0, n)),
                  pl.BlockSpec((None, 1, tn), lambda m, n, i: (l * N_BRANCH + i, 0, n)),
                  pl.BlockSpec((None, tm, bw), lambda m, n, i: (i, m, 0)),
                  pl.BlockSpec((None, None, bw, tn), lambda m, n, i: (l, i, 0, n))],
        out_specs=pl.BlockSpec((tm, tn), lambda m, n, i: (m, n)),
        out_shape=jax.ShapeDtypeStruct((t, d), BF16),
        scratch_shapes=[pltpu.VMEM((tm, tn), F32)],
        compiler_params=_cparams("parallel", "parallel", "arbitrary"), name="merge",
    )(h, w_bgate, b_bgate3, ys, w_branch)


def _mm_resid_kernel(a_ref, w_ref, x_ref, g_ref, o_ref):
    o_ref[...] = x_ref[...] + g_ref[...] * _dot(a_ref[...], w_ref[...].astype(BF16))


def _out_proj(m, w_out, x, mod3, l, which_gate, tm=ROW_TILE, tn=512):
    t, k = m.shape
    n = w_out.shape[-1]
    return pl.pallas_call(
        _mm_resid_kernel, grid=(t // tm, n // tn),
        in_specs=[pl.BlockSpec((tm, k), lambda i, j: (i, 0)),
                  pl.BlockSpec((None, k, tn), lambda i, j: (l, 0, j)),
                  pl.BlockSpec((tm, tn), lambda i, j: (i, j)),
                  pl.BlockSpec((None, 1, tn), lambda i, j: (_mod_group(i, tm) * N_MOD + which_gate, 0, j))],
        out_specs=pl.BlockSpec((tm, tn), lambda i, j: (i, j)),
        out_shape=jax.ShapeDtypeStruct((t, n), F32),
        compiler_params=_cparams("parallel", "parallel"), name="out_proj",
    )(m, w_out, x, mod3)


def _ffn_up_kernel(e_ref, nb_ref, x_ref, wg_ref, wu_ref, o_ref):
    @pl.when(pl.program_id(1) < nb_ref[0])
    def _():
        x = x_ref[...]
        g = _dot(x, wg_ref[...].astype(BF16))
        u = _dot(x, wu_ref[...].astype(BF16))
        o_ref[...] = (g * jax.nn.sigmoid(g) * u).astype(o_ref.dtype)

    @pl.when(pl.program_id(1) >= nb_ref[0])
    def _():
        o_ref[...] = jnp.zeros_like(o_ref)


def _ffn_up(xg, blk_e, n_used, w_gate, w_up, l, bm, tn=256):
    r, d = xg.shape
    f = w_gate.shape[-1]
    w_spec = pl.BlockSpec((None, None, d, tn), lambda j, b, e, nb: (l, e[b], 0, j))
    return pl.pallas_call(
        _ffn_up_kernel,
        grid_spec=pltpu.PrefetchScalarGridSpec(
            num_scalar_prefetch=2, grid=(f // tn, r // bm),
            in_specs=[pl.BlockSpec((bm, d), lambda j, b, e, nb: (b, 0)), w_spec, w_spec],
            out_specs=pl.BlockSpec((bm, tn), lambda j, b, e, nb: (b, j))),
        out_shape=jax.ShapeDtypeStruct((r, f), BF16),
        compiler_params=_cparams("parallel", "arbitrary"), name="ffn_up",
    )(blk_e, n_used, xg, w_gate, w_up)


def _ffn_down_kernel(e_ref, nb_ref, a_ref, w_ref, o_ref):
    @pl.when(pl.program_id(1) < nb_ref[0])
    def _():
        o_ref[...] = _dot(a_ref[...], w_ref[...].astype(BF16))

    @pl.when(pl.program_id(1) >= nb_ref[0])
    def _():
        o_ref[...] = jnp.zeros_like(o_ref)


def _ffn_down(a, blk_e, n_used, w_down, l, bm, tn=1024):
    r, f = a.shape
    d = w_down.shape[-1]
    return pl.pallas_call(
        _ffn_down_kernel,
        grid_spec=pltpu.PrefetchScalarGridSpec(
            num_scalar_prefetch=2, grid=(d // tn, r // bm),
            in_specs=[pl.BlockSpec((bm, f), lambda j, b, e, nb: (b, 0)),
                      pl.BlockSpec((None, None, f, tn), lambda j, b, e, nb: (l, e[b], 0, j))],
            out_specs=pl.BlockSpec((bm, tn), lambda j, b, e, nb: (b, j))),
        out_shape=jax.ShapeDtypeStruct((r, d), F32),
        compiler_params=_cparams("parallel", "arbitrary"), name="ffn_down",
    )(blk_e, n_used, a, w_down)


def _rmsnorm(x, g):
    return x * lax.rsqrt(jnp.mean(x * x, axis=-1, keepdims=True) + RMS_EPS) * g


def _split_cols(z, sizes):
    out, o = [], 0
    for s in sizes:
        out.append(z[..., o:o + s])
        o += s
    return out


def _short_conv3(x, w, b):
    L = x.shape[1]
    xp = jnp.pad(x, ((0, 0), (1, 1), (0, 0)))
    return xp[:, :L] * w[0] + xp[:, 1:L + 1] * w[1] + xp[:, 2:] * w[2] + b


def _hyena_filters(L, w1, b1, w2, b2, w3, freq, decay):
    t = jnp.linspace(0.0, 1.0, L, dtype=F32)[:, None]
    w = 2.0 * math.pi * jnp.arange(L, dtype=F32)[:, None] / L
    f = jnp.linspace(1e-4, HY_BANDS - 1, HY_BANDS, dtype=F32)[None, :]
    z = jnp.concatenate([t, jnp.cos(f * w), -jnp.sin(f * w)], axis=-1)
    h = jnp.sin(freq[0] * (z @ w1 + b1))
    h = jnp.sin(freq[1] * (h @ w2 + b2))
    h = (h @ w3) * jnp.exp(-t * jnp.abs(decay))
    h = h / (jnp.sum(jnp.abs(h), axis=0, keepdims=True) + 1e-6)
    return h.reshape(L, 2, 2, HY_W)


def _long_conv_bidir(u, hf, hb, skip):
    L = u.shape[1]
    k = jnp.concatenate([hf, jnp.zeros_like(hf[:1]), hb[:0:-1]], axis=0)
    uf = jnp.fft.rfft(u, n=2 * L, axis=1)
    kf = jnp.fft.rfft(k, n=2 * L, axis=0)
    y = jnp.fft.irfft(uf * kf[None], n=2 * L, axis=1)[:, :L]
    return y + u * skip


def _hyena_mixer(z, p, l):
    L = z.shape[1]
    z = _short_conv3(z, p['hy_conv_w'][l], p['hy_conv_b'][l])
    x1, x2, v = jnp.split(z, 3, axis=-1)
    h = _hyena_filters(L, p['hy_w1'][l], p['hy_b1'][l], p['hy_w2'][l], p['hy_b2'][l],
                       p['hy_w3'][l], p['hy_freq'][l], p['hy_decay'][l])
    skip = p['hy_skip'][l]
    y = x1 * _long_conv_bidir(v, h[:, 0, 0], h[:, 0, 1], skip[0])
    return x2 * _long_conv_bidir(y, h[:, 1, 0], h[:, 1, 1], skip[1])


def _cplx_combine(e1, e2):
    ar1, ai1, br1, bi1 = e1
    ar2, ai2, br2, bi2 = e2
    return (ar2 * ar1 - ai2 * ai1, ar2 * ai1 + ai2 * ar1,
            ar2 * br1 - ai2 * bi1 + br2, ar2 * bi1 + ai2 * br1 + bi2)


def _s5_direction(u, lam_re, lam_im, log_step, b_re, b_im, c_re, c_im, h0_re, h0_im):
    lr = jnp.minimum(lam_re, -1e-4)
    dt = jnp.exp(log_step)[:, None]
    mag, ang = jnp.exp(lr * dt), lam_im * dt
    ar, ai = mag * jnp.cos(ang), mag * jnp.sin(ang)
    den = lr * lr + lam_im * lam_im
    kr = ((ar - 1.0) * lr + ai * lam_im) / den
    ki = (ai * lr - (ar - 1.0) * lam_im) / den
    bb_re = kr[..., None] * b_re - ki[..., None] * b_im
    bb_im = kr[..., None] * b_im + ki[..., None] * b_re
    bu_re = jnp.einsum('blgs,gps->blgp', u, bb_re)
    bu_im = jnp.einsum('blgs,gps->blgp', u, bb_im)
    bu_re = bu_re.at[:, 0].add(ar * h0_re - ai * h0_im)
    bu_im = bu_im.at[:, 0].add(ar * h0_im + ai * h0_re)
    a_re = jnp.broadcast_to(ar, bu_re.shape)
    a_im = jnp.broadcast_to(ai, bu_im.shape)
    _, _, s_re, s_im = lax.associative_scan(_cplx_combine, (a_re, a_im, bu_re, bu_im), axis=1)
    y = jnp.einsum('gsp,blgp->blgs', c_re, s_re) - jnp.einsum('gsp,blgp->blgs', c_im, s_im)
    return y, s_re[:, -1], s_im[:, -1]


def _s5_mixer(u, p, l, h0):
    B, L, _ = u.shape
    ug = u.reshape(B, L, S5_GROUPS, S5_GROUP)
    prm = [p[n][l] for n in ('s5_lam_re', 's5_lam_im', 's5_log_step', 's5_b_re', 's5_b_im', 's5_c_re', 's5_c_im')]
    yf, fr, fi = _s5_direction(ug, *[t[0] for t in prm], h0[:, 0, 0], h0[:, 0, 1])
    yb, br, bi = _s5_direction(ug[:, ::-1], *[t[1] for t in prm], h0[:, 1, 0], h0[:, 1, 1])
    y = yf + yb[:, ::-1] + ug * p['s5_d'][l].reshape(S5_GROUPS, S5_GROUP)
    y = jax.nn.gelu(y.reshape(B, L, S5_W))
    y = y * jax.nn.sigmoid(y @ p['s5_glu_w'][l] + p['s5_glu_b'][l])
    fin = jnp.stack([jnp.stack([fr, fi], 1), jnp.stack([br, bi], 1)], 1)
    return y, fin


def _axial_rope_tables(L):
    n_rows = L // GRID_W
    rows = jnp.repeat(jnp.arange(n_rows, dtype=F32), GRID_W)
    cols = jnp.tile(jnp.arange(GRID_W, dtype=F32), n_rows)
    n_freq = MLA_ROPE // 4
    inv = ROPE_BASE ** (-jnp.arange(n_freq, dtype=F32) / n_freq)
    a_r, a_c = rows[:, None] * inv, cols[:, None] * inv
    ang = jnp.concatenate([a_r, a_r, a_c, a_c], axis=-1)
    return jnp.cos(ang), jnp.sin(ang)


def _apply_rope(x, cos, sin):
    q = MLA_ROPE // 4
    x1, x2, x3, x4 = x[..., :q], x[..., q:2 * q], x[..., 2 * q:3 * q], x[..., 3 * q:]
    rot = jnp.concatenate([-x2, x1, -x4, x3], axis=-1)
    return x * cos + rot * sin


def _mla_expand(c_kv, w_kvb):
    B, L, _ = c_kv.shape
    kv = (c_kv @ w_kvb).reshape(B, L, MLA_HEADS, MLA_NOPE + MLA_V)
    return kv[..., :MLA_NOPE], kv[..., MLA_NOPE:]


def _block_attention(q_nope, q_rope, k_nope, k_rope, v):
    B, Lq = q_nope.shape[:2]
    nb = Lq // Q_BLOCK
    scale = (MLA_NOPE + MLA_ROPE) ** -0.5
    qn = q_nope.reshape(B, nb, Q_BLOCK, MLA_HEADS, MLA_NOPE).swapaxes(0, 1)
    qr = q_rope.reshape(B, nb, Q_BLOCK, MLA_HEADS, MLA_ROPE).swapaxes(0, 1)

    def one_block(args):
        qn_b, qr_b = args
        s = jnp.einsum('bqhd,bkhd->bhqk', qn_b, k_nope) + jnp.einsum('bqhr,bkr->bhqk', qr_b, k_rope)
        pr = jax.nn.softmax(s * scale, axis=-1)
        return jnp.einsum('bhqk,bkhd->bqhd', pr, v)

    o = lax.map(one_block, (qn, qr))
    return o.swapaxes(0, 1).reshape(B, Lq, MLA_HEADS * MLA_V)


def _mla_mixer(q_a, kv_a, k_rope, p, l, ctx):
    B, L, _ = q_a.shape
    q = (_rmsnorm(q_a, p['mla_qa_g'][l]) @ p['mla_wqb'][l]).reshape(B, L, MLA_HEADS, MLA_NOPE + MLA_ROPE)
    q_nope, q_rope = q[..., :MLA_NOPE], q[..., MLA_NOPE:]
    c_kv = _rmsnorm(kv_a, p['mla_kva_g'][l])
    k_nope, v = _mla_expand(c_kv, p['mla_wkvb'][l])
    if ctx is None:
        return _block_attention(q_nope, q_rope, k_nope, k_rope, v), c_kv
    ckv_ctx, kr_ctx = ctx
    cos, sin = _axial_rope_tables(L)
    q_rot = _apply_rope(q_rope, cos[:, None, :], sin[:, None, :])
    k_rot = _apply_rope(k_rope, cos, sin)
    kn_ctx, v_ctx = _mla_expand(ckv_ctx, p['mla_wkvb'][l])
    y = _block_attention(q_nope, q_rot,
                         jnp.concatenate([kn_ctx, k_nope], axis=1),
                         jnp.concatenate([kr_ctx, k_rot], axis=1),
                         jnp.concatenate([v_ctx, v], axis=1))
    return y, c_kv


def _gla_direction(q, k, v, log_a, S0):
    B, L, H, _ = q.shape
    n, C = L // GLA_CHUNK, GLA_CHUNK
    rs = lambda t: t.reshape(B, n, C, H, t.shape[-1])
    q, k, v, log_a = rs(q), rs(k), rs(v), rs(log_a)
    b = jnp.cumsum(log_a, axis=2)
    b_end = b[:, :, -1]
    q_in = q * jnp.exp(b)
    k_in = k * jnp.exp(-b)
    k_end = k * jnp.exp(b_end[:, :, None] - b)
    mask = jnp.tril(jnp.ones((C, C), dtype=bool))
    s = jnp.where(mask, jnp.einsum('bnthd,bnshd->bnhts', q_in, k_in), 0.0)
    o = jnp.einsum('bnhts,bnshe->bnthe', s, v)
    kv = jnp.einsum('bnshd,bnshe->bnhde', k_end, v)
    decay = jnp.exp(b_end)

    def step(S, inp):
        dec, kv_c = inp
        return dec[..., None] * S + kv_c, S

    S_fin, S_start = lax.scan(step, S0, (jnp.moveaxis(decay, 1, 0), jnp.moveaxis(kv, 1, 0)))
    o = o + jnp.einsum('bnthd,nbhde->bnthe', q_in, S_start)
    return o.reshape(B, L, H, GLA_DV), S_fin


def _gla_mixer(q, k, v, r, g_af, g_ab, p, l, S0):
    B, L, _ = q.shape
    heads = lambda t, dd: t.reshape(B, L, GLA_HEADS, dd)
    wa2, ba2 = p['gla_wa2'][l], p['gla_ba2'][l]
    qh = heads(q, GLA_DK) * GLA_DK ** -0.5
    kh, vh = heads(k, GLA_DK), heads(v, GLA_DV)
    la_f = heads(jax.nn.log_sigmoid(g_af @ wa2[0] + ba2[0]), GLA_DK) / GLA_GATE_TEMP
    la_b = heads(jax.nn.log_sigmoid(g_ab @ wa2[1] + ba2[1]), GLA_DK) / GLA_GATE_TEMP
    of, sf = _gla_direction(qh, kh, vh, la_f, S0[:, 0])
    ob, sb = _gla_direction(qh[:, ::-1], kh[:, ::-1], vh[:, ::-1], la_b[:, ::-1], S0[:, 1])
    o = _rmsnorm(of + ob[:, ::-1], p['gla_norm_g'][l]).reshape(B, L, GLA_HEADS * GLA_DV)
    o = o * jax.nn.silu(r)
    return o, jnp.stack([sf, sb], 1)


def _mixers(z, p, l, ctx):
    B = z.shape[0]
    (z_hy, u_s5, q_a, kv_a, k_rope, g_q, g_k, g_v, g_r, g_af, g_ab) = _split_cols(z, IN_SPLITS)
    if ctx is None:
        s5_h0 = jnp.zeros((B, 2, 2, S5_GROUPS, S5_STATE), F32)
        gla_s0 = jnp.zeros((B, 2, GLA_HEADS, GLA_DK, GLA_DV), F32)
        mla_ctx = None
    else:
        ckv_ctx, kr_ctx, s5_h0, gla_s0 = ctx
        mla_ctx = (ckv_ctx, kr_ctx)
    y_hy = _hyena_mixer(z_hy, p, l)
    y_s5, s5_fin = _s5_mixer(u_s5, p, l, s5_h0)
    y_mla, c_kv = _mla_mixer(q_a, kv_a, k_rope, p, l, mla_ctx)
    y_gla, gla_fin = _gla_mixer(g_q, g_k, g_v, g_r, g_af, g_ab, p, l, gla_s0)
    return (y_hy, y_s5, y_mla, y_gla), (c_kv, k_rope, s5_fin, gla_fin)


def _route(logits, router_b):
    t = logits.shape[0]
    scores = jax.nn.sigmoid(logits)
    sel = scores + router_b
    grp = lax.top_k(sel.reshape(t, N_EXPERT_GROUPS, EXPERTS_PER_GROUP), 2)[0].sum(-1)
    _, grp_idx = lax.top_k(grp, TOPK_GROUPS)
    grp_mask = jax.nn.one_hot(grp_idx, N_EXPERT_GROUPS, dtype=F32).sum(-2) > 0
    sel = jnp.where(jnp.repeat(grp_mask, EXPERTS_PER_GROUP, axis=-1), sel, -jnp.inf)
    _, idx = lax.top_k(sel, TOP_K)
    w = jnp.take_along_axis(scores, idx, axis=-1)
    w = w / jnp.sum(w, axis=-1, keepdims=True) * ROUTED_SCALE
    return idx, w


def _dispatch(idx, bm):
    t = idx.shape[0]
    n_asg = t * TOP_K
    e_flat = idx.reshape(-1)
    order = jnp.argsort(e_flat)
    e_s = e_flat[order]
    tok_s = (jnp.arange(n_asg, dtype=jnp.int32) // TOP_K)[order]
    counts = jnp.bincount(e_flat, length=N_EXPERTS)
    start = jnp.cumsum(counts) - counts
    padded = (counts + bm - 1) // bm * bm
    pad_end = jnp.cumsum(padded)
    pad_start = pad_end - padded
    dest_s = pad_start[e_s] + jnp.arange(n_asg, dtype=jnp.int32) - start[e_s]
    n_blocks = n_asg // bm + N_EXPERTS
    n_slots = n_blocks * bm
    slot_tok = jnp.zeros((n_slots,), jnp.int32).at[dest_s].set(tok_s)
    dest = jnp.zeros((n_asg,), jnp.int32).at[order].set(dest_s.astype(jnp.int32))
    blk_e = jnp.minimum(jnp.searchsorted(pad_end, jnp.arange(n_blocks, dtype=jnp.int32) * bm, side='right'),
                        N_EXPERTS - 1).astype(jnp.int32)
    n_used = (pad_end[-1] // bm).astype(jnp.int32).reshape(1)
    return slot_tok, dest.reshape(t, TOP_K), blk_e, n_used


def _moe(h2, logits, p, l):
    t = h2.shape[0]
    idx, w = _route(logits, p['router_b'][l])
    slot_tok, dest, blk_e, n_used = _dispatch(idx, MOE_ROWS)
    xg = h2[slot_tok]
    mid = _ffn_up(xg, blk_e, n_used, p['exp_w_gate'], p['exp_w_up'], l, MOE_ROWS)
    y = _ffn_down(mid, blk_e, n_used, p['exp_w_down'], l, MOE_ROWS)
    routed = jnp.sum(y[dest] * w[..., None], axis=1)
    zero_e = jnp.zeros((t // SHARED_ROWS,), jnp.int32)
    all_used = jnp.full((1,), t // SHARED_ROWS, jnp.int32)
    mid_s = _ffn_up(h2, zero_e, all_used, p['sh_w_gate'][:, None], p['sh_w_up'][:, None], l, SHARED_ROWS)
    shared = _ffn_down(mid_s, zero_e, all_used, p['sh_w_down'][:, None], l, SHARED_ROWS)
    return routed + shared


def _rows_per_token(mod, which):
    m = mod.reshape(N_MOD_GROUPS, N_MOD, D_MODEL)[:, which]
    return jnp.concatenate([jnp.broadcast_to(m[:1], (T_CTX, D_MODEL)),
                            jnp.repeat(m[1:], DEC_SEQ, axis=0)], axis=0)


def kernel(x_prompt, x_sample, c, cache_mla_ckv, cache_mla_krope, state_s5, state_gla, c_ctx, mod_w, mod_b, norm1_g, norm2_g, w_in, hy_conv_w, hy_conv_b, hy_w1, hy_b1, hy_w2, hy_b2, hy_w3, hy_freq, hy_decay, hy_skip, s5_lam_re, s5_lam_im, s5_log_step, s5_b_re, s5_b_im, s5_c_re, s5_c_im, s5_d, s5_glu_w, s5_glu_b, mla_qa_g, mla_wqb, mla_kva_g, mla_wkvb, gla_wa2, gla_ba2, gla_norm_g, w_branch, w_bgate, b_bgate, w_out, router_w, router_b, exp_w_gate, exp_w_up, exp_w_down, sh_w_gate, sh_w_up, sh_w_down, final_g):
    p = dict(locals())
    x = jnp.concatenate([x_prompt.reshape(T_CTX, D_MODEL), x_sample.reshape(T_LAT, D_MODEL)], axis=0)

    cvec = jnp.concatenate([c_ctx[None], c, jnp.zeros((ADALN_ROWS - N_MOD_GROUPS, D_MODEL), F32)], axis=0)
    cvec = jax.nn.silu(cvec).astype(BF16)
    mod_b3 = mod_b.reshape(DEPTH, 1, N_MOD * D_MODEL)
    norm1_g3 = norm1_g.reshape(DEPTH, 1, D_MODEL)
    norm2_g3 = norm2_g.reshape(DEPTH, 1, D_MODEL)
    b_bgate3 = b_bgate.reshape(DEPTH * N_BRANCH, 1, D_MODEL)

    new_ckv, new_kr, new_s5, new_gla = [], [], [], []
    for l in range(DEPTH):
        mod = _adaln(cvec, mod_w, mod_b3, l)[:N_MOD_GROUPS]
        mod3 = mod.reshape(N_MOD_GROUPS * N_MOD, 1, D_MODEL)
        h = _norm_mod(x, norm1_g3, mod3, l, which_scale=1, which_shift=0)
        z = _in_proj(h, w_in, l)
        z_ctx = z[:T_CTX].reshape(BATCH, SEQ, N_IN)
        z_lat = z[T_CTX:].reshape(DEC_BATCH, DEC_SEQ, N_IN)
        ys_c, (ckv, kr, s5f, glaf) = _mixers(z_ctx, p, l, None)
        ys_l, _ = _mixers(z_lat, p, l, (cache_mla_ckv[:, l], cache_mla_krope[:, l], state_s5[:, l], state_gla[:, l]))
        new_ckv.append(ckv)
        new_kr.append(kr)
        new_s5.append(s5f)
        new_gla.append(glaf)
        ys = jnp.stack([jnp.concatenate([a.reshape(T_CTX, BRANCH_W), b.reshape(T_LAT, BRANCH_W)], axis=0)
                        for a, b in zip(ys_c, ys_l)], axis=0).astype(BF16)
        m = _merge(h, ys, w_bgate, b_bgate3, w_branch, l)
        x = _out_proj(m, w_out, x, mod3, l, which_gate=2)
        h2, logits = _norm_mod(x, norm2_g3, mod3, l, which_scale=4, which_shift=3, router_w=router_w)
        x = x + _rows_per_token(mod, 5) * _moe(h2, logits, p, l)

    y = _final_norm(x, final_g.reshape(1, D_MODEL))
    y_prompt = y[:T_CTX].reshape(BATCH, SEQ, D_MODEL)
    y_sample = y[T_CTX:].reshape(DEC_BATCH, DEC_SEQ, D_MODEL)
    return (y_prompt, y_sample, jnp.stack(new_ckv, axis=1), jnp.stack(new_kr, axis=1),
            jnp.stack(new_s5, axis=1), jnp.stack(new_gla, axis=1))
```

```python
import functools
import math

import jax
import jax.numpy as jnp
from jax import lax
from jax.experimental import pallas as pl
from jax.experimental.pallas import tpu as pltpu

D_MODEL = 4096
BATCH = 32
SEQ = 256
DEPTH = 2
DEC_BATCH = 2
DEC_SEQ = 4096
PAST_LEN = 256
GRID_W = 64
RMS_EPS = 1e-6
N_BRANCH = 4
BRANCH_W = 1024
HY_W = BRANCH_W
HY_POS_EMB = 33
HY_BANDS = (HY_POS_EMB - 1) // 2
HY_FFN = 64
S5_W = BRANCH_W
S5_GROUP = 16
S5_GROUPS = S5_W // S5_GROUP
S5_STATE = 64
MLA_HEADS = 8
MLA_NOPE = 128
MLA_ROPE = 64
MLA_V = BRANCH_W // MLA_HEADS
MLA_Q_RANK = 768
MLA_KV_RANK = 512
ROPE_BASE = 10000.0
Q_BLOCK = 128
GLA_HEADS = 4
GLA_DK = 128
GLA_DV = BRANCH_W // GLA_HEADS
GLA_GATE_RANK = 16
GLA_GATE_TEMP = 16.0
GLA_CHUNK = 64
N_EXPERTS = 64
TOP_K = 6
N_EXPERT_GROUPS = 8
EXPERTS_PER_GROUP = N_EXPERTS // N_EXPERT_GROUPS
TOPK_GROUPS = 4
D_EXPERT = 1024
D_SHARED = 1024
ROUTED_SCALE = 2.5
IN_SPLITS = (3 * HY_W, S5_W, MLA_Q_RANK, MLA_KV_RANK, MLA_ROPE,
             GLA_HEADS * GLA_DK, GLA_HEADS * GLA_DK, GLA_HEADS * GLA_DV, GLA_HEADS * GLA_DV,
             GLA_GATE_RANK, GLA_GATE_RANK)
N_IN = sum(IN_SPLITS)

T_CTX = BATCH * SEQ
T_LAT = DEC_BATCH * DEC_SEQ
T_ALL = T_CTX + T_LAT
N_MOD_GROUPS = 1 + DEC_BATCH
N_MOD = 6

F32 = jnp.float32
BF16 = jnp.bfloat16

VMEM_LIMIT_BYTES = 56 * 1024 * 1024
ROW_TILE = 1024
MOE_ROWS = 256
SHARED_ROWS = 512
ADALN_ROWS = 16


def _cparams(*sem):
    return pltpu.CompilerParams(dimension_semantics=sem, vmem_limit_bytes=VMEM_LIMIT_BYTES)


def _mod_group(i, tm):
    n_ctx = T_CTX // tm
    per_req = DEC_SEQ // tm
    return jnp.where(i < n_ctx, 0, 1 + (i - n_ctx) // per_req)


def _dot(a, b):
    return jnp.dot(a, b, preferred_element_type=F32)


def _mm_bias_kernel(a_ref, w_ref, b_ref, o_ref):
    o_ref[...] = _dot(a_ref[...], w_ref[...].astype(BF16)) + b_ref[...]


def _adaln(a, mod_w, mod_b3, l, tn=1024):
    m, k = a.shape
    n = mod_w.shape[-1]
    return pl.pallas_call(
        _mm_bias_kernel,
        grid=(n // tn,),
        in_specs=[pl.BlockSpec((m, k), lambda j: (0, 0)),
                  pl.BlockSpec((None, k, tn), lambda j: (l, 0, j)),
                  pl.BlockSpec((None, 1, tn), lambda j: (l, 0, j))],
        out_specs=pl.BlockSpec((m, tn), lambda j: (0, j)),
        out_shape=jax.ShapeDtypeStruct((m, n), F32),
        compiler_params=_cparams("parallel"),
        name="adaln",
    )(a, mod_w, mod_b3)


def _norm_mod_kernel(x_ref, g_ref, sc_ref, sh_ref, o_ref):
    x = x_ref[...]
    y = x * lax.rsqrt(jnp.mean(x * x, axis=-1, keepdims=True) + RMS_EPS) * g_ref[...]
    o_ref[...] = (y * (1.0 + sc_ref[...]) + sh_ref[...]).astype(o_ref.dtype)


def _norm_mod_router_kernel(x_ref, g_ref, sc_ref, sh_ref, rw_ref, o_ref, lg_ref):
    x = x_ref[...]
    y = x * lax.rsqrt(jnp.mean(x * x, axis=-1, keepdims=True) + RMS_EPS) * g_ref[...]
    h = y * (1.0 + sc_ref[...]) + sh_ref[...]
    o_ref[...] = h.astype(o_ref.dtype)
    lg_ref[...] = jnp.dot(h, rw_ref[...], preferred_element_type=F32, precision=lax.Precision.HIGHEST)


def _norm_mod(x, gain3, mod3, l, which_scale, which_shift, router_w=None, tm=256):
    t, d = x.shape
    in_specs = [pl.BlockSpec((tm, d), lambda i: (i, 0)),
                pl.BlockSpec((None, 1, d), lambda i: (l, 0, 0)),
                pl.BlockSpec((None, 1, d), lambda i: (_mod_group(i, tm) * N_MOD + which_scale, 0, 0)),
                pl.BlockSpec((None, 1, d), lambda i: (_mod_group(i, tm) * N_MOD + which_shift, 0, 0))]
    h_spec = pl.BlockSpec((tm, d), lambda i: (i, 0))
    h_shape = jax.ShapeDtypeStruct((t, d), BF16)
    if router_w is None:
        return pl.pallas_call(
            _norm_mod_kernel, grid=(t // tm,), in_specs=in_specs, out_specs=h_spec, out_shape=h_shape,
            compiler_params=_cparams("parallel"), name="norm_mod",
        )(x, gain3, mod3, mod3)
    return pl.pallas_call(
        _norm_mod_router_kernel, grid=(t // tm,),
        in_specs=in_specs + [pl.BlockSpec((None, d, N_EXPERTS), lambda i: (l, 0, 0))],
        out_specs=(h_spec, pl.BlockSpec((tm, N_EXPERTS), lambda i: (i, 0))),
        out_shape=(h_shape, jax.ShapeDtypeStruct((t, N_EXPERTS), F32)),
        compiler_params=_cparams("parallel"), name="norm_mod_router",
    )(x, gain3, mod3, mod3, router_w)


def _final_norm_kernel(x_ref, g_ref, o_ref):
    x = x_ref[...]
    o_ref[...] = x * lax.rsqrt(jnp.mean(x * x, axis=-1, keepdims=True) + RMS_EPS) * g_ref[...]


def _final_norm(x, g2, tm=256):
    t, d = x.shape
    return pl.pallas_call(
        _final_norm_kernel, grid=(t // tm,),
        in_specs=[pl.BlockSpec((tm, d), lambda i: (i, 0)), pl.BlockSpec((1, d), lambda i: (0, 0))],
        out_specs=pl.BlockSpec((tm, d), lambda i: (i, 0)),
        out_shape=jax.ShapeDtypeStruct((t, d), F32),
        compiler_params=_cparams("parallel"), name="final_norm",
    )(x, g2)


def _mm_kernel(a_ref, w_ref, o_ref):
    o_ref[...] = _dot(a_ref[...], w_ref[...].astype(BF16)).astype(o_ref.dtype)


def _in_proj(h, w_in, l, tm=ROW_TILE, tn=512):
    t, k = h.shape
    n = w_in.shape[-1]
    return pl.pallas_call(
        _mm_kernel, grid=(t // tm, pl.cdiv(n, tn)),
        in_specs=[pl.BlockSpec((tm, k), lambda i, j: (i, 0)),
                  pl.BlockSpec((None, k, tn), lambda i, j: (l, 0, j))],
        out_specs=pl.BlockSpec((tm, tn), lambda i, j: (i, j)),
        out_shape=jax.ShapeDtypeStruct((t, n), F32),
        compiler_params=_cparams("parallel", "parallel"), name="in_proj",
    )(h, w_in)


def _merge_kernel(h_ref, wg_ref, bg_ref, y_ref, wb_ref, o_ref, acc_ref):
    i = pl.program_id(2)
    gate = jax.nn.sigmoid(_dot(h_ref[...], wg_ref[...].astype(BF16)) + bg_ref[...])
    contrib = gate * _dot(y_ref[...], wb_ref[...].astype(BF16))

    @pl.when(i == 0)
    def _():
        acc_ref[...] = contrib

    @pl.when(i > 0)
    def _():
        acc_ref[...] += contrib

    @pl.when(i == N_BRANCH - 1)
    def _():
        o_ref[...] = acc_ref[...].astype(o_ref.dtype)


def _merge(h, ys, w_bgate, b_bgate3, w_branch, l, tm=ROW_TILE, tn=256):
    t, d = h.shape
    bw = ys.shape[-1]
    return pl.pallas_call(
        _merge_kernel, grid=(t // tm, d // tn, N_BRANCH),
        in_specs=[pl.BlockSpec((tm, d), lambda m, n, i: (m, 0)),
                  pl.BlockSpec((None, None, d, tn), lambda m, n, i: (l, i, 0, n)),
                  pl.BlockSpec((None, 1, tn), lambda m, n, i: (l * N_BRANCH + i, 0, n)),
                  pl.BlockSpec((None, tm, bw), lambda m, n, i: (i, m, 0)),
                  pl.BlockSpec((None, None, bw, tn), lambda m, n, i: (l, i, 0, n))],
        out_specs=pl.BlockSpec((tm, tn), lambda m, n, i: (m, n)),
        out_shape=jax.ShapeDtypeStruct((t, d), BF16),
        scratch_shapes=[pltpu.VMEM((tm, tn), F32)],
        compiler_params=_cparams("parallel", "parallel", "arbitrary"), name="merge",
    )(h, w_bgate, b_bgate3, ys, w_branch)


def _mm_resid_kernel(a_ref, w_ref, x_ref, g_ref, o_ref):
    o_ref[...] = x_ref[...] + g_ref[...] * _dot(a_ref[...], w_ref[...].astype(BF16))


def _out_proj(m, w_out, x, mod3, l, which_gate, tm=ROW_TILE, tn=512):
    t, k = m.shape
    n = w_out.shape[-1]
    return pl.pallas_call(
        _mm_resid_kernel, grid=(t // tm, n // tn),
        in_specs=[pl.BlockSpec((tm, k), lambda i, j: (i, 0)),
                  pl.BlockSpec((None, k, tn), lambda i, j: (l, 0, j)),
                  pl.BlockSpec((tm, tn), lambda i, j: (i, j)),
                  pl.BlockSpec((None, 1, tn), lambda i, j: (_mod_group(i, tm) * N_MOD + which_gate, 0, j))],
        out_specs=pl.BlockSpec((tm, tn), lambda i, j: (i, j)),
        out_shape=jax.ShapeDtypeStruct((t, n), F32),
        compiler_params=_cparams("parallel", "parallel"), name="out_proj",
    )(m, w_out, x, mod3)


def _ffn_up_kernel(e_ref, nb_ref, x_ref, wg_ref, wu_ref, o_ref):
    @pl.when(pl.program_id(1) < nb_ref[0])
    def _():
        x = x_ref[...]
        g = _dot(x, wg_ref[...].astype(BF16))
        u = _dot(x, wu_ref[...].astype(BF16))
        o_ref[...] = (g * jax.nn.sigmoid(g) * u).astype(o_ref.dtype)

    @pl.when(pl.program_id(1) >= nb_ref[0])
    def _():
        o_ref[...] = jnp.zeros_like(o_ref)


def _ffn_up(xg, blk_e, n_used, w_gate, w_up, l, bm, tn=256):
    r, d = xg.shape
    f = w_gate.shape[-1]
    w_spec = pl.BlockSpec((None, None, d, tn), lambda j, b, e, nb: (l, e[b], 0, j))
    return pl.pallas_call(
        _ffn_up_kernel,
        grid_spec=pltpu.PrefetchScalarGridSpec(
            num_scalar_prefetch=2, grid=(f // tn, r // bm),
            in_specs=[pl.BlockSpec((bm, d), lambda j, b, e, nb: (b, 0)), w_spec, w_spec],
            out_specs=pl.BlockSpec((bm, tn), lambda j, b, e, nb: (b, j))),
        out_shape=jax.ShapeDtypeStruct((r, f), BF16),
        compiler_params=_cparams("parallel", "arbitrary"), name="ffn_up",
    )(blk_e, n_used, xg, w_gate, w_up)


def _ffn_down_kernel(e_ref, nb_ref, a_ref, w_ref, o_ref):
    @pl.when(pl.program_id(1) < nb_ref[0])
    def _():
        o_ref[...] = _dot(a_ref[...], w_ref[...].astype(BF16))

    @pl.when(pl.program_id(1) >= nb_ref[0])
    def _():
        o_ref[...] = jnp.zeros_like(o_ref)


def _ffn_down(a, blk_e, n_used, w_down, l, bm, tn=1024):
    r, f = a.shape
    d = w_down.shape[-1]
    return pl.pallas_call(
        _ffn_down_kernel,
        grid_spec=pltpu.PrefetchScalarGridSpec(
            num_scalar_prefetch=2, grid=(d // tn, r // bm),
            in_specs=[pl.BlockSpec((bm, f), lambda j, b, e, nb: (b, 0)),
                      pl.BlockSpec((None, None, f, tn), lambda j, b, e, nb: (l, e[b], 0, j))],
            out_specs=pl.BlockSpec((bm, tn), lambda j, b, e, nb: (b, j))),
        out_shape=jax.ShapeDtypeStruct((r, d), F32),
        compiler_params=_cparams("parallel", "arbitrary"), name="ffn_down",
    )(blk_e, n_used, a, w_down)


def _rmsnorm(x, g):
    return x * lax.rsqrt(jnp.mean(x * x, axis=-1, keepdims=True) + RMS_EPS) * g


def _split_cols(z, sizes):
    out, o = [], 0
    for s in sizes:
        out.append(z[..., o:o + s])
        o += s
    return out


def _short_conv3(x, w, b):
    L = x.shape[1]
    xp = jnp.pad(x, ((0, 0), (1, 1), (0, 0)))
    return xp[:, :L] * w[0] + xp[:, 1:L + 1] * w[1] + xp[:, 2:] * w[2] + b


def _hyena_filters(L, w1, b1, w2, b2, w3, freq, decay):
    t = jnp.linspace(0.0, 1.0, L, dtype=F32)[:, None]
    w = 2.0 * math.pi * jnp.arange(L, dtype=F32)[:, None] / L
    f = jnp.linspace(1e-4, HY_BANDS - 1, HY_BANDS, dtype=F32)[None, :]
    z = jnp.concatenate([t, jnp.cos(f * w), -jnp.sin(f * w)], axis=-1)
    h = jnp.sin(freq[0] * (z @ w1 + b1))
    h = jnp.sin(freq[1] * (h @ w2 + b2))
    h = (h @ w3) * jnp.exp(-t * jnp.abs(decay))
    h = h / (jnp.sum(jnp.abs(h), axis=0, keepdims=True) + 1e-6)
    return h.reshape(L, 2, 2, HY_W)


def _long_conv_bidir(u, hf, hb, skip):
    L = u.shape[1]
    k = jnp.concatenate([hf, jnp.zeros_like(hf[:1]), hb[:0:-1]], axis=0)
    uf = jnp.fft.rfft(u, n=2 * L, axis=1)
    kf = jnp.fft.rfft(k, n=2 * L, axis=0)
    y = jnp.fft.irfft(uf * kf[None], n=2 * L, axis=1)[:, :L]
    return y + u * skip


def _hyena_mixer(z, p, l):
    L = z.shape[1]
    z = _short_conv3(z, p['hy_conv_w'][l], p['hy_conv_b'][l])
    x1, x2, v = jnp.split(z, 3, axis=-1)
    h = _hyena_filters(L, p['hy_w1'][l], p['hy_b1'][l], p['hy_w2'][l], p['hy_b2'][l],
                       p['hy_w3'][l], p['hy_freq'][l], p['hy_decay'][l])
    skip = p['hy_skip'][l]
    y = x1 * _long_conv_bidir(v, h[:, 0, 0], h[:, 0, 1], skip[0])
    return x2 * _long_conv_bidir(y, h[:, 1, 0], h[:, 1, 1], skip[1])


S5_SEG = 256
S5_LANES = S5_GROUPS * S5_STATE
LANE = 128
SUBLANE = 8
S5_LANE_TILES = S5_LANES // LANE
S5_CHUNKS = S5_W // LANE
S5_CHUNK_STATES = S5_LANES // S5_CHUNKS
S5_TILES_PER_CHUNK = S5_CHUNK_STATES // LANE
N_SEQ = BATCH + DEC_BATCH
N_SEG = T_ALL // S5_SEG
N_SEG_CTX = T_CTX // S5_SEG
SEGS_PER_CTX = SEQ // S5_SEG
SEGS_PER_LAT = DEC_SEQ // S5_SEG


def _s5_seq_of_seg(seg):
    return jnp.where(seg < N_SEG_CTX, seg // SEGS_PER_CTX, BATCH + (seg - N_SEG_CTX) // SEGS_PER_LAT)


def _s5_pos_in_seq(seg):
    in_ctx = seg < N_SEG_CTX
    pos = jnp.where(in_ctx, seg % SEGS_PER_CTX, (seg - N_SEG_CTX) % SEGS_PER_LAT)
    return pos, jnp.where(in_ctx, SEGS_PER_CTX, SEGS_PER_LAT)


def _s5_kernel(*refs, reverse):
    if reverse:
        (u_ref, h0_ref, sc_ref, bre_ref, bim_ref, cre_ref, cim_ref, yf_ref, d_ref, gw_ref, gb_ref,
         y_ref, fin_ref, bu_re, bu_im, state) = refs
    else:
        (u_ref, h0_ref, sc_ref, bre_ref, bim_ref, cre_ref, cim_ref,
         y_ref, fin_ref, bu_re, bu_im, state) = refs
    i = pl.program_id(0)
    seg = (N_SEG - 1 - i) if reverse else i
    pos, n_pos = _s5_pos_in_seq(seg)
    is_start = (pos == n_pos - 1) if reverse else (pos == 0)

    @pl.when(is_start)
    def _():
        state[:, 0:2, :] = h0_ref[...]

    u = u_ref[...]
    ub = u.astype(BF16)
    for j in range(S5_CHUNKS):
        uj = ub[:, j * LANE:(j + 1) * LANE]
        pre = _dot(uj, bre_ref[j])
        pim = _dot(uj, bim_ref[j])
        for q in range(S5_TILES_PER_CHUNK):
            bu_re[j * S5_TILES_PER_CHUNK + q] = pre[:, q * LANE:(q + 1) * LANE]
            bu_im[j * S5_TILES_PER_CHUNK + q] = pim[:, q * LANE:(q + 1) * LANE]

    n_row_tiles = S5_SEG // SUBLANE

    def lane_body(lt, _):
        mult = [sc_ref[k, lt] for k in range(8)]
        s0 = (state[lt, 0:1, :], state[lt, 1:2, :])

        def row_body(jr, carry):
            s_re, s_im = carry
            rt = (n_row_tiles - 1 - jr) if reverse else jr
            rows = pl.ds(pl.multiple_of(rt * SUBLANE, SUBLANE), SUBLANE)
            xr = bu_re[lt, rows, :]
            xi = bu_im[lt, rows, :]
            for n, k in enumerate((1, 2, 4)):
                mr, mi = mult[2 * n], mult[2 * n + 1]
                shift = (SUBLANE - k) if reverse else k
                rr = pltpu.roll(xr, shift, 0)
                ri = pltpu.roll(xi, shift, 0)
                xr, xi = xr + mr * rr - mi * ri, xi + mr * ri + mi * rr
            pr, pi = mult[6], mult[7]
            br = jnp.broadcast_to(s_re, (SUBLANE, LANE))
            bi = jnp.broadcast_to(s_im, (SUBLANE, LANE))
            xr, xi = xr + pr * br - pi * bi, xi + pr * bi + pi * br
            bu_re[lt, rows, :] = xr
            bu_im[lt, rows, :] = xi
            last = 0 if reverse else SUBLANE - 1
            return xr[last:last + 1, :], xi[last:last + 1, :]

        s_re, s_im = lax.fori_loop(0, n_row_tiles, row_body, s0, unroll=2)
        state[lt, 0:1, :] = s_re
        state[lt, 1:2, :] = s_im
        return 0

    lax.fori_loop(0, S5_LANE_TILES, lane_body, 0)
    fin_ref[...] = state[:, 0:2, :]

    ys = []
    for j in range(S5_CHUNKS):
        tiles = range(j * S5_TILES_PER_CHUNK, (j + 1) * S5_TILES_PER_CHUNK)
        sr = jnp.concatenate([bu_re[t] for t in tiles], axis=-1).astype(BF16)
        si = jnp.concatenate([bu_im[t] for t in tiles], axis=-1).astype(BF16)
        ys.append(_dot(sr, cre_ref[j]) - _dot(si, cim_ref[j]))
    y = jnp.concatenate(ys, axis=-1)
    if reverse:
        y = jax.nn.gelu(y + yf_ref[...] + u * d_ref[...])
        gate = jax.nn.sigmoid(_dot(y.astype(BF16), gw_ref[...].astype(BF16)) + gb_ref[...])
        y_ref[...] = (y * gate).astype(y_ref.dtype)
    else:
        y_ref[...] = y


def _s5_direction_consts(lam_re, lam_im, log_step, b_re, b_im, c_re, c_im, reverse):
    lr = jnp.minimum(lam_re, -1e-4)
    dt = jnp.exp(log_step)[:, None]
    mag, ang = jnp.exp(lr * dt), lam_im * dt
    ar, ai = mag * jnp.cos(ang), mag * jnp.sin(ang)
    den = lr * lr + lam_im * lam_im
    kr = ((ar - 1.0) * lr + ai * lam_im) / den
    ki = (ai * lr - (ar - 1.0) * lam_im) / den
    bb_re = kr[..., None] * b_re - ki[..., None] * b_im
    bb_im = kr[..., None] * b_im + ki[..., None] * b_re
    gpc = S5_GROUPS // S5_CHUNKS
    eye = jnp.eye(gpc, dtype=F32)

    def in_blocks(bb):
        t = bb.reshape(S5_CHUNKS, gpc, S5_STATE, S5_GROUP).transpose(0, 1, 3, 2)
        return jnp.einsum('jgsp,gh->jgshp', t, eye).reshape(S5_CHUNKS, LANE, S5_CHUNK_STATES).astype(BF16)

    def out_blocks(cc):
        t = cc.reshape(S5_CHUNKS, gpc, S5_GROUP, S5_STATE).transpose(0, 1, 3, 2)
        return jnp.einsum('jgps,gh->jgphs', t, eye).reshape(S5_CHUNKS, S5_CHUNK_STATES, LANE).astype(BF16)

    def power(k):
        m = jnp.exp(lr * dt * k)
        return (m * jnp.cos(ang * k)).reshape(-1), (m * jnp.sin(ang * k)).reshape(-1)

    r = jnp.arange(SUBLANE)
    tiles = []
    for k in (1, 2, 4):
        keep = ((r < SUBLANE - k) if reverse else (r >= k)).astype(F32)[:, None]
        pr, pi = power(float(k))
        tiles += [keep * pr[None], keep * pi[None]]
    expo = ((SUBLANE - r) if reverse else (r + 1)).astype(F32)[:, None]
    m = jnp.exp((lr * dt).reshape(-1)[None] * expo)
    a = ang.reshape(-1)[None] * expo
    tiles += [m * jnp.cos(a), m * jnp.sin(a)]
    sc = jnp.stack(tiles).reshape(8, SUBLANE, S5_LANE_TILES, LANE).transpose(0, 2, 1, 3)
    return sc, in_blocks(bb_re), in_blocks(bb_im), out_blocks(c_re), out_blocks(c_im)


def _s5_pass(z, h0, consts, reverse, extra=()):
    seg_of = (lambda i: N_SEG - 1 - i) if reverse else (lambda i: i)
    u_col = (3 * HY_W) // S5_W
    row_spec = lambda col: pl.BlockSpec((S5_SEG, S5_W), lambda i: (seg_of(i), col))
    seq_spec = pl.BlockSpec((None, S5_LANE_TILES, 2, LANE), lambda i: (_s5_seq_of_seg(seg_of(i)), 0, 0, 0))
    whole = lambda a: pl.BlockSpec(a.shape, lambda i: (0,) * a.ndim)
    in_specs = [row_spec(u_col), seq_spec] + [whole(a) for a in consts]
    if reverse:
        yf, d2, glu_w, glu_b2 = extra
        in_specs += [row_spec(0), whole(d2), whole(glu_w), whole(glu_b2)]
    return pl.pallas_call(
        functools.partial(_s5_kernel, reverse=reverse), grid=(N_SEG,),
        in_specs=in_specs,
        out_specs=(row_spec(0), seq_spec),
        out_shape=(jax.ShapeDtypeStruct((T_ALL, S5_W), BF16 if reverse else F32),
                   jax.ShapeDtypeStruct((N_SEQ, S5_LANE_TILES, 2, LANE), F32)),
        scratch_shapes=[pltpu.VMEM((S5_LANE_TILES, S5_SEG, LANE), F32),
                        pltpu.VMEM((S5_LANE_TILES, S5_SEG, LANE), F32),
                        pltpu.VMEM((S5_LANE_TILES, SUBLANE, LANE), F32)],
        compiler_params=_cparams("arbitrary"), name="s5_bwd" if reverse else "s5_fwd",
    )(z, h0, *consts, *extra)


def _s5_mixer(z, p, l, state_lat):
    def tiles(h):
        return h.reshape(N_SEQ, 2, S5_LANE_TILES, LANE).transpose(0, 2, 1, 3)

    def untiles(f):
        return f.transpose(0, 2, 1, 3).reshape(N_SEQ, 2, S5_GROUPS, S5_STATE)

    h0 = jnp.concatenate([jnp.zeros((BATCH, 2, 2, S5_GROUPS, S5_STATE), F32), state_lat], axis=0)
    prm = [p[n][l] for n in ('s5_lam_re', 's5_lam_im', 's5_log_step', 's5_b_re', 's5_b_im', 's5_c_re', 's5_c_im')]
    cf = _s5_direction_consts(*[t[0] for t in prm], reverse=False)
    cb = _s5_direction_consts(*[t[1] for t in prm], reverse=True)
    yf, fin_f = _s5_pass(z, tiles(h0[:, 0]), cf, False)
    extra = (yf, p['s5_d'][l].reshape(1, S5_W), p['s5_glu_w'][l], p['s5_glu_b'][l].reshape(1, S5_W))
    y, fin_b = _s5_pass(z, tiles(h0[:, 1]), cb, True, extra)
    return y, jnp.stack([untiles(fin_f), untiles(fin_b)], axis=1)


def _axial_rope_tables(L):
    n_rows = L // GRID_W
    rows = jnp.repeat(jnp.arange(n_rows, dtype=F32), GRID_W)
    cols = jnp.tile(jnp.arange(GRID_W, dtype=F32), n_rows)
    n_freq = MLA_ROPE // 4
    inv = ROPE_BASE ** (-jnp.arange(n_freq, dtype=F32) / n_freq)
    a_r, a_c = rows[:, None] * inv, cols[:, None] * inv
    ang = jnp.concatenate([a_r, a_r, a_c, a_c], axis=-1)
    return jnp.cos(ang), jnp.sin(ang)


def _apply_rope(x, cos, sin):
    q = MLA_ROPE // 4
    x1, x2, x3, x4 = x[..., :q], x[..., q:2 * q], x[..., 2 * q:3 * q], x[..., 3 * q:]
    rot = jnp.concatenate([-x2, x1, -x4, x3], axis=-1)
    return x * cos + rot * sin


def _mla_expand(c_kv, w_kvb):
    B, L, _ = c_kv.shape
    kv = (c_kv @ w_kvb).reshape(B, L, MLA_HEADS, MLA_NOPE + MLA_V)
    return kv[..., :MLA_NOPE], kv[..., MLA_NOPE:]


def _block_attention(q_nope, q_rope, k_nope, k_rope, v):
    B, Lq = q_nope.shape[:2]
    nb = Lq // Q_BLOCK
    scale = (MLA_NOPE + MLA_ROPE) ** -0.5
    qn = q_nope.reshape(B, nb, Q_BLOCK, MLA_HEADS, MLA_NOPE).swapaxes(0, 1)
    qr = q_rope.reshape(B, nb, Q_BLOCK, MLA_HEADS, MLA_ROPE).swapaxes(0, 1)

    def one_block(args):
        qn_b, qr_b = args
        s = jnp.einsum('bqhd,bkhd->bhqk', qn_b, k_nope) + jnp.einsum('bqhr,bkr->bhqk', qr_b, k_rope)
        pr = jax.nn.softmax(s * scale, axis=-1)
        return jnp.einsum('bhqk,bkhd->bqhd', pr, v)

    o = lax.map(one_block, (qn, qr))
    return o.swapaxes(0, 1).reshape(B, Lq, MLA_HEADS * MLA_V)


def _mla_mixer(q_a, kv_a, k_rope, p, l, ctx):
    B, L, _ = q_a.shape
    q = (_rmsnorm(q_a, p['mla_qa_g'][l]) @ p['mla_wqb'][l]).reshape(B, L, MLA_HEADS, MLA_NOPE + MLA_ROPE)
    q_nope, q_rope = q[..., :MLA_NOPE], q[..., MLA_NOPE:]
    c_kv = _rmsnorm(kv_a, p['mla_kva_g'][l])
    k_nope, v = _mla_expand(c_kv, p['mla_wkvb'][l])
    if ctx is None:
        return _block_attention(q_nope, q_rope, k_nope, k_rope, v), c_kv
    ckv_ctx, kr_ctx = ctx
    cos, sin = _axial_rope_tables(L)
    q_rot = _apply_rope(q_rope, cos[:, None, :], sin[:, None, :])
    k_rot = _apply_rope(k_rope, cos, sin)
    kn_ctx, v_ctx = _mla_expand(ckv_ctx, p['mla_wkvb'][l])
    y = _block_attention(q_nope, q_rot,
                         jnp.concatenate([kn_ctx, k_nope], axis=1),
                         jnp.concatenate([kr_ctx, k_rot], axis=1),
                         jnp.concatenate([v_ctx, v], axis=1))
    return y, c_kv


def _gla_direction(q, k, v, log_a, S0):
    B, L, H, _ = q.shape
    n, C = L // GLA_CHUNK, GLA_CHUNK
    rs = lambda t: t.reshape(B, n, C, H, t.shape[-1])
    q, k, v, log_a = rs(q), rs(k), rs(v), rs(log_a)
    b = jnp.cumsum(log_a, axis=2)
    b_end = b[:, :, -1]
    q_in = q * jnp.exp(b)
    k_in = k * jnp.exp(-b)
    k_end = k * jnp.exp(b_end[:, :, None] - b)
    mask = jnp.tril(jnp.ones((C, C), dtype=bool))
    s = jnp.where(mask, jnp.einsum('bnthd,bnshd->bnhts', q_in, k_in), 0.0)
    o = jnp.einsum('bnhts,bnshe->bnthe', s, v)
    kv = jnp.einsum('bnshd,bnshe->bnhde', k_end, v)
    decay = jnp.exp(b_end)

    def step(S, inp):
        dec, kv_c = inp
        return dec[..., None] * S + kv_c, S

    S_fin, S_start = lax.scan(step, S0, (jnp.moveaxis(decay, 1, 0), jnp.moveaxis(kv, 1, 0)))
    o = o + jnp.einsum('bnthd,nbhde->bnthe', q_in, S_start)
    return o.reshape(B, L, H, GLA_DV), S_fin


def _gla_mixer(q, k, v, r, g_af, g_ab, p, l, S0):
    B, L, _ = q.shape
    heads = lambda t, dd: t.reshape(B, L, GLA_HEADS, dd)
    wa2, ba2 = p['gla_wa2'][l], p['gla_ba2'][l]
    qh = heads(q, GLA_DK) * GLA_DK ** -0.5
    kh, vh = heads(k, GLA_DK), heads(v, GLA_DV)
    la_f = heads(jax.nn.log_sigmoid(g_af @ wa2[0] + ba2[0]), GLA_DK) / GLA_GATE_TEMP
    la_b = heads(jax.nn.log_sigmoid(g_ab @ wa2[1] + ba2[1]), GLA_DK) / GLA_GATE_TEMP
    of, sf = _gla_direction(qh, kh, vh, la_f, S0[:, 0])
    ob, sb = _gla_direction(qh[:, ::-1], kh[:, ::-1], vh[:, ::-1], la_b[:, ::-1], S0[:, 1])
    o = _rmsnorm(of + ob[:, ::-1], p['gla_norm_g'][l]).reshape(B, L, GLA_HEADS * GLA_DV)
    o = o * jax.nn.silu(r)
    return o, jnp.stack([sf, sb], 1)


def _mixers(z, p, l, ctx):
    B = z.shape[0]
    (z_hy, _, q_a, kv_a, k_rope, g_q, g_k, g_v, g_r, g_af, g_ab) = _split_cols(z, IN_SPLITS)
    if ctx is None:
        gla_s0 = jnp.zeros((B, 2, GLA_HEADS, GLA_DK, GLA_DV), F32)
        mla_ctx = None
    else:
        ckv_ctx, kr_ctx, gla_s0 = ctx
        mla_ctx = (ckv_ctx, kr_ctx)
    y_hy = _hyena_mixer(z_hy, p, l)
    y_mla, c_kv = _mla_mixer(q_a, kv_a, k_rope, p, l, mla_ctx)
    y_gla, gla_fin = _gla_mixer(g_q, g_k, g_v, g_r, g_af, g_ab, p, l, gla_s0)
    return (y_hy, y_mla, y_gla), (c_kv, k_rope, gla_fin)


def _route(logits, router_b):
    t = logits.shape[0]
    scores = jax.nn.sigmoid(logits)
    sel = scores + router_b
    grp = lax.top_k(sel.reshape(t, N_EXPERT_GROUPS, EXPERTS_PER_GROUP), 2)[0].sum(-1)
    _, grp_idx = lax.top_k(grp, TOPK_GROUPS)
    grp_mask = jax.nn.one_hot(grp_idx, N_EXPERT_GROUPS, dtype=F32).sum(-2) > 0
    sel = jnp.where(jnp.repeat(grp_mask, EXPERTS_PER_GROUP, axis=-1), sel, -jnp.inf)
    _, idx = lax.top_k(sel, TOP_K)
    w = jnp.take_along_axis(scores, idx, axis=-1)
    w = w / jnp.sum(w, axis=-1, keepdims=True) * ROUTED_SCALE
    return idx, w


RANK_ROWS = 512
RANK_COLS = 8


def _rank_kernel(idx_ref, rank_ref, cnt_ref, carry):
    @pl.when(pl.program_id(0) == 0)
    def _():
        carry[...] = jnp.zeros_like(carry)

    idx = idx_ref[...]
    bt = idx.shape[0]
    lanes = lax.broadcasted_iota(jnp.int32, (bt, LANE), 1)
    hits = [lanes == idx[:, k:k + 1] for k in range(TOP_K)]
    onehot = sum(h.astype(F32) for h in hits)
    earlier = (lax.broadcasted_iota(jnp.int32, (bt, bt), 1) < lax.broadcasted_iota(jnp.int32, (bt, bt), 0))
    before = _dot(earlier.astype(BF16), onehot.astype(BF16)) + carry[...]
    out = jnp.zeros((bt, LANE), F32)
    for k in range(TOP_K):
        out = jnp.where(lanes == k, jnp.sum(jnp.where(hits[k], before, 0.0), axis=-1, keepdims=True), out)
    rank_ref[...] = out.astype(jnp.int32)
    carry[...] += jnp.sum(onehot, axis=0, keepdims=True)
    cnt_ref[...] = carry[...].astype(jnp.int32)


def _expert_ranks(idx):
    t = idx.shape[0]
    idx8 = jnp.concatenate([idx.astype(jnp.int32), jnp.full((t, RANK_COLS - TOP_K), -1, jnp.int32)], axis=1)
    rank, cnt = pl.pallas_call(
        _rank_kernel, grid=(t // RANK_ROWS,),
        in_specs=[pl.BlockSpec((RANK_ROWS, RANK_COLS), lambda i: (i, 0))],
        out_specs=(pl.BlockSpec((RANK_ROWS, LANE), lambda i: (i, 0)), pl.BlockSpec((1, LANE), lambda i: (0, 0))),
        out_shape=(jax.ShapeDtypeStruct((t, LANE), jnp.int32), jax.ShapeDtypeStruct((1, LANE), jnp.int32)),
        scratch_shapes=[pltpu.VMEM((1, LANE), F32)],
        compiler_params=_cparams("arbitrary"), name="expert_ranks",
    )(idx8)
    return rank[:, :TOP_K], cnt[0, :N_EXPERTS]


def _dispatch(idx, bm):
    t = idx.shape[0]
    n_asg = t * TOP_K
    rank, counts = _expert_ranks(idx)
    padded = (counts + bm - 1) // bm * bm
    pad_end = jnp.cumsum(padded)
    pad_start = pad_end - padded
    dest = pad_start[idx] + rank
    n_blocks = n_asg // bm + N_EXPERTS
    n_slots = n_blocks * bm
    tok = jnp.broadcast_to(jnp.arange(t, dtype=jnp.int32)[:, None], (t, TOP_K))
    slot_tok = jnp.zeros((n_slots,), jnp.int32).at[dest.reshape(-1)].set(tok.reshape(-1))
    blk_e = jnp.minimum(jnp.searchsorted(pad_end, jnp.arange(n_blocks, dtype=jnp.int32) * bm, side='right'),
                        N_EXPERTS - 1).astype(jnp.int32)
    n_used = (pad_end[-1] // bm).astype(jnp.int32).reshape(1)
    return slot_tok, dest, blk_e, n_used


def _moe(h2, logits, p, l):
    t = h2.shape[0]
    idx, w = _route(logits, p['router_b'][l])
    slot_tok, dest, blk_e, n_used = _dispatch(idx, MOE_ROWS)
    xg = h2[slot_tok]
    mid = _ffn_up(xg, blk_e, n_used, p['exp_w_gate'], p['exp_w_up'], l, MOE_ROWS)
    y = _ffn_down(mid, blk_e, n_used, p['exp_w_down'], l, MOE_ROWS)
    routed = jnp.sum(y[dest] * w[..., None], axis=1)
    zero_e = jnp.zeros((t // SHARED_ROWS,), jnp.int32)
    all_used = jnp.full((1,), t // SHARED_ROWS, jnp.int32)
    mid_s = _ffn_up(h2, zero_e, all_used, p['sh_w_gate'][:, None], p['sh_w_up'][:, None], l, SHARED_ROWS)
    shared = _ffn_down(mid_s, zero_e, all_used, p['sh_w_down'][:, None], l, SHARED_ROWS)
    return routed + shared


def _rows_per_token(mod, which):
    m = mod.reshape(N_MOD_GROUPS, N_MOD, D_MODEL)[:, which]
    return jnp.concatenate([jnp.broadcast_to(m[:1], (T_CTX, D_MODEL)),
                            jnp.repeat(m[1:], DEC_SEQ, axis=0)], axis=0)


def kernel(x_prompt, x_sample, c, cache_mla_ckv, cache_mla_krope, state_s5, state_gla, c_ctx, mod_w, mod_b, norm1_g, norm2_g, w_in, hy_conv_w, hy_conv_b, hy_w1, hy_b1, hy_w2, hy_b2, hy_w3, hy_freq, hy_decay, hy_skip, s5_lam_re, s5_lam_im, s5_log_step, s5_b_re, s5_b_im, s5_c_re, s5_c_im, s5_d, s5_glu_w, s5_glu_b, mla_qa_g, mla_wqb, mla_kva_g, mla_wkvb, gla_wa2, gla_ba2, gla_norm_g, w_branch, w_bgate, b_bgate, w_out, router_w, router_b, exp_w_gate, exp_w_up, exp_w_down, sh_w_gate, sh_w_up, sh_w_down, final_g):
    p = dict(locals())
    x = jnp.concatenate([x_prompt.reshape(T_CTX, D_MODEL), x_sample.reshape(T_LAT, D_MODEL)], axis=0)

    cvec = jnp.concatenate([c_ctx[None], c, jnp.zeros((ADALN_ROWS - N_MOD_GROUPS, D_MODEL), F32)], axis=0)
    cvec = jax.nn.silu(cvec).astype(BF16)
    mod_b3 = mod_b.reshape(DEPTH, 1, N_MOD * D_MODEL)
    norm1_g3 = norm1_g.reshape(DEPTH, 1, D_MODEL)
    norm2_g3 = norm2_g.reshape(DEPTH, 1, D_MODEL)
    b_bgate3 = b_bgate.reshape(DEPTH * N_BRANCH, 1, D_MODEL)

    new_ckv, new_kr, new_s5, new_gla = [], [], [], []
    for l in range(DEPTH):
        mod = _adaln(cvec, mod_w, mod_b3, l)[:N_MOD_GROUPS]
        mod3 = mod.reshape(N_MOD_GROUPS * N_MOD, 1, D_MODEL)
        h = _norm_mod(x, norm1_g3, mod3, l, which_scale=1, which_shift=0)
        z = _in_proj(h, w_in, l)
        z_ctx = z[:T_CTX].reshape(BATCH, SEQ, N_IN)
        z_lat = z[T_CTX:].reshape(DEC_BATCH, DEC_SEQ, N_IN)
        y_s5, s5f = _s5_mixer(z, p, l, state_s5[:, l])
        ys_c, (ckv, kr, glaf) = _mixers(z_ctx, p, l, None)
        ys_l, _ = _mixers(z_lat, p, l, (cache_mla_ckv[:, l], cache_mla_krope[:, l], state_gla[:, l]))
        new_ckv.append(ckv)
        new_kr.append(kr)
        new_s5.append(s5f[:BATCH])
        new_gla.append(glaf)
        y_hy, y_mla, y_gla = [jnp.concatenate([a.reshape(T_CTX, BRANCH_W), b.reshape(T_LAT, BRANCH_W)],
                                              axis=0).astype(BF16) for a, b in zip(ys_c, ys_l)]
        ys = jnp.stack([y_hy, y_s5, y_mla, y_gla], axis=0)
        m = _merge(h, ys, w_bgate, b_bgate3, w_branch, l)
        x = _out_proj(m, w_out, x, mod3, l, which_gate=2)
        h2, logits = _norm_mod(x, norm2_g3, mod3, l, which_scale=4, which_shift=3, router_w=router_w)
        x = x + _rows_per_token(mod, 5) * _moe(h2, logits, p, l)

    y = _final_norm(x, final_g.reshape(1, D_MODEL))
    y_prompt = y[:T_CTX].reshape(BATCH, SEQ, D_MODEL)
    y_sample = y[T_CTX:].reshape(DEC_BATCH, DEC_SEQ, D_MODEL)
    return (y_prompt, y_sample, jnp.stack(new_ckv, axis=1), jnp.stack(new_kr, axis=1),
            jnp.stack(new_s5, axis=1), jnp.stack(new_gla, axis=1))
```

```python
import functools
import math

import jax
import jax.numpy as jnp
from jax import lax
from jax.experimental import pallas as pl
from jax.experimental.pallas import tpu as pltpu

D_MODEL = 4096
BATCH = 32
SEQ = 256
DEPTH = 2
DEC_BATCH = 2
DEC_SEQ = 4096
PAST_LEN = 256
GRID_W = 64
RMS_EPS = 1e-6
N_BRANCH = 4
BRANCH_W = 1024
HY_W = BRANCH_W
HY_POS_EMB = 33
HY_BANDS = (HY_POS_EMB - 1) // 2
HY_FFN = 64
S5_W = BRANCH_W
S5_GROUP = 16
S5_GROUPS = S5_W // S5_GROUP
S5_STATE = 64
MLA_HEADS = 8
MLA_NOPE = 128
MLA_ROPE = 64
MLA_V = BRANCH_W // MLA_HEADS
MLA_Q_RANK = 768
MLA_KV_RANK = 512
ROPE_BASE = 10000.0
Q_BLOCK = 128
GLA_HEADS = 4
GLA_DK = 128
GLA_DV = BRANCH_W // GLA_HEADS
GLA_GATE_RANK = 16
GLA_GATE_TEMP = 16.0
GLA_CHUNK = 64
N_EXPERTS = 64
TOP_K = 6
N_EXPERT_GROUPS = 8
EXPERTS_PER_GROUP = N_EXPERTS // N_EXPERT_GROUPS
TOPK_GROUPS = 4
D_EXPERT = 1024
D_SHARED = 1024
ROUTED_SCALE = 2.5
IN_SPLITS = (3 * HY_W, S5_W, MLA_Q_RANK, MLA_KV_RANK, MLA_ROPE,
             GLA_HEADS * GLA_DK, GLA_HEADS * GLA_DK, GLA_HEADS * GLA_DV, GLA_HEADS * GLA_DV,
             GLA_GATE_RANK, GLA_GATE_RANK)
N_IN = sum(IN_SPLITS)

T_CTX = BATCH * SEQ
T_LAT = DEC_BATCH * DEC_SEQ
T_ALL = T_CTX + T_LAT
N_MOD_GROUPS = 1 + DEC_BATCH
N_MOD = 6

F32 = jnp.float32
BF16 = jnp.bfloat16

LANE = 128
SUBLANE = 8

Z_HY = 0
Z_S5 = Z_HY + 3 * HY_W
Z_GV = Z_S5 + S5_W
Z_GR = Z_GV + GLA_HEADS * GLA_DV
Z_GQ = Z_GR + GLA_HEADS * GLA_DV
Z_GK = Z_GQ + GLA_HEADS * GLA_DK
Z_QA = Z_GK + GLA_HEADS * GLA_DK
Z_KVA = Z_QA + MLA_Q_RANK
Z_TAIL = Z_KVA + MLA_KV_RANK
TAIL_KROPE = 0
TAIL_GAF = TAIL_KROPE + MLA_ROPE
TAIL_GAB = TAIL_GAF + GLA_GATE_RANK
N_Z = Z_TAIL + LANE
assert Z_TAIL % LANE == 0 and TAIL_GAB + GLA_GATE_RANK <= LANE


def _permute_w_in(w_in):
    o = [0]
    for s in IN_SPLITS:
        o.append(o[-1] + s)
    hy, s5, qa, kva, krope, gq, gk, gv, gr, gaf, gab = range(len(IN_SPLITS))
    parts = [w_in[..., o[i]:o[i + 1]] for i in (hy, s5, gv, gr, gq, gk, qa, kva, krope, gaf, gab)]
    pad = jnp.zeros(w_in.shape[:-1] + (N_Z - N_IN,), w_in.dtype)
    return jnp.concatenate(parts + [pad], axis=-1)


VMEM_LIMIT_BYTES = 56 * 1024 * 1024
ROW_TILE = 1024
MOE_ROWS = 256
SHARED_ROWS = 512
ADALN_ROWS = 16


def _cparams(*sem):
    return pltpu.CompilerParams(dimension_semantics=sem, vmem_limit_bytes=VMEM_LIMIT_BYTES)


def _mod_group(i, tm):
    n_ctx = T_CTX // tm
    per_req = DEC_SEQ // tm
    return jnp.where(i < n_ctx, 0, 1 + (i - n_ctx) // per_req)


def _dot(a, b):
    return jnp.dot(a, b, preferred_element_type=F32)


def _mm_bias_kernel(a_ref, w_ref, b_ref, o_ref):
    o_ref[...] = _dot(a_ref[...], w_ref[...].astype(BF16)) + b_ref[...]


def _adaln(a, mod_w, mod_b3, l, tn=1024):
    m, k = a.shape
    n = mod_w.shape[-1]
    return pl.pallas_call(
        _mm_bias_kernel,
        grid=(n // tn,),
        in_specs=[pl.BlockSpec((m, k), lambda j: (0, 0)),
                  pl.BlockSpec((None, k, tn), lambda j: (l, 0, j)),
                  pl.BlockSpec((None, 1, tn), lambda j: (l, 0, j))],
        out_specs=pl.BlockSpec((m, tn), lambda j: (0, j)),
        out_shape=jax.ShapeDtypeStruct((m, n), F32),
        compiler_params=_cparams("parallel"),
        name="adaln",
    )(a, mod_w, mod_b3)


def _norm_mod_kernel(x_ref, g_ref, sc_ref, sh_ref, o_ref):
    x = x_ref[...]
    y = x * lax.rsqrt(jnp.mean(x * x, axis=-1, keepdims=True) + RMS_EPS) * g_ref[...]
    o_ref[...] = (y * (1.0 + sc_ref[...]) + sh_ref[...]).astype(o_ref.dtype)


def _norm_mod_router_kernel(x_ref, g_ref, sc_ref, sh_ref, rw_ref, o_ref, lg_ref):
    x = x_ref[...]
    y = x * lax.rsqrt(jnp.mean(x * x, axis=-1, keepdims=True) + RMS_EPS) * g_ref[...]
    h = y * (1.0 + sc_ref[...]) + sh_ref[...]
    o_ref[...] = h.astype(o_ref.dtype)
    lg_ref[...] = lax.dot_general(rw_ref[...], h, (((1,), (1,)), ((), ())), preferred_element_type=F32,
                                  precision=lax.Precision.HIGHEST)


def _norm_mod(x, gain3, mod3, l, which_scale, which_shift, router_w=None, tm=256):
    t, d = x.shape
    in_specs = [pl.BlockSpec((tm, d), lambda i: (i, 0)),
                pl.BlockSpec((None, 1, d), lambda i: (l, 0, 0)),
                pl.BlockSpec((None, 1, d), lambda i: (_mod_group(i, tm) * N_MOD + which_scale, 0, 0)),
                pl.BlockSpec((None, 1, d), lambda i: (_mod_group(i, tm) * N_MOD + which_shift, 0, 0))]
    h_spec = pl.BlockSpec((tm, d), lambda i: (i, 0))
    h_shape = jax.ShapeDtypeStruct((t, d), BF16)
    if router_w is None:
        return pl.pallas_call(
            _norm_mod_kernel, grid=(t // tm,), in_specs=in_specs, out_specs=h_spec, out_shape=h_shape,
            compiler_params=_cparams("parallel"), name="norm_mod",
        )(x, gain3, mod3, mod3)
    return pl.pallas_call(
        _norm_mod_router_kernel, grid=(t // tm,),
        in_specs=in_specs + [pl.BlockSpec((None, N_EXPERTS, d), lambda i: (l, 0, 0))],
        out_specs=(h_spec, pl.BlockSpec((N_EXPERTS, tm), lambda i: (0, i))),
        out_shape=(h_shape, jax.ShapeDtypeStruct((N_EXPERTS, t), F32)),
        compiler_params=_cparams("parallel"), name="norm_mod_router",
    )(x, gain3, mod3, mod3, router_w)


def _final_norm_kernel(x_ref, g_ref, o_ref):
    x = x_ref[...]
    o_ref[...] = x * lax.rsqrt(jnp.mean(x * x, axis=-1, keepdims=True) + RMS_EPS) * g_ref[...]


def _final_norm(x, g2, tm=256):
    t, d = x.shape
    return pl.pallas_call(
        _final_norm_kernel, grid=(t // tm,),
        in_specs=[pl.BlockSpec((tm, d), lambda i: (i, 0)), pl.BlockSpec((1, d), lambda i: (0, 0))],
        out_specs=pl.BlockSpec((tm, d), lambda i: (i, 0)),
        out_shape=jax.ShapeDtypeStruct((t, d), F32),
        compiler_params=_cparams("parallel"), name="final_norm",
    )(x, g2)


def _mm_kernel(a_ref, w_ref, o_ref):
    o_ref[...] = _dot(a_ref[...], w_ref[...].astype(BF16)).astype(o_ref.dtype)


def _in_proj(h, w_in, l, tm=ROW_TILE, tn=512):
    t, k = h.shape
    n = w_in.shape[-1]
    return pl.pallas_call(
        _mm_kernel, grid=(t // tm, pl.cdiv(n, tn)),
        in_specs=[pl.BlockSpec((tm, k), lambda i, j: (i, 0)),
                  pl.BlockSpec((None, k, tn), lambda i, j: (l, 0, j))],
        out_specs=pl.BlockSpec((tm, tn), lambda i, j: (i, j)),
        out_shape=jax.ShapeDtypeStruct((t, n), F32),
        compiler_params=_cparams("parallel", "parallel"), name="in_proj",
    )(h, w_in)


def _merge_kernel(h_ref, wg_ref, bg_ref, y_ref, wb_ref, o_ref, acc_ref):
    i = pl.program_id(2)
    gate = jax.nn.sigmoid(_dot(h_ref[...], wg_ref[...].astype(BF16)) + bg_ref[...])
    contrib = gate * _dot(y_ref[...], wb_ref[...].astype(BF16))

    @pl.when(i == 0)
    def _():
        acc_ref[...] = contrib

    @pl.when(i > 0)
    def _():
        acc_ref[...] += contrib

    @pl.when(i == N_BRANCH - 1)
    def _():
        o_ref[...] = acc_ref[...].astype(o_ref.dtype)


def _merge(h, ys, w_bgate, b_bgate3, w_branch, l, tm=ROW_TILE, tn=256):
    t, d = h.shape
    bw = ys.shape[-1]
    return pl.pallas_call(
        _merge_kernel, grid=(t // tm, d // tn, N_BRANCH),
        in_specs=[pl.BlockSpec((tm, d), lambda m, n, i: (m, 0)),
                  pl.BlockSpec((None, None, d, tn), lambda m, n, i: (l, i, 0, n)),
                  pl.BlockSpec((None, 1, tn), lambda m, n, i: (l * N_BRANCH + i, 0, n)),
                  pl.BlockSpec((None, tm, bw), lambda m, n, i: (i, m, 0)),
                  pl.BlockSpec((None, None, bw, tn), lambda m, n, i: (l, i, 0, n))],
        out_specs=pl.BlockSpec((tm, tn), lambda m, n, i: (m, n)),
        out_shape=jax.ShapeDtypeStruct((t, d), BF16),
        scratch_shapes=[pltpu.VMEM((tm, tn), F32)],
        compiler_params=_cparams("parallel", "parallel", "arbitrary"), name="merge",
    )(h, w_bgate, b_bgate3, ys, w_branch)


def _mm_resid_kernel(a_ref, w_ref, x_ref, g_ref, o_ref):
    o_ref[...] = x_ref[...] + g_ref[...] * _dot(a_ref[...], w_ref[...].astype(BF16))


def _out_proj(m, w_out, x, mod3, l, which_gate, tm=ROW_TILE, tn=512):
    t, k = m.shape
    n = w_out.shape[-1]
    return pl.pallas_call(
        _mm_resid_kernel, grid=(t // tm, n // tn),
        in_specs=[pl.BlockSpec((tm, k), lambda i, j: (i, 0)),
                  pl.BlockSpec((None, k, tn), lambda i, j: (l, 0, j)),
                  pl.BlockSpec((tm, tn), lambda i, j: (i, j)),
                  pl.BlockSpec((None, 1, tn), lambda i, j: (_mod_group(i, tm) * N_MOD + which_gate, 0, j))],
        out_specs=pl.BlockSpec((tm, tn), lambda i, j: (i, j)),
        out_shape=jax.ShapeDtypeStruct((t, n), F32),
        compiler_params=_cparams("parallel", "parallel"), name="out_proj",
    )(m, w_out, x, mod3)


def _ffn_up_kernel(e_ref, nb_ref, x_ref, wg_ref, wu_ref, o_ref):
    @pl.when(pl.program_id(1) < nb_ref[0])
    def _():
        x = x_ref[...]
        g = _dot(x, wg_ref[...].astype(BF16))
        u = _dot(x, wu_ref[...].astype(BF16))
        o_ref[...] = (g * jax.nn.sigmoid(g) * u).astype(o_ref.dtype)

    @pl.when(pl.program_id(1) >= nb_ref[0])
    def _():
        o_ref[...] = jnp.zeros_like(o_ref)


def _ffn_up(xg, blk_e, n_used, w_gate, w_up, l, bm, tn=256):
    r, d = xg.shape
    f = w_gate.shape[-1]
    w_spec = pl.BlockSpec((None, None, d, tn), lambda j, b, e, nb: (l, e[b], 0, j))
    return pl.pallas_call(
        _ffn_up_kernel,
        grid_spec=pltpu.PrefetchScalarGridSpec(
            num_scalar_prefetch=2, grid=(f // tn, r // bm),
            in_specs=[pl.BlockSpec((bm, d), lambda j, b, e, nb: (b, 0)), w_spec, w_spec],
            out_specs=pl.BlockSpec((bm, tn), lambda j, b, e, nb: (b, j))),
        out_shape=jax.ShapeDtypeStruct((r, f), BF16),
        compiler_params=_cparams("parallel", "arbitrary"), name="ffn_up",
    )(blk_e, n_used, xg, w_gate, w_up)


def _ffn_down_kernel(e_ref, nb_ref, a_ref, w_ref, o_ref):
    @pl.when(pl.program_id(1) < nb_ref[0])
    def _():
        o_ref[...] = _dot(a_ref[...], w_ref[...].astype(BF16))

    @pl.when(pl.program_id(1) >= nb_ref[0])
    def _():
        o_ref[...] = jnp.zeros_like(o_ref)


def _ffn_down(a, blk_e, n_used, w_down, l, bm, tn=1024):
    r, f = a.shape
    d = w_down.shape[-1]
    return pl.pallas_call(
        _ffn_down_kernel,
        grid_spec=pltpu.PrefetchScalarGridSpec(
            num_scalar_prefetch=2, grid=(d // tn, r // bm),
            in_specs=[pl.BlockSpec((bm, f), lambda j, b, e, nb: (b, 0)),
                      pl.BlockSpec((None, None, f, tn), lambda j, b, e, nb: (l, e[b], 0, j))],
            out_specs=pl.BlockSpec((bm, tn), lambda j, b, e, nb: (b, j))),
        out_shape=jax.ShapeDtypeStruct((r, d), F32),
        compiler_params=_cparams("parallel", "arbitrary"), name="ffn_down",
    )(blk_e, n_used, a, w_down)


def _short_conv3(x, w, b):
    L = x.shape[1]
    xp = jnp.pad(x, ((0, 0), (1, 1), (0, 0)))
    return xp[:, :L] * w[0] + xp[:, 1:L + 1] * w[1] + xp[:, 2:] * w[2] + b


def _hyena_filters(L, w1, b1, w2, b2, w3, freq, decay):
    t = jnp.linspace(0.0, 1.0, L, dtype=F32)[:, None]
    w = 2.0 * math.pi * jnp.arange(L, dtype=F32)[:, None] / L
    f = jnp.linspace(1e-4, HY_BANDS - 1, HY_BANDS, dtype=F32)[None, :]
    z = jnp.concatenate([t, jnp.cos(f * w), -jnp.sin(f * w)], axis=-1)
    h = jnp.sin(freq[0] * (z @ w1 + b1))
    h = jnp.sin(freq[1] * (h @ w2 + b2))
    h = (h @ w3) * jnp.exp(-t * jnp.abs(decay))
    h = h / (jnp.sum(jnp.abs(h), axis=0, keepdims=True) + 1e-6)
    return h.reshape(L, 2, 2, HY_W)


def _long_conv_bidir(u, hf, hb, skip):
    L = u.shape[1]
    k = jnp.concatenate([hf, jnp.zeros_like(hf[:1]), hb[:0:-1]], axis=0)
    uf = jnp.fft.rfft(u, n=2 * L, axis=1)
    kf = jnp.fft.rfft(k, n=2 * L, axis=0)
    y = jnp.fft.irfft(uf * kf[None], n=2 * L, axis=1)[:, :L]
    return y + u * skip


def _hyena_mixer(z, p, l):
    L = z.shape[1]
    z = _short_conv3(z, p['hy_conv_w'][l], p['hy_conv_b'][l])
    x1, x2, v = jnp.split(z, 3, axis=-1)
    h = _hyena_filters(L, p['hy_w1'][l], p['hy_b1'][l], p['hy_w2'][l], p['hy_b2'][l],
                       p['hy_w3'][l], p['hy_freq'][l], p['hy_decay'][l])
    skip = p['hy_skip'][l]
    y = x1 * _long_conv_bidir(v, h[:, 0, 0], h[:, 0, 1], skip[0])
    return x2 * _long_conv_bidir(y, h[:, 1, 0], h[:, 1, 1], skip[1])


S5_SEG = 256
S5_LANES = S5_GROUPS * S5_STATE
S5_LANE_TILES = S5_LANES // LANE
S5_CHUNKS = S5_W // LANE
S5_CHUNK_STATES = S5_LANES // S5_CHUNKS
S5_TILES_PER_CHUNK = S5_CHUNK_STATES // LANE
N_SEQ = BATCH + DEC_BATCH
N_SEG = T_ALL // S5_SEG
N_SEG_CTX = T_CTX // S5_SEG
SEGS_PER_CTX = SEQ // S5_SEG
SEGS_PER_LAT = DEC_SEQ // S5_SEG


def _s5_seq_of_seg(seg):
    return jnp.where(seg < N_SEG_CTX, seg // SEGS_PER_CTX, BATCH + (seg - N_SEG_CTX) // SEGS_PER_LAT)


def _s5_pos_in_seq(seg):
    in_ctx = seg < N_SEG_CTX
    pos = jnp.where(in_ctx, seg % SEGS_PER_CTX, (seg - N_SEG_CTX) % SEGS_PER_LAT)
    return pos, jnp.where(in_ctx, SEGS_PER_CTX, SEGS_PER_LAT)


def _s5_kernel(*refs, reverse):
    if reverse:
        (u_ref, h0_ref, sc_ref, bre_ref, bim_ref, cre_ref, cim_ref, yf_ref, d_ref, gw_ref, gb_ref,
         y_ref, fin_ref, bu_re, bu_im, state) = refs
    else:
        (u_ref, h0_ref, sc_ref, bre_ref, bim_ref, cre_ref, cim_ref,
         y_ref, fin_ref, bu_re, bu_im, state) = refs
    i = pl.program_id(0)
    seg = (N_SEG - 1 - i) if reverse else i
    pos, n_pos = _s5_pos_in_seq(seg)
    is_start = (pos == n_pos - 1) if reverse else (pos == 0)

    @pl.when(is_start)
    def _():
        state[:, 0:2, :] = h0_ref[...]

    u = u_ref[...]
    ub = u.astype(BF16)
    for j in range(S5_CHUNKS):
        uj = ub[:, j * LANE:(j + 1) * LANE]
        pre = _dot(uj, bre_ref[j])
        pim = _dot(uj, bim_ref[j])
        for q in range(S5_TILES_PER_CHUNK):
            bu_re[j * S5_TILES_PER_CHUNK + q] = pre[:, q * LANE:(q + 1) * LANE]
            bu_im[j * S5_TILES_PER_CHUNK + q] = pim[:, q * LANE:(q + 1) * LANE]

    n_row_tiles = S5_SEG // SUBLANE

    def lane_body(lt, _):
        mult = [sc_ref[k, lt] for k in range(8)]
        s0 = (state[lt, 0:1, :], state[lt, 1:2, :])

        def row_body(jr, carry):
            s_re, s_im = carry
            rt = (n_row_tiles - 1 - jr) if reverse else jr
            rows = pl.ds(pl.multiple_of(rt * SUBLANE, SUBLANE), SUBLANE)
            xr = bu_re[lt, rows, :]
            xi = bu_im[lt, rows, :]
            for n, k in enumerate((1, 2, 4)):
                mr, mi = mult[2 * n], mult[2 * n + 1]
                shift = (SUBLANE - k) if reverse else k
                rr = pltpu.roll(xr, shift, 0)
                ri = pltpu.roll(xi, shift, 0)
                xr, xi = xr + mr * rr - mi * ri, xi + mr * ri + mi * rr
            pr, pi = mult[6], mult[7]
            br = jnp.broadcast_to(s_re, (SUBLANE, LANE))
            bi = jnp.broadcast_to(s_im, (SUBLANE, LANE))
            xr, xi = xr + pr * br - pi * bi, xi + pr * bi + pi * br
            bu_re[lt, rows, :] = xr
            bu_im[lt, rows, :] = xi
            last = 0 if reverse else SUBLANE - 1
            return xr[last:last + 1, :], xi[last:last + 1, :]

        s_re, s_im = lax.fori_loop(0, n_row_tiles, row_body, s0, unroll=2)
        state[lt, 0:1, :] = s_re
        state[lt, 1:2, :] = s_im
        return 0

    lax.fori_loop(0, S5_LANE_TILES, lane_body, 0)
    fin_ref[...] = state[:, 0:2, :]

    ys = []
    for j in range(S5_CHUNKS):
        tiles = range(j * S5_TILES_PER_CHUNK, (j + 1) * S5_TILES_PER_CHUNK)
        sr = jnp.concatenate([bu_re[t] for t in tiles], axis=-1).astype(BF16)
        si = jnp.concatenate([bu_im[t] for t in tiles], axis=-1).astype(BF16)
        ys.append(_dot(sr, cre_ref[j]) - _dot(si, cim_ref[j]))
    y = jnp.concatenate(ys, axis=-1)
    if reverse:
        y = jax.nn.gelu(y + yf_ref[...] + u * d_ref[...])
        gate = jax.nn.sigmoid(_dot(y.astype(BF16), gw_ref[...].astype(BF16)) + gb_ref[...])
        y_ref[...] = (y * gate).astype(y_ref.dtype)
    else:
        y_ref[...] = y


def _s5_direction_consts(lam_re, lam_im, log_step, b_re, b_im, c_re, c_im, reverse):
    lr = jnp.minimum(lam_re, -1e-4)
    dt = jnp.exp(log_step)[:, None]
    mag, ang = jnp.exp(lr * dt), lam_im * dt
    ar, ai = mag * jnp.cos(ang), mag * jnp.sin(ang)
    den = lr * lr + lam_im * lam_im
    kr = ((ar - 1.0) * lr + ai * lam_im) / den
    ki = (ai * lr - (ar - 1.0) * lam_im) / den
    bb_re = kr[..., None] * b_re - ki[..., None] * b_im
    bb_im = kr[..., None] * b_im + ki[..., None] * b_re
    gpc = S5_GROUPS // S5_CHUNKS
    eye = jnp.eye(gpc, dtype=F32)

    def in_blocks(bb):
        t = bb.reshape(S5_CHUNKS, gpc, S5_STATE, S5_GROUP).transpose(0, 1, 3, 2)
        return jnp.einsum('jgsp,gh->jgshp', t, eye).reshape(S5_CHUNKS, LANE, S5_CHUNK_STATES).astype(BF16)

    def out_blocks(cc):
        t = cc.reshape(S5_CHUNKS, gpc, S5_GROUP, S5_STATE).transpose(0, 1, 3, 2)
        return jnp.einsum('jgps,gh->jgphs', t, eye).reshape(S5_CHUNKS, S5_CHUNK_STATES, LANE).astype(BF16)

    def power(k):
        m = jnp.exp(lr * dt * k)
        return (m * jnp.cos(ang * k)).reshape(-1), (m * jnp.sin(ang * k)).reshape(-1)

    r = jnp.arange(SUBLANE)
    tiles = []
    for k in (1, 2, 4):
        keep = ((r < SUBLANE - k) if reverse else (r >= k)).astype(F32)[:, None]
        pr, pi = power(float(k))
        tiles += [keep * pr[None], keep * pi[None]]
    expo = ((SUBLANE - r) if reverse else (r + 1)).astype(F32)[:, None]
    m = jnp.exp((lr * dt).reshape(-1)[None] * expo)
    a = ang.reshape(-1)[None] * expo
    tiles += [m * jnp.cos(a), m * jnp.sin(a)]
    sc = jnp.stack(tiles).reshape(8, SUBLANE, S5_LANE_TILES, LANE).transpose(0, 2, 1, 3)
    return sc, in_blocks(bb_re), in_blocks(bb_im), out_blocks(c_re), out_blocks(c_im)


def _s5_pass(z, h0, consts, reverse, extra=()):
    seg_of = (lambda i: N_SEG - 1 - i) if reverse else (lambda i: i)
    u_col = Z_S5 // S5_W
    row_spec = lambda col: pl.BlockSpec((S5_SEG, S5_W), lambda i: (seg_of(i), col))
    seq_spec = pl.BlockSpec((None, S5_LANE_TILES, 2, LANE), lambda i: (_s5_seq_of_seg(seg_of(i)), 0, 0, 0))
    whole = lambda a: pl.BlockSpec(a.shape, lambda i: (0,) * a.ndim)
    in_specs = [row_spec(u_col), seq_spec] + [whole(a) for a in consts]
    if reverse:
        yf, d2, glu_w, glu_b2 = extra
        in_specs += [row_spec(0), whole(d2), whole(glu_w), whole(glu_b2)]
    return pl.pallas_call(
        functools.partial(_s5_kernel, reverse=reverse), grid=(N_SEG,),
        in_specs=in_specs,
        out_specs=(row_spec(0), seq_spec),
        out_shape=(jax.ShapeDtypeStruct((T_ALL, S5_W), BF16 if reverse else F32),
                   jax.ShapeDtypeStruct((N_SEQ, S5_LANE_TILES, 2, LANE), F32)),
        scratch_shapes=[pltpu.VMEM((S5_LANE_TILES, S5_SEG, LANE), F32),
                        pltpu.VMEM((S5_LANE_TILES, S5_SEG, LANE), F32),
                        pltpu.VMEM((S5_LANE_TILES, SUBLANE, LANE), F32)],
        compiler_params=_cparams("arbitrary"), name="s5_bwd" if reverse else "s5_fwd",
    )(z, h0, *consts, *extra)


def _s5_mixer(z, p, l, state_lat):
    def tiles(h):
        return h.reshape(N_SEQ, 2, S5_LANE_TILES, LANE).transpose(0, 2, 1, 3)

    def untiles(f):
        return f.transpose(0, 2, 1, 3).reshape(N_SEQ, 2, S5_GROUPS, S5_STATE)

    h0 = jnp.concatenate([jnp.zeros((BATCH, 2, 2, S5_GROUPS, S5_STATE), F32), state_lat], axis=0)
    prm = [p[n][l] for n in ('s5_lam_re', 's5_lam_im', 's5_log_step', 's5_b_re', 's5_b_im', 's5_c_re', 's5_c_im')]
    cf = _s5_direction_consts(*[t[0] for t in prm], reverse=False)
    cb = _s5_direction_consts(*[t[1] for t in prm], reverse=True)
    yf, fin_f = _s5_pass(z, tiles(h0[:, 0]), cf, False)
    extra = (yf, p['s5_d'][l].reshape(1, S5_W), p['s5_glu_w'][l], p['s5_glu_b'][l].reshape(1, S5_W))
    y, fin_b = _s5_pass(z, tiles(h0[:, 1]), cb, True, extra)
    return y, jnp.stack([untiles(fin_f), untiles(fin_b)], axis=1)


def _axial_rope_tables(L):
    n_rows = L // GRID_W
    rows = jnp.repeat(jnp.arange(n_rows, dtype=F32), GRID_W)
    cols = jnp.tile(jnp.arange(GRID_W, dtype=F32), n_rows)
    n_freq = MLA_ROPE // 4
    inv = ROPE_BASE ** (-jnp.arange(n_freq, dtype=F32) / n_freq)
    a_r, a_c = rows[:, None] * inv, cols[:, None] * inv
    ang = jnp.concatenate([a_r, a_r, a_c, a_c], axis=-1)
    return jnp.cos(ang), jnp.sin(ang)


def _dot_nt(a, b):
    return lax.dot_general(a, b, (((1,), (1,)), ((), ())), preferred_element_type=F32)


def _dot_tn(a, b):
    return lax.dot_general(a, b, (((0,), (0,)), ((), ())), preferred_element_type=F32)


def _rms(x):
    return x * lax.rsqrt(jnp.mean(x * x, axis=-1, keepdims=True) + RMS_EPS)


MLA_ROWS = 256
KV_ROWS = 512
QA_BLOCK = 256
assert Z_QA % QA_BLOCK == 0 and Z_KVA % QA_BLOCK == 0 and MLA_Q_RANK % QA_BLOCK == 0 and MLA_KV_RANK % QA_BLOCK == 0


def _kv_expand_kernel(a0_ref, a1_ref, g_ref, w_ref, ckv_ref, kn_ref, v_ref, *, normalize):
    c = jnp.concatenate([a0_ref[...], a1_ref[...]], axis=-1)
    if normalize:
        c = _rms(c) * g_ref[...]
    ckv_ref[...] = c
    kv = _dot(c.astype(BF16), w_ref[...].astype(BF16))
    half = MLA_HEADS * MLA_NOPE
    kn_ref[...] = kv[:, :half].astype(kn_ref.dtype)
    v_ref[...] = kv[:, half:].astype(v_ref.dtype)


def _kv_expand(src, col0, gain2, w_kvb_p, normalize):
    rows = src.shape[0]
    cb = col0 // QA_BLOCK
    hw = MLA_HEADS * MLA_NOPE
    return pl.pallas_call(
        functools.partial(_kv_expand_kernel, normalize=normalize), grid=(rows // KV_ROWS,),
        in_specs=[pl.BlockSpec((KV_ROWS, QA_BLOCK), lambda i: (i, cb)),
                  pl.BlockSpec((KV_ROWS, QA_BLOCK), lambda i: (i, cb + 1)),
                  pl.BlockSpec((1, MLA_KV_RANK), lambda i: (0, 0)),
                  pl.BlockSpec(w_kvb_p.shape, lambda i: (0, 0))],
        out_specs=(pl.BlockSpec((KV_ROWS, MLA_KV_RANK), lambda i: (i, 0)),
                   pl.BlockSpec((KV_ROWS, hw), lambda i: (i, 0)),
                   pl.BlockSpec((KV_ROWS, hw), lambda i: (i, 0))),
        out_shape=(jax.ShapeDtypeStruct((rows, MLA_KV_RANK), F32),
                   jax.ShapeDtypeStruct((rows, hw), BF16),
                   jax.ShapeDtypeStruct((rows, hw), BF16)),
        compiler_params=_cparams("parallel"), name="kv_expand",
    )(src, src, gain2, w_kvb_p)


def _mla_attn_kernel(*refs, n_parts, rope):
    qa_refs, (qg_ref, wn_ref, wr_ref) = refs[:3], refs[3:6]
    pos = 6
    if rope:
        wrr_ref, cq_ref, sq_ref, ck_ref, sk_ref = refs[pos:pos + 5]
        pos += 5
    parts = [refs[pos + 3 * i: pos + 3 * i + 3] for i in range(n_parts)]
    o_ref = refs[pos + 3 * n_parts]

    qa = jnp.concatenate([r[...] for r in qa_refs], axis=-1)
    qa = (_rms(qa) * qg_ref[...]).astype(BF16)
    qn = _dot(qa, wn_ref[...].astype(BF16)).astype(BF16)
    qr = _dot(qa, wr_ref[...].astype(BF16))
    if rope:
        qr = qr * cq_ref[...] + _dot(qa, wrr_ref[...].astype(BF16)) * sq_ref[...]
    qr = qr.astype(BF16)
    scale = (MLA_NOPE + MLA_ROPE) ** -0.5

    scores = []
    for i, (kn_ref, kr_ref, _) in enumerate(parts):
        kr = kr_ref[...]
        if rope and i == n_parts - 1:
            lane = lax.broadcasted_iota(jnp.int32, kr.shape, 1)
            partner = jnp.where(lane % (MLA_ROPE // 2) < MLA_ROPE // 4,
                                pltpu.roll(kr, LANE - MLA_ROPE // 4, 1), pltpu.roll(kr, MLA_ROPE // 4, 1))
            kr = kr * ck_ref[...] + partner * sk_ref[...]
        scores.append((_dot_nt(qn, kn_ref[...]) + _dot_nt(qr, kr.astype(BF16))) * scale)
    m = scores[0].max(axis=-1, keepdims=True)
    for s in scores[1:]:
        m = jnp.maximum(m, s.max(axis=-1, keepdims=True))
    den = 0.0
    acc = 0.0
    for s, (_, _, v_ref) in zip(scores, parts):
        e = jnp.exp(s - m)
        den = den + e.sum(axis=-1, keepdims=True)
        acc = acc + _dot(e.astype(BF16), v_ref[...])
    o_ref[...] = (acc / den).astype(o_ref.dtype)


def _mla_attention(z, q_row_block0, n_seq, n_qblk, q_consts, rope_consts, parts):
    rope = rope_consts is not None
    qcol = Z_QA // QA_BLOCK
    q_rows = lambda b, h, j: q_row_block0 + b * n_qblk + j
    in_specs = [pl.BlockSpec((MLA_ROWS, QA_BLOCK), lambda b, h, j, c=c: (q_rows(b, h, j), qcol + c))
                for c in range(MLA_Q_RANK // QA_BLOCK)]
    head_w = pl.BlockSpec((None, MLA_Q_RANK, LANE), lambda b, h, j: (h, 0, 0))
    in_specs += [pl.BlockSpec((1, MLA_Q_RANK), lambda b, h, j: (0, 0)), head_w, head_w]
    args = [z] * (MLA_Q_RANK // QA_BLOCK) + list(q_consts)
    if rope:
        w_rot, cos_t, sin_t, sin_signed = rope_consts
        q_tab = pl.BlockSpec((MLA_ROWS, LANE), lambda b, h, j: (j, 0))
        k_tab = pl.BlockSpec(cos_t.shape, lambda b, h, j: (0, 0))
        in_specs += [head_w, q_tab, q_tab, k_tab, k_tab]
        args += [w_rot, cos_t, sin_t, cos_t, sin_signed]
    for kn, kr, kr_col, v, n_keys, row_block in parts:
        in_specs += [pl.BlockSpec((n_keys, LANE), lambda b, h, j, rb=row_block: (rb(b), h)),
                     pl.BlockSpec((n_keys, LANE), lambda b, h, j, rb=row_block, cc=kr_col: (rb(b), cc)),
                     pl.BlockSpec((n_keys, LANE), lambda b, h, j, rb=row_block: (rb(b), h))]
        args += [kn, kr, v]
    return pl.pallas_call(
        functools.partial(_mla_attn_kernel, n_parts=len(parts), rope=rope),
        grid=(n_seq, MLA_HEADS, n_qblk), in_specs=in_specs,
        out_specs=pl.BlockSpec((MLA_ROWS, LANE), lambda b, h, j: (b * n_qblk + j, h)),
        out_shape=jax.ShapeDtypeStruct((n_seq * n_qblk * MLA_ROWS, MLA_HEADS * MLA_V), BF16),
        compiler_params=_cparams("parallel", "parallel", "arbitrary"),
        name="mla_attn_rope" if rope else "mla_attn",
    )(*args)


def _rope_rot_cols(w):
    q = MLA_ROPE // 4
    return jnp.concatenate([-w[..., q:2 * q], w[..., :q], -w[..., 3 * q:], w[..., 2 * q:3 * q]], axis=-1)


def _mla_mixer(z, p, l, ckv_cache, kr_cache):
    H = MLA_HEADS
    wq = p['mla_wqb'][l].reshape(MLA_Q_RANK, H, MLA_NOPE + MLA_ROPE).transpose(1, 0, 2)
    w_nope = wq[..., :MLA_NOPE]
    lane_pad = lambda w: jnp.concatenate([w, jnp.zeros(w.shape[:-1] + (LANE - MLA_ROPE,), F32)], axis=-1)
    w_rope = lane_pad(wq[..., MLA_NOPE:])
    w_rope_rot = lane_pad(_rope_rot_cols(wq[..., MLA_NOPE:]))
    wkv = p['mla_wkvb'][l].reshape(MLA_KV_RANK, H, MLA_NOPE + MLA_V)
    w_kvb_p = jnp.concatenate([wkv[..., :MLA_NOPE].reshape(MLA_KV_RANK, -1),
                               wkv[..., MLA_NOPE:].reshape(MLA_KV_RANK, -1)], axis=-1)
    qg = p['mla_qa_g'][l].reshape(1, MLA_Q_RANK)
    kvg = p['mla_kva_g'][l].reshape(1, MLA_KV_RANK)

    c_kv, kn, v = _kv_expand(z, Z_KVA, kvg, w_kvb_p, True)
    _, kn_c, v_c = _kv_expand(ckv_cache.reshape(DEC_BATCH * PAST_LEN, MLA_KV_RANK), 0, kvg, w_kvb_p, False)
    kr_c = lane_pad(kr_cache.reshape(DEC_BATCH * PAST_LEN, MLA_ROPE))

    tail = Z_TAIL // LANE
    q_consts = (qg, w_nope, w_rope)
    y_ctx = _mla_attention(z, 0, BATCH, SEQ // MLA_ROWS, q_consts, None,
                           [(kn, z, tail, v, SEQ, lambda b: b)])
    cos, sin = _axial_rope_tables(DEC_SEQ)
    q4 = MLA_ROPE // 4
    sign = jnp.tile(jnp.concatenate([-jnp.ones((q4,), F32), jnp.ones((q4,), F32)]), 2)
    tab = lambda t: jnp.tile(t, (1, LANE // MLA_ROPE))
    lat0 = T_CTX // DEC_SEQ
    y_lat = _mla_attention(z, T_CTX // MLA_ROWS, DEC_BATCH, DEC_SEQ // MLA_ROWS, q_consts,
                           (w_rope_rot, tab(cos), tab(sin), tab(sin * sign)),
                           [(kn_c, kr_c, 0, v_c, PAST_LEN, lambda b: b),
                            (kn, z, tail, v, DEC_SEQ, lambda b: lat0 + b)])
    return jnp.concatenate([y_ctx, y_lat], axis=0), c_kv


GLA_SEG = S5_SEG
GLA_CHUNKS = GLA_SEG // GLA_CHUNK
GLA_QK = GLA_HEADS * GLA_DK
GLA_VW = GLA_HEADS * GLA_DV
assert Z_GQ % GLA_QK == 0 and Z_GK % GLA_QK == 0 and Z_GV % GLA_VW == 0 and Z_GR % GLA_VW == 0


def _gla_kernel(*refs, reverse):
    if reverse:
        (q_ref, k_ref, v_ref, r_ref, t_ref, wa_ref, ba_ref, s0_ref, of_ref, ng_ref,
         o_ref, fin_ref, state) = refs
    else:
        q_ref, k_ref, v_ref, t_ref, wa_ref, ba_ref, s0_ref, o_ref, fin_ref, state = refs
    i = pl.program_id(0)
    seg = (N_SEG - 1 - i) if reverse else i
    pos, n_pos = _s5_pos_in_seq(seg)
    is_start = (pos == n_pos - 1) if reverse else (pos == 0)

    @pl.when(is_start)
    def _():
        state[...] = s0_ref[...]

    log_a = jax.nn.log_sigmoid(_dot(t_ref[...].astype(BF16), wa_ref[...]) + ba_ref[...]) / GLA_GATE_TEMP
    C = GLA_CHUNK
    row = lax.broadcasted_iota(jnp.int32, (C, C), 0)
    col = lax.broadcasted_iota(jnp.int32, (C, C), 1)
    keep = (col >= row) if reverse else (col <= row)
    ones = keep.astype(F32)
    chunk_order = range(GLA_CHUNKS - 1, -1, -1) if reverse else range(GLA_CHUNKS)
    edge = 0 if reverse else C - 1

    head_out = []
    for h in range(GLA_HEADS):
        S = state[h]
        outs = [None] * GLA_CHUNKS
        for c in chunk_order:
            rows = slice(c * C, (c + 1) * C)
            dk = slice(h * GLA_DK, (h + 1) * GLA_DK)
            b = jnp.dot(ones, log_a[rows, dk], preferred_element_type=F32, precision=lax.Precision.HIGHEST)
            b_end = b[edge:edge + 1]
            q = q_ref[rows, dk] * GLA_DK ** -0.5
            k = k_ref[rows, dk]
            v = v_ref[rows, h * GLA_DV:(h + 1) * GLA_DV].astype(BF16)
            q_in = (q * jnp.exp(b)).astype(BF16)
            k_in = (k * jnp.exp(-b)).astype(BF16)
            k_end = (k * jnp.exp(b_end - b)).astype(BF16)
            s = jnp.where(keep, _dot_nt(q_in, k_in), 0.0)
            outs[c] = _dot(s.astype(BF16), v) + _dot_nt(q_in, S.astype(BF16))
            S = S * jnp.exp(b_end) + _dot_tn(v, k_end)
        state[h] = S
        head_out.append(jnp.concatenate(outs, axis=0))
    fin_ref[...] = state[...]

    if reverse:
        res = []
        for h in range(GLA_HEADS):
            dv = slice(h * GLA_DV, (h + 1) * GLA_DV)
            res.append(_rms(head_out[h] + of_ref[:, dv]) * ng_ref[...])
        r = r_ref[...]
        o_ref[...] = (jnp.concatenate(res, axis=-1) * (r * jax.nn.sigmoid(r))).astype(o_ref.dtype)
    else:
        o_ref[...] = jnp.concatenate(head_out, axis=-1)


def _gla_pass(z, s0, wa, ba, reverse, extra=()):
    seg_of = (lambda i: N_SEG - 1 - i) if reverse else (lambda i: i)
    rows = lambda width, col0: pl.BlockSpec((GLA_SEG, width), lambda i: (seg_of(i), col0 // width))
    seq_spec = pl.BlockSpec((None, GLA_HEADS, GLA_DV, GLA_DK), lambda i: (_s5_seq_of_seg(seg_of(i)), 0, 0, 0))
    whole = lambda a: pl.BlockSpec(a.shape, lambda i: (0,) * a.ndim)
    in_specs = [rows(GLA_QK, Z_GQ), rows(GLA_QK, Z_GK), rows(GLA_VW, Z_GV)]
    args = [z, z, z]
    if reverse:
        in_specs.append(rows(GLA_VW, Z_GR))
        args.append(z)
    in_specs += [rows(LANE, Z_TAIL), whole(wa), whole(ba), seq_spec]
    args += [z, wa, ba, s0]
    if reverse:
        of, ng = extra
        in_specs += [rows(GLA_VW, 0), whole(ng)]
        args += [of, ng]
    return pl.pallas_call(
        functools.partial(_gla_kernel, reverse=reverse), grid=(N_SEG,),
        in_specs=in_specs, out_specs=(rows(GLA_VW, 0), seq_spec),
        out_shape=(jax.ShapeDtypeStruct((T_ALL, GLA_VW), BF16 if reverse else F32),
                   jax.ShapeDtypeStruct((N_SEQ, GLA_HEADS, GLA_DV, GLA_DK), F32)),
        scratch_shapes=[pltpu.VMEM((GLA_HEADS, GLA_DV, GLA_DK), F32)],
        compiler_params=_cparams("arbitrary"), name="gla_bwd" if reverse else "gla_fwd",
    )(*args)


def _gla_mixer(z, p, l, state_lat):
    s0 = jnp.concatenate([jnp.zeros((BATCH, 2, GLA_HEADS, GLA_DK, GLA_DV), F32), state_lat], axis=0)
    s0 = s0.transpose(0, 1, 2, 4, 3)

    def gate_w(d, row0):
        w = jnp.zeros((LANE, GLA_QK), F32).at[row0:row0 + GLA_GATE_RANK].set(p['gla_wa2'][l, d])
        return w.astype(BF16), p['gla_ba2'][l, d].reshape(1, GLA_QK)

    of, fin_f = _gla_pass(z, s0[:, 0], *gate_w(0, TAIL_GAF), False)
    ng = p['gla_norm_g'][l].reshape(1, GLA_DV)
    y, fin_b = _gla_pass(z, s0[:, 1], *gate_w(1, TAIL_GAB), True, (of, ng))
    return y, jnp.stack([fin_f, fin_b], axis=1).transpose(0, 1, 2, 4, 3)


ROUTE_COLS = 512
ROUTE_OUT_ROWS = SUBLANE
assert TOP_K <= ROUTE_OUT_ROWS and N_EXPERTS <= LANE and EXPERTS_PER_GROUP == SUBLANE


def _route_kernel(lg_ref, b_ref, idx_ref, w_ref, rank_ref, cnt_ref, carry):
    @pl.when(pl.program_id(0) == 0)
    def _():
        carry[...] = jnp.zeros_like(carry)

    bt = lg_ref.shape[1]
    G, M = N_EXPERT_GROUPS, EXPERTS_PER_GROUP
    neg = -jnp.inf
    scores = jax.nn.sigmoid(lg_ref[...])
    x = (scores + b_ref[...]).reshape(G, M, bt)
    scores = scores.reshape(G, M, bt)
    member = lax.broadcasted_iota(jnp.int32, (G, M, bt), 1).astype(F32)
    group = lax.broadcasted_iota(jnp.int32, (G, M, bt), 0).astype(F32)
    expert = group * M + member

    m1 = jnp.max(x, axis=1, keepdims=True)
    i1 = jnp.min(jnp.where(x == m1, member, float(M)), axis=1, keepdims=True)
    m2 = jnp.max(jnp.where(member == i1, neg, x), axis=1, keepdims=True)
    gs = jnp.broadcast_to(m1 + m2, (G, M, bt))
    beaten = jnp.zeros((G, M, bt), F32)
    for g in range(G):
        o = gs[g]
        beaten = beaten + jnp.where((o > gs) | ((o == gs) & (g < group)), 1.0, 0.0)
    x = jnp.where(beaten < TOPK_GROUPS, x, neg)

    hits, picked_w = [], []
    for _ in range(TOP_K):
        m = jnp.max(jnp.max(x, axis=0), axis=0, keepdims=True)
        e = jnp.min(jnp.min(jnp.where(x == m, expert, float(N_EXPERTS)), axis=0), axis=0, keepdims=True)
        hit = expert == e
        hits.append(hit)
        picked_w.append(jnp.sum(jnp.sum(jnp.where(hit, scores, 0.0), axis=0), axis=0, keepdims=True))
        x = jnp.where(hit, neg, x)
    total = sum(picked_w)

    onehot = sum(jnp.where(h, 1.0, 0.0) for h in hits).reshape(N_EXPERTS, bt)
    earlier = (lax.broadcasted_iota(jnp.int32, (bt, bt), 0) < lax.broadcasted_iota(jnp.int32, (bt, bt), 1))
    before = (_dot(onehot.astype(BF16), earlier.astype(BF16)) + carry[...]).reshape(G, M, bt)

    out_row = lax.broadcasted_iota(jnp.int32, (ROUTE_OUT_ROWS, bt), 0)
    idx_o = jnp.full((ROUTE_OUT_ROWS, bt), -1.0, F32)
    w_o = jnp.zeros((ROUTE_OUT_ROWS, bt), F32)
    rank_o = jnp.zeros((ROUTE_OUT_ROWS, bt), F32)
    for k in range(TOP_K):
        e = jnp.sum(jnp.sum(jnp.where(hits[k], expert, 0.0), axis=0), axis=0, keepdims=True)
        r = jnp.sum(jnp.sum(jnp.where(hits[k], before, 0.0), axis=0), axis=0, keepdims=True)
        idx_o = jnp.where(out_row == k, e, idx_o)
        w_o = jnp.where(out_row == k, picked_w[k] / total * ROUTED_SCALE, w_o)
        rank_o = jnp.where(out_row == k, r, rank_o)
    idx_ref[...] = idx_o.astype(jnp.int32)
    w_ref[...] = w_o
    rank_ref[...] = rank_o.astype(jnp.int32)
    carry[...] += jnp.sum(onehot, axis=1, keepdims=True)
    cnt_ref[...] = jnp.broadcast_to(carry[...], cnt_ref.shape).astype(jnp.int32)


def _route(logits_t, router_b2):
    t = logits_t.shape[1]
    row_spec = pl.BlockSpec((ROUTE_OUT_ROWS, ROUTE_COLS), lambda i: (0, i))
    row_shape = lambda dt: jax.ShapeDtypeStruct((ROUTE_OUT_ROWS, t), dt)
    idx, w, rank, cnt = pl.pallas_call(
        _route_kernel, grid=(t // ROUTE_COLS,),
        in_specs=[pl.BlockSpec((N_EXPERTS, ROUTE_COLS), lambda i: (0, i)),
                  pl.BlockSpec((N_EXPERTS, 1), lambda i: (0, 0))],
        out_specs=(row_spec, row_spec, row_spec, pl.BlockSpec((N_EXPERTS, LANE), lambda i: (0, 0))),
        out_shape=(row_shape(jnp.int32), row_shape(F32), row_shape(jnp.int32),
                   jax.ShapeDtypeStruct((N_EXPERTS, LANE), jnp.int32)),
        scratch_shapes=[pltpu.VMEM((N_EXPERTS, 1), F32)],
        compiler_params=_cparams("arbitrary"), name="route",
    )(logits_t, router_b2)
    return idx[:TOP_K], w[:TOP_K], rank[:TOP_K], cnt[:, 0]


def _dispatch(idx, rank, counts, bm):
    t = idx.shape[1]
    n_asg = t * TOP_K
    padded = (counts + bm - 1) // bm * bm
    pad_end = jnp.cumsum(padded)
    pad_start = pad_end - padded
    dest = pad_start[idx] + rank
    n_blocks = n_asg // bm + N_EXPERTS
    n_slots = n_blocks * bm
    tok = jnp.broadcast_to(jnp.arange(t, dtype=jnp.int32)[None, :], (TOP_K, t))
    slot_tok = jnp.zeros((n_slots,), jnp.int32).at[dest.reshape(-1)].set(tok.reshape(-1))
    blk_e = jnp.minimum(jnp.searchsorted(pad_end, jnp.arange(n_blocks, dtype=jnp.int32) * bm, side='right'),
                        N_EXPERTS - 1).astype(jnp.int32)
    n_used = (pad_end[-1] // bm).astype(jnp.int32).reshape(1)
    return slot_tok, dest, blk_e, n_used


def _moe(h2, logits_t, p, l):
    t = h2.shape[0]
    idx, w, rank, counts = _route(logits_t, p['router_b'][l].reshape(N_EXPERTS, 1))
    slot_tok, dest, blk_e, n_used = _dispatch(idx, rank, counts, MOE_ROWS)
    xg = h2[slot_tok]
    mid = _ffn_up(xg, blk_e, n_used, p['exp_w_gate'], p['exp_w_up'], l, MOE_ROWS)
    y = _ffn_down(mid, blk_e, n_used, p['exp_w_down'], l, MOE_ROWS)
    routed = jnp.sum(y[dest] * w[..., None], axis=0)
    zero_e = jnp.zeros((t // SHARED_ROWS,), jnp.int32)
    all_used = jnp.full((1,), t // SHARED_ROWS, jnp.int32)
    mid_s = _ffn_up(h2, zero_e, all_used, p['sh_w_gate'][:, None], p['sh_w_up'][:, None], l, SHARED_ROWS)
    shared = _ffn_down(mid_s, zero_e, all_used, p['sh_w_down'][:, None], l, SHARED_ROWS)
    return routed + shared


def _rows_per_token(mod, which):
    m = mod.reshape(N_MOD_GROUPS, N_MOD, D_MODEL)[:, which]
    return jnp.concatenate([jnp.broadcast_to(m[:1], (T_CTX, D_MODEL)),
                            jnp.repeat(m[1:], DEC_SEQ, axis=0)], axis=0)


def kernel(x_prompt, x_sample, c, cache_mla_ckv, cache_mla_krope, state_s5, state_gla, c_ctx, mod_w, mod_b, norm1_g, norm2_g, w_in, hy_conv_w, hy_conv_b, hy_w1, hy_b1, hy_w2, hy_b2, hy_w3, hy_freq, hy_decay, hy_skip, s5_lam_re, s5_lam_im, s5_log_step, s5_b_re, s5_b_im, s5_c_re, s5_c_im, s5_d, s5_glu_w, s5_glu_b, mla_qa_g, mla_wqb, mla_kva_g, mla_wkvb, gla_wa2, gla_ba2, gla_norm_g, w_branch, w_bgate, b_bgate, w_out, router_w, router_b, exp_w_gate, exp_w_up, exp_w_down, sh_w_gate, sh_w_up, sh_w_down, final_g):
    p = dict(locals())
    x = jnp.concatenate([x_prompt.reshape(T_CTX, D_MODEL), x_sample.reshape(T_LAT, D_MODEL)], axis=0)

    cvec = jnp.concatenate([c_ctx[None], c, jnp.zeros((ADALN_ROWS - N_MOD_GROUPS, D_MODEL), F32)], axis=0)
    cvec = jax.nn.silu(cvec).astype(BF16)
    mod_b3 = mod_b.reshape(DEPTH, 1, N_MOD * D_MODEL)
    norm1_g3 = norm1_g.reshape(DEPTH, 1, D_MODEL)
    norm2_g3 = norm2_g.reshape(DEPTH, 1, D_MODEL)
    b_bgate3 = b_bgate.reshape(DEPTH * N_BRANCH, 1, D_MODEL)
    w_in_p = _permute_w_in(w_in)
    router_w_t = router_w.transpose(0, 2, 1)

    new_ckv, new_kr, new_s5, new_gla = [], [], [], []
    for l in range(DEPTH):
        mod = _adaln(cvec, mod_w, mod_b3, l)[:N_MOD_GROUPS]
        mod3 = mod.reshape(N_MOD_GROUPS * N_MOD, 1, D_MODEL)
        h = _norm_mod(x, norm1_g3, mod3, l, which_scale=1, which_shift=0)
        z = _in_proj(h, w_in_p, l)
        z_hy = z[:, Z_HY:Z_HY + 3 * HY_W]
        y_hy = jnp.concatenate(
            [_hyena_mixer(z_hy[:T_CTX].reshape(BATCH, SEQ, 3 * HY_W), p, l).reshape(T_CTX, HY_W),
             _hyena_mixer(z_hy[T_CTX:].reshape(DEC_BATCH, DEC_SEQ, 3 * HY_W), p, l).reshape(T_LAT, HY_W)],
            axis=0).astype(BF16)
        y_s5, s5f = _s5_mixer(z, p, l, state_s5[:, l])
        y_mla, c_kv = _mla_mixer(z, p, l, cache_mla_ckv[:, l], cache_mla_krope[:, l])
        y_gla, glaf = _gla_mixer(z, p, l, state_gla[:, l])
        new_ckv.append(c_kv[:T_CTX].reshape(BATCH, SEQ, MLA_KV_RANK))
        new_kr.append(z[:T_CTX, Z_TAIL + TAIL_KROPE:Z_TAIL + TAIL_KROPE + MLA_ROPE].reshape(BATCH, SEQ, MLA_ROPE))
        new_s5.append(s5f[:BATCH])
        new_gla.append(glaf[:BATCH])
        ys = jnp.stack([y_hy, y_s5, y_mla, y_gla], axis=0)
        m = _merge(h, ys, w_bgate, b_bgate3, w_branch, l)
        x = _out_proj(m, w_out, x, mod3, l, which_gate=2)
        h2, logits_t = _norm_mod(x, norm2_g3, mod3, l, which_scale=4, which_shift=3, router_w=router_w_t)
        x = x + _rows_per_token(mod, 5) * _moe(h2, logits_t, p, l)

    y = _final_norm(x, final_g.reshape(1, D_MODEL))
    y_prompt = y[:T_CTX].reshape(BATCH, SEQ, D_MODEL)
    y_sample = y[T_CTX:].reshape(DEC_BATCH, DEC_SEQ, D_MODEL)
    return (y_prompt, y_sample, jnp.stack(new_ckv, axis=1), jnp.stack(new_kr, axis=1),
            jnp.stack(new_s5, axis=1), jnp.stack(new_gla, axis=1))
```

```python
import functools
import math

import jax
import jax.numpy as jnp
from jax import lax
from jax.experimental import pallas as pl
from jax.experimental.pallas import tpu as pltpu

D_MODEL = 4096
BATCH = 32
SEQ = 256
DEPTH = 2
DEC_BATCH = 2
DEC_SEQ = 4096
PAST_LEN = 256
GRID_W = 64
RMS_EPS = 1e-6
N_BRANCH = 4
BRANCH_W = 1024
HY_W = BRANCH_W
HY_POS_EMB = 33
HY_BANDS = (HY_POS_EMB - 1) // 2
HY_FFN = 64
S5_W = BRANCH_W
S5_GROUP = 16
S5_GROUPS = S5_W // S5_GROUP
S5_STATE = 64
MLA_HEADS = 8
MLA_NOPE = 128
MLA_ROPE = 64
MLA_V = BRANCH_W // MLA_HEADS
MLA_Q_RANK = 768
MLA_KV_RANK = 512
ROPE_BASE = 10000.0
Q_BLOCK = 128
GLA_HEADS = 4
GLA_DK = 128
GLA_DV = BRANCH_W // GLA_HEADS
GLA_GATE_RANK = 16
GLA_GATE_TEMP = 16.0
GLA_CHUNK = 64
N_EXPERTS = 64
TOP_K = 6
N_EXPERT_GROUPS = 8
EXPERTS_PER_GROUP = N_EXPERTS // N_EXPERT_GROUPS
TOPK_GROUPS = 4
D_EXPERT = 1024
D_SHARED = 1024
ROUTED_SCALE = 2.5
IN_SPLITS = (3 * HY_W, S5_W, MLA_Q_RANK, MLA_KV_RANK, MLA_ROPE,
             GLA_HEADS * GLA_DK, GLA_HEADS * GLA_DK, GLA_HEADS * GLA_DV, GLA_HEADS * GLA_DV,
             GLA_GATE_RANK, GLA_GATE_RANK)
N_IN = sum(IN_SPLITS)

T_CTX = BATCH * SEQ
T_LAT = DEC_BATCH * DEC_SEQ
T_ALL = T_CTX + T_LAT
N_MOD_GROUPS = 1 + DEC_BATCH
N_MOD = 6

F32 = jnp.float32
BF16 = jnp.bfloat16

LANE = 128
SUBLANE = 8

Z_HY = 0
Z_S5 = Z_HY + 3 * HY_W
Z_GV = Z_S5 + S5_W
Z_GR = Z_GV + GLA_HEADS * GLA_DV
Z_GQ = Z_GR + GLA_HEADS * GLA_DV
Z_GK = Z_GQ + GLA_HEADS * GLA_DK
Z_QA = Z_GK + GLA_HEADS * GLA_DK
Z_KVA = Z_QA + MLA_Q_RANK
Z_TAIL = Z_KVA + MLA_KV_RANK
TAIL_KROPE = 0
TAIL_GAF = TAIL_KROPE + MLA_ROPE
TAIL_GAB = TAIL_GAF + GLA_GATE_RANK
N_Z = Z_TAIL + LANE
assert Z_TAIL % LANE == 0 and TAIL_GAB + GLA_GATE_RANK <= LANE


def _permute_w_in(w_in):
    o = [0]
    for s in IN_SPLITS:
        o.append(o[-1] + s)
    hy, s5, qa, kva, krope, gq, gk, gv, gr, gaf, gab = range(len(IN_SPLITS))
    parts = [w_in[..., o[i]:o[i + 1]] for i in (hy, s5, gv, gr, gq, gk, qa, kva, krope, gaf, gab)]
    pad = jnp.zeros(w_in.shape[:-1] + (N_Z - N_IN,), w_in.dtype)
    return jnp.concatenate(parts + [pad], axis=-1)


VMEM_LIMIT_BYTES = 56 * 1024 * 1024
ROW_TILE = 1024
MOE_ROWS = 512
SHARED_ROWS = 1024
ADALN_ROWS = 16


def _cparams(*sem):
    return pltpu.CompilerParams(dimension_semantics=sem, vmem_limit_bytes=VMEM_LIMIT_BYTES)


def _mod_group(i, tm):
    n_ctx = T_CTX // tm
    per_req = DEC_SEQ // tm
    return jnp.where(i < n_ctx, 0, 1 + (i - n_ctx) // per_req)


def _dot(a, b):
    return jnp.dot(a, b, preferred_element_type=F32)


def _mm_bias_kernel(a_ref, w_ref, b_ref, o_ref):
    o_ref[...] = _dot(a_ref[...], w_ref[...].astype(BF16)) + b_ref[...]


def _adaln(a, mod_w, mod_b3, l, tn=1024):
    m, k = a.shape
    n = mod_w.shape[-1]
    return pl.pallas_call(
        _mm_bias_kernel,
        grid=(n // tn,),
        in_specs=[pl.BlockSpec((m, k), lambda j: (0, 0)),
                  pl.BlockSpec((None, k, tn), lambda j: (l, 0, j)),
                  pl.BlockSpec((None, 1, tn), lambda j: (l, 0, j))],
        out_specs=pl.BlockSpec((m, tn), lambda j: (0, j)),
        out_shape=jax.ShapeDtypeStruct((m, n), F32),
        compiler_params=_cparams("parallel"),
        name="adaln",
    )(a, mod_w, mod_b3)


def _norm_mod_kernel(x_ref, g_ref, sc_ref, sh_ref, o_ref):
    x = x_ref[...]
    y = x * lax.rsqrt(jnp.mean(x * x, axis=-1, keepdims=True) + RMS_EPS) * g_ref[...]
    o_ref[...] = (y * (1.0 + sc_ref[...]) + sh_ref[...]).astype(o_ref.dtype)


def _norm_mod_router_kernel(x_ref, g_ref, sc_ref, sh_ref, rw_ref, o_ref, lg_ref):
    x = x_ref[...]
    y = x * lax.rsqrt(jnp.mean(x * x, axis=-1, keepdims=True) + RMS_EPS) * g_ref[...]
    h = y * (1.0 + sc_ref[...]) + sh_ref[...]
    o_ref[...] = h.astype(o_ref.dtype)
    lg_ref[...] = lax.dot_general(rw_ref[...], h, (((1,), (1,)), ((), ())), preferred_element_type=F32,
                                  precision=lax.Precision.HIGHEST)


def _norm_mod(x, gain3, mod3, l, which_scale, which_shift, router_w=None, tm=256):
    t, d = x.shape
    in_specs = [pl.BlockSpec((tm, d), lambda i: (i, 0)),
                pl.BlockSpec((None, 1, d), lambda i: (l, 0, 0)),
                pl.BlockSpec((None, 1, d), lambda i: (_mod_group(i, tm) * N_MOD + which_scale, 0, 0)),
                pl.BlockSpec((None, 1, d), lambda i: (_mod_group(i, tm) * N_MOD + which_shift, 0, 0))]
    h_spec = pl.BlockSpec((tm, d), lambda i: (i, 0))
    h_shape = jax.ShapeDtypeStruct((t, d), BF16)
    if router_w is None:
        return pl.pallas_call(
            _norm_mod_kernel, grid=(t // tm,), in_specs=in_specs, out_specs=h_spec, out_shape=h_shape,
            compiler_params=_cparams("parallel"), name="norm_mod",
        )(x, gain3, mod3, mod3)
    return pl.pallas_call(
        _norm_mod_router_kernel, grid=(t // tm,),
        in_specs=in_specs + [pl.BlockSpec((None, N_EXPERTS, d), lambda i: (l, 0, 0))],
        out_specs=(h_spec, pl.BlockSpec((N_EXPERTS, tm), lambda i: (0, i))),
        out_shape=(h_shape, jax.ShapeDtypeStruct((N_EXPERTS, t), F32)),
        compiler_params=_cparams("parallel"), name="norm_mod_router",
    )(x, gain3, mod3, mod3, router_w)


def _final_norm_kernel(x_ref, g_ref, o_ref):
    x = x_ref[...]
    o_ref[...] = x * lax.rsqrt(jnp.mean(x * x, axis=-1, keepdims=True) + RMS_EPS) * g_ref[...]


def _final_norm(x, g2, tm=256):
    t, d = x.shape
    return pl.pallas_call(
        _final_norm_kernel, grid=(t // tm,),
        in_specs=[pl.BlockSpec((tm, d), lambda i: (i, 0)), pl.BlockSpec((1, d), lambda i: (0, 0))],
        out_specs=pl.BlockSpec((tm, d), lambda i: (i, 0)),
        out_shape=jax.ShapeDtypeStruct((t, d), F32),
        compiler_params=_cparams("parallel"), name="final_norm",
    )(x, g2)


def _mm_kernel(a_ref, w_ref, o_ref):
    o_ref[...] = _dot(a_ref[...], w_ref[...].astype(BF16)).astype(o_ref.dtype)


def _in_proj(h, w_in, l, tm=ROW_TILE, tn=512):
    t, k = h.shape
    n = w_in.shape[-1]
    return pl.pallas_call(
        _mm_kernel, grid=(t // tm, pl.cdiv(n, tn)),
        in_specs=[pl.BlockSpec((tm, k), lambda i, j: (i, 0)),
                  pl.BlockSpec((None, k, tn), lambda i, j: (l, 0, j))],
        out_specs=pl.BlockSpec((tm, tn), lambda i, j: (i, j)),
        out_shape=jax.ShapeDtypeStruct((t, n), F32),
        compiler_params=_cparams("parallel", "parallel"), name="in_proj",
    )(h, w_in)


def _merge_kernel(h_ref, wg_ref, bg_ref, y_ref, wb_ref, o_ref, acc_ref):
    i = pl.program_id(2)
    gate = jax.nn.sigmoid(_dot(h_ref[...], wg_ref[...].astype(BF16)) + bg_ref[...])
    contrib = gate * _dot(y_ref[...], wb_ref[...].astype(BF16))

    @pl.when(i == 0)
    def _():
        acc_ref[...] = contrib

    @pl.when(i > 0)
    def _():
        acc_ref[...] += contrib

    @pl.when(i == N_BRANCH - 1)
    def _():
        o_ref[...] = acc_ref[...].astype(o_ref.dtype)


def _merge(h, ys, w_bgate, b_bgate3, w_branch, l, tm=ROW_TILE, tn=256):
    t, d = h.shape
    bw = ys.shape[-1]
    return pl.pallas_call(
        _merge_kernel, grid=(t // tm, d // tn, N_BRANCH),
        in_specs=[pl.BlockSpec((tm, d), lambda m, n, i: (m, 0)),
                  pl.BlockSpec((None, None, d, tn), lambda m, n, i: (l, i, 0, n)),
                  pl.BlockSpec((None, 1, tn), lambda m, n, i: (l * N_BRANCH + i, 0, n)),
                  pl.BlockSpec((None, tm, bw), lambda m, n, i: (i, m, 0)),
                  pl.BlockSpec((None, None, bw, tn), lambda m, n, i: (l, i, 0, n))],
        out_specs=pl.BlockSpec((tm, tn), lambda m, n, i: (m, n)),
        out_shape=jax.ShapeDtypeStruct((t, d), BF16),
        scratch_shapes=[pltpu.VMEM((tm, tn), F32)],
        compiler_params=_cparams("parallel", "parallel", "arbitrary"), name="merge",
    )(h, w_bgate, b_bgate3, ys, w_branch)


def _mm_resid_kernel(a_ref, w_ref, x_ref, g_ref, o_ref):
    o_ref[...] = x_ref[...] + g_ref[...] * _dot(a_ref[...], w_ref[...].astype(BF16))


def _out_proj(m, w_out, x, mod3, l, which_gate, tm=ROW_TILE, tn=512):
    t, k = m.shape
    n = w_out.shape[-1]
    return pl.pallas_call(
        _mm_resid_kernel, grid=(t // tm, n // tn),
        in_specs=[pl.BlockSpec((tm, k), lambda i, j: (i, 0)),
                  pl.BlockSpec((None, k, tn), lambda i, j: (l, 0, j)),
                  pl.BlockSpec((tm, tn), lambda i, j: (i, j)),
                  pl.BlockSpec((None, 1, tn), lambda i, j: (_mod_group(i, tm) * N_MOD + which_gate, 0, j))],
        out_specs=pl.BlockSpec((tm, tn), lambda i, j: (i, j)),
        out_shape=jax.ShapeDtypeStruct((t, n), F32),
        compiler_params=_cparams("parallel", "parallel"), name="out_proj",
    )(m, w_out, x, mod3)


def _ffn_up_kernel(e_ref, nb_ref, x_ref, wg_ref, wu_ref, o_ref):
    @pl.when(pl.program_id(1) < nb_ref[0])
    def _():
        x = x_ref[...]
        g = _dot(x, wg_ref[...].astype(BF16))
        u = _dot(x, wu_ref[...].astype(BF16))
        o_ref[...] = (g * jax.nn.sigmoid(g) * u).astype(o_ref.dtype)

    @pl.when(pl.program_id(1) >= nb_ref[0])
    def _():
        o_ref[...] = jnp.zeros_like(o_ref)


def _ffn_up(xg, blk_e, n_used, w_gate, w_up, l, bm, tn=256):
    r, d = xg.shape
    f = w_gate.shape[-1]
    w_spec = pl.BlockSpec((None, None, d, tn), lambda j, b, e, nb: (l, e[b], 0, j))
    return pl.pallas_call(
        _ffn_up_kernel,
        grid_spec=pltpu.PrefetchScalarGridSpec(
            num_scalar_prefetch=2, grid=(f // tn, r // bm),
            in_specs=[pl.BlockSpec((bm, d), lambda j, b, e, nb: (b, 0)), w_spec, w_spec],
            out_specs=pl.BlockSpec((bm, tn), lambda j, b, e, nb: (b, j))),
        out_shape=jax.ShapeDtypeStruct((r, f), BF16),
        compiler_params=_cparams("parallel", "arbitrary"), name="ffn_up",
    )(blk_e, n_used, xg, w_gate, w_up)


def _ffn_down_kernel(e_ref, nb_ref, a_ref, w_ref, o_ref):
    @pl.when(pl.program_id(1) < nb_ref[0])
    def _():
        o_ref[...] = _dot(a_ref[...], w_ref[...].astype(BF16)).astype(o_ref.dtype)

    @pl.when(pl.program_id(1) >= nb_ref[0])
    def _():
        o_ref[...] = jnp.zeros_like(o_ref)


def _ffn_down(a, blk_e, n_used, w_down, l, bm, out_dtype, tn=1024):
    r, f = a.shape
    d = w_down.shape[-1]
    return pl.pallas_call(
        _ffn_down_kernel,
        grid_spec=pltpu.PrefetchScalarGridSpec(
            num_scalar_prefetch=2, grid=(d // tn, r // bm),
            in_specs=[pl.BlockSpec((bm, f), lambda j, b, e, nb: (b, 0)),
                      pl.BlockSpec((None, None, f, tn), lambda j, b, e, nb: (l, e[b], 0, j))],
            out_specs=pl.BlockSpec((bm, tn), lambda j, b, e, nb: (b, j))),
        out_shape=jax.ShapeDtypeStruct((r, d), out_dtype),
        compiler_params=_cparams("parallel", "arbitrary"), name="ffn_down",
    )(blk_e, n_used, a, w_down)


HY_FREQ_TILE = 256
HY_CH_TILE = 256


def _dft_matrices(L):
    n = 2 * L
    k = jnp.arange(L, dtype=jnp.int32)[:, None]
    t = jnp.arange(L, dtype=jnp.int32)[None, :]
    ang = ((k * t) % n).astype(F32) * (2.0 * math.pi / n)
    fr = jnp.cos(ang)
    fi = -jnp.sin(ang)
    fi = fi.at[0].set(jnp.where(t[0] % 2 == 0, 1.0, -1.0))
    fr, fi = fr.astype(BF16), fi.astype(BF16)
    return fr, fi, fr.T, fi.T


def _mm_bf16_kernel(a_ref, b_ref, o_ref):
    o_ref[...] = _dot(a_ref[...], b_ref[...])


def _filter_spectra(f_stack, h, L):
    n = h.shape[1]
    tn = 512
    return pl.pallas_call(
        _mm_bf16_kernel, grid=(2 * L // HY_FREQ_TILE, n // tn),
        in_specs=[pl.BlockSpec((HY_FREQ_TILE, L), lambda i, j: (i, 0)), pl.BlockSpec((L, tn), lambda i, j: (0, j))],
        out_specs=pl.BlockSpec((HY_FREQ_TILE, tn), lambda i, j: (i, j)),
        out_shape=jax.ShapeDtypeStruct((2 * L, n), F32),
        compiler_params=_cparams("parallel", "parallel"), name="hyena_filter_dft",
    )(f_stack, h)


def _hyena_filters(L, w1, b1, w2, b2, w3, freq, decay):
    hp = lax.Precision.HIGHEST
    t = jnp.linspace(0.0, 1.0, L, dtype=F32)[:, None]
    w = 2.0 * math.pi * jnp.arange(L, dtype=F32)[:, None] / L
    f = jnp.linspace(1e-4, HY_BANDS - 1, HY_BANDS, dtype=F32)[None, :]
    z = jnp.concatenate([t, jnp.cos(f * w), -jnp.sin(f * w)], axis=-1)
    h = jnp.sin(freq[0] * (jnp.dot(z, w1, precision=hp) + b1))
    h = jnp.sin(freq[1] * (jnp.dot(h, w2, precision=hp) + b2))
    h = jnp.dot(h, w3, precision=hp) * jnp.exp(-t * jnp.abs(decay))
    h = h / (jnp.sum(jnp.abs(h), axis=0, keepdims=True) + 1e-6)
    return h.reshape(L, 2, 2, HY_W)


def _hyena_spectra(L, dft, p, l):
    fr, fi, _, _ = dft
    h = _hyena_filters(L, p['hy_w1'][l], p['hy_b1'][l], p['hy_w2'][l], p['hy_b2'][l],
                       p['hy_w3'][l], p['hy_freq'][l], p['hy_decay'][l])
    hf = h[:, :, 0].reshape(L, 2 * HY_W)
    hb = h[:, :, 1].reshape(L, 2 * HY_W)
    late = jnp.concatenate([jnp.zeros_like(hb[:1]), hb[:0:-1]], axis=0)
    spec = _filter_spectra(jnp.concatenate([fr, fi], axis=0), jnp.concatenate([hf, late], axis=1).astype(BF16), L)
    k = jnp.arange(L)[:, None]
    sign = jnp.where(k % 2 == 0, 1.0, -1.0)
    kr = spec[:L, :2 * HY_W] + sign * spec[:L, 2 * HY_W:]
    ki = spec[L:, :2 * HY_W] + sign * spec[L:, 2 * HY_W:]
    scale = jnp.where(k == 0, 1.0, 2.0) / (2 * L)
    a = kr * scale
    b = jnp.where(k == 0, 0.0, ki * scale)
    d = jnp.where(k == 0, ki, kr) * scale
    return [tuple(m[:, o * HY_W:(o + 1) * HY_W] for m in (a, b, d)) for o in range(2)]


def _short_conv_rows(x, w_ref, b_ref):
    n = x.shape[0]
    row = lax.broadcasted_iota(jnp.int32, x.shape, 0)
    prev = jnp.where(row == 0, 0.0, pltpu.roll(x, 1, 0))
    nxt = jnp.where(row == n - 1, 0.0, pltpu.roll(x, n - 1, 0))
    return prev * w_ref[0:1, :] + x * w_ref[1:2, :] + nxt * w_ref[2:3, :] + b_ref[...]


def _hyena_conv_kernel(*refs, conv_input):
    if conv_input:
        (a_ref, wa_ref, ba_ref, g_ref, wg_ref, bg_ref, skip_ref, fr_ref, fi_ref, gr_ref, gi_ref,
         sa_ref, sb_ref, sd_ref, o_ref, u_f32, u_bf, acc) = refs
    else:
        (a_ref, g_ref, wg_ref, bg_ref, skip_ref, fr_ref, fi_ref, gr_ref, gi_ref,
         sa_ref, sb_ref, sd_ref, o_ref, u_f32, u_bf, acc) = refs
    f = pl.program_id(2)

    @pl.when(f == 0)
    def _():
        u = a_ref[...]
        if conv_input:
            u = _short_conv_rows(u, wa_ref, ba_ref)
        u_f32[...] = u
        u_bf[...] = u.astype(BF16)
        acc[...] = jnp.zeros_like(acc)

    ub = u_bf[...]
    ur = _dot(fr_ref[...], ub)
    ui = _dot(fi_ref[...], ub)
    pr = ur * sa_ref[...] - ui * sb_ref[...]
    pi = ur * sb_ref[...] + ui * sd_ref[...]
    acc[...] += _dot(gr_ref[...], pr.astype(BF16)) + _dot(gi_ref[...], pi.astype(BF16))

    @pl.when(f == pl.num_programs(2) - 1)
    def _():
        gate = _short_conv_rows(g_ref[...], wg_ref, bg_ref)
        o_ref[...] = (gate * (acc[...] + skip_ref[...] * u_f32[...])).astype(o_ref.dtype)


def _hyena_conv(L, n_seq, tc, dft, spectra, skip2, a, a_row0, a_col0, a_conv, g, g_row0, g_col0, g_conv, out_dtype):
    fr, fi, gr, gi = dft
    n_f = L // HY_FREQ_TILE
    once = pl.Buffered(1) if L > HY_FREQ_TILE else None
    seq = lambda arr_row0, col0: pl.BlockSpec((L, tc), lambda b, c, f: (arr_row0 + b, col0 // tc + c),
                                              **({'pipeline_mode': once} if once else {}))
    chan = lambda rows: pl.BlockSpec((rows, tc), lambda b, c, f: (0, c))
    in_specs, args = [seq(a_row0, a_col0)], [a]
    if a_conv is not None:
        in_specs += [chan(3), chan(1)]
        args += list(a_conv)
    in_specs += [seq(g_row0, g_col0), chan(3), chan(1), chan(1),
                 pl.BlockSpec((HY_FREQ_TILE, L), lambda b, c, f: (f, 0)),
                 pl.BlockSpec((HY_FREQ_TILE, L), lambda b, c, f: (f, 0)),
                 pl.BlockSpec((L, HY_FREQ_TILE), lambda b, c, f: (0, f)),
                 pl.BlockSpec((L, HY_FREQ_TILE), lambda b, c, f: (0, f))]
    args += [g, *g_conv, skip2, fr, fi, gr, gi]
    in_specs += [pl.BlockSpec((HY_FREQ_TILE, tc), lambda b, c, f: (f, c))] * 3
    args += list(spectra)
    return pl.pallas_call(
        functools.partial(_hyena_conv_kernel, conv_input=a_conv is not None),
        grid=(n_seq, HY_W // tc, n_f), in_specs=in_specs,
        out_specs=pl.BlockSpec((L, tc), lambda b, c, f: (b, c), **({'pipeline_mode': once} if once else {})),
        out_shape=jax.ShapeDtypeStruct((n_seq * L, HY_W), out_dtype),
        scratch_shapes=[pltpu.VMEM((L, tc), F32), pltpu.VMEM((L, tc), BF16), pltpu.VMEM((L, tc), F32)],
        compiler_params=_cparams("parallel", "parallel", "arbitrary"), name="hyena_conv",
    )(*args)


def _hyena_mixer(z, p, l, dfts):
    cw, cb = p['hy_conv_w'][l], p['hy_conv_b'][l].reshape(1, 3 * HY_W)
    third = lambda i: (cw[:, i * HY_W:(i + 1) * HY_W], cb[:, i * HY_W:(i + 1) * HY_W])
    skip = p['hy_skip'][l]
    outs = []
    for L, n_seq, row0, tc in ((SEQ, BATCH, 0, HY_W), (DEC_SEQ, DEC_BATCH, T_CTX // DEC_SEQ, HY_CH_TILE)):
        spectra = _hyena_spectra(L, dfts[L], p, l)
        y1 = _hyena_conv(L, n_seq, tc, dfts[L], spectra[0], skip[0:1], z, row0, Z_HY + 2 * HY_W, third(2),
                         z, row0, Z_HY, third(0), F32)
        outs.append(_hyena_conv(L, n_seq, tc, dfts[L], spectra[1], skip[1:2], y1, 0, 0, None,
                                z, row0, Z_HY + HY_W, third(1), BF16))
    return jnp.concatenate(outs, axis=0)


S5_SEG = 256
S5_LANES = S5_GROUPS * S5_STATE
S5_LANE_TILES = S5_LANES // LANE
S5_CHUNKS = S5_W // LANE
S5_CHUNK_STATES = S5_LANES // S5_CHUNKS
S5_TILES_PER_CHUNK = S5_CHUNK_STATES // LANE
N_SEQ = BATCH + DEC_BATCH
N_SEG = T_ALL // S5_SEG
N_SEG_CTX = T_CTX // S5_SEG
SEGS_PER_CTX = SEQ // S5_SEG
SEGS_PER_LAT = DEC_SEQ // S5_SEG


def _s5_seq_of_seg(seg):
    return jnp.where(seg < N_SEG_CTX, seg // SEGS_PER_CTX, BATCH + (seg - N_SEG_CTX) // SEGS_PER_LAT)


def _s5_pos_in_seq(seg):
    in_ctx = seg < N_SEG_CTX
    pos = jnp.where(in_ctx, seg % SEGS_PER_CTX, (seg - N_SEG_CTX) % SEGS_PER_LAT)
    return pos, jnp.where(in_ctx, SEGS_PER_CTX, SEGS_PER_LAT)


def _s5_kernel(*refs, reverse):
    if reverse:
        (u_ref, h0_ref, sc_ref, bre_ref, bim_ref, cre_ref, cim_ref, yf_ref, d_ref, gw_ref, gb_ref,
         y_ref, fin_ref, bu_re, bu_im, state) = refs
    else:
        (u_ref, h0_ref, sc_ref, bre_ref, bim_ref, cre_ref, cim_ref,
         y_ref, fin_ref, bu_re, bu_im, state) = refs
    i = pl.program_id(0)
    seg = (N_SEG - 1 - i) if reverse else i
    pos, n_pos = _s5_pos_in_seq(seg)
    is_start = (pos == n_pos - 1) if reverse else (pos == 0)

    @pl.when(is_start)
    def _():
        state[:, 0:2, :] = h0_ref[...]

    u = u_ref[...]
    ub = u.astype(BF16)
    for j in range(S5_CHUNKS):
        uj = ub[:, j * LANE:(j + 1) * LANE]
        pre = _dot(uj, bre_ref[j])
        pim = _dot(uj, bim_ref[j])
        for q in range(S5_TILES_PER_CHUNK):
            bu_re[j * S5_TILES_PER_CHUNK + q] = pre[:, q * LANE:(q + 1) * LANE]
            bu_im[j * S5_TILES_PER_CHUNK + q] = pim[:, q * LANE:(q + 1) * LANE]

    n_row_tiles = S5_SEG // SUBLANE

    def lane_body(lt, _):
        mult = [sc_ref[k, lt] for k in range(8)]
        s0 = (state[lt, 0:1, :], state[lt, 1:2, :])

        def row_body(jr, carry):
            s_re, s_im = carry
            rt = (n_row_tiles - 1 - jr) if reverse else jr
            rows = pl.ds(pl.multiple_of(rt * SUBLANE, SUBLANE), SUBLANE)
            xr = bu_re[lt, rows, :]
            xi = bu_im[lt, rows, :]
            for n, k in enumerate((1, 2, 4)):
                mr, mi = mult[2 * n], mult[2 * n + 1]
                shift = (SUBLANE - k) if reverse else k
                rr = pltpu.roll(xr, shift, 0)
                ri = pltpu.roll(xi, shift, 0)
                xr, xi = xr + mr * rr - mi * ri, xi + mr * ri + mi * rr
            pr, pi = mult[6], mult[7]
            br = jnp.broadcast_to(s_re, (SUBLANE, LANE))
            bi = jnp.broadcast_to(s_im, (SUBLANE, LANE))
            xr, xi = xr + pr * br - pi * bi, xi + pr * bi + pi * br
            bu_re[lt, rows, :] = xr
            bu_im[lt, rows, :] = xi
            last = 0 if reverse else SUBLANE - 1
            return xr[last:last + 1, :], xi[last:last + 1, :]

        s_re, s_im = lax.fori_loop(0, n_row_tiles, row_body, s0, unroll=2)
        state[lt, 0:1, :] = s_re
        state[lt, 1:2, :] = s_im
        return 0

    lax.fori_loop(0, S5_LANE_TILES, lane_body, 0)
    fin_ref[...] = state[:, 0:2, :]

    ys = []
    for j in range(S5_CHUNKS):
        tiles = range(j * S5_TILES_PER_CHUNK, (j + 1) * S5_TILES_PER_CHUNK)
        sr = jnp.concatenate([bu_re[t] for t in tiles], axis=-1).astype(BF16)
        si = jnp.concatenate([bu_im[t] for t in tiles], axis=-1).astype(BF16)
        ys.append(_dot(sr, cre_ref[j]) - _dot(si, cim_ref[j]))
    y = jnp.concatenate(ys, axis=-1)
    if reverse:
        y = jax.nn.gelu(y + yf_ref[...] + u * d_ref[...])
        gate = jax.nn.sigmoid(_dot(y.astype(BF16), gw_ref[...].astype(BF16)) + gb_ref[...])
        y_ref[...] = (y * gate).astype(y_ref.dtype)
    else:
        y_ref[...] = y


def _s5_direction_consts(lam_re, lam_im, log_step, b_re, b_im, c_re, c_im, reverse):
    lr = jnp.minimum(lam_re, -1e-4)
    dt = jnp.exp(log_step)[:, None]
    mag, ang = jnp.exp(lr * dt), lam_im * dt
    ar, ai = mag * jnp.cos(ang), mag * jnp.sin(ang)
    den = lr * lr + lam_im * lam_im
    kr = ((ar - 1.0) * lr + ai * lam_im) / den
    ki = (ai * lr - (ar - 1.0) * lam_im) / den
    bb_re = kr[..., None] * b_re - ki[..., None] * b_im
    bb_im = kr[..., None] * b_im + ki[..., None] * b_re
    gpc = S5_GROUPS // S5_CHUNKS
    eye = jnp.eye(gpc, dtype=F32)

    def in_blocks(bb):
        t = bb.reshape(S5_CHUNKS, gpc, S5_STATE, S5_GROUP).transpose(0, 1, 3, 2)
        return jnp.einsum('jgsp,gh->jgshp', t, eye).reshape(S5_CHUNKS, LANE, S5_CHUNK_STATES).astype(BF16)

    def out_blocks(cc):
        t = cc.reshape(S5_CHUNKS, gpc, S5_GROUP, S5_STATE).transpose(0, 1, 3, 2)
        return jnp.einsum('jgps,gh->jgphs', t, eye).reshape(S5_CHUNKS, S5_CHUNK_STATES, LANE).astype(BF16)

    def power(k):
        m = jnp.exp(lr * dt * k)
        return (m * jnp.cos(ang * k)).reshape(-1), (m * jnp.sin(ang * k)).reshape(-1)

    r = jnp.arange(SUBLANE)
    tiles = []
    for k in (1, 2, 4):
        keep = ((r < SUBLANE - k) if reverse else (r >= k)).astype(F32)[:, None]
        pr, pi = power(float(k))
        tiles += [keep * pr[None], keep * pi[None]]
    expo = ((SUBLANE - r) if reverse else (r + 1)).astype(F32)[:, None]
    m = jnp.exp((lr * dt).reshape(-1)[None] * expo)
    a = ang.reshape(-1)[None] * expo
    tiles += [m * jnp.cos(a), m * jnp.sin(a)]
    sc = jnp.stack(tiles).reshape(8, SUBLANE, S5_LANE_TILES, LANE).transpose(0, 2, 1, 3)
    return sc, in_blocks(bb_re), in_blocks(bb_im), out_blocks(c_re), out_blocks(c_im)


def _s5_pass(z, h0, consts, reverse, extra=()):
    seg_of = (lambda i: N_SEG - 1 - i) if reverse else (lambda i: i)
    u_col = Z_S5 // S5_W
    row_spec = lambda col: pl.BlockSpec((S5_SEG, S5_W), lambda i: (seg_of(i), col))
    seq_spec = pl.BlockSpec((None, S5_LANE_TILES, 2, LANE), lambda i: (_s5_seq_of_seg(seg_of(i)), 0, 0, 0))
    whole = lambda a: pl.BlockSpec(a.shape, lambda i: (0,) * a.ndim)
    in_specs = [row_spec(u_col), seq_spec] + [whole(a) for a in consts]
    if reverse:
        yf, d2, glu_w, glu_b2 = extra
        in_specs += [row_spec(0), whole(d2), whole(glu_w), whole(glu_b2)]
    return pl.pallas_call(
        functools.partial(_s5_kernel, reverse=reverse), grid=(N_SEG,),
        in_specs=in_specs,
        out_specs=(row_spec(0), seq_spec),
        out_shape=(jax.ShapeDtypeStruct((T_ALL, S5_W), BF16 if reverse else F32),
                   jax.ShapeDtypeStruct((N_SEQ, S5_LANE_TILES, 2, LANE), F32)),
        scratch_shapes=[pltpu.VMEM((S5_LANE_TILES, S5_SEG, LANE), F32),
                        pltpu.VMEM((S5_LANE_TILES, S5_SEG, LANE), F32),
                        pltpu.VMEM((S5_LANE_TILES, SUBLANE, LANE), F32)],
        compiler_params=_cparams("arbitrary"), name="s5_bwd" if reverse else "s5_fwd",
    )(z, h0, *consts, *extra)


def _s5_mixer(z, p, l, state_lat):
    def tiles(h):
        return h.reshape(N_SEQ, 2, S5_LANE_TILES, LANE).transpose(0, 2, 1, 3)

    def untiles(f):
        return f.transpose(0, 2, 1, 3).reshape(N_SEQ, 2, S5_GROUPS, S5_STATE)

    h0 = jnp.concatenate([jnp.zeros((BATCH, 2, 2, S5_GROUPS, S5_STATE), F32), state_lat], axis=0)
    prm = [p[n][l] for n in ('s5_lam_re', 's5_lam_im', 's5_log_step', 's5_b_re', 's5_b_im', 's5_c_re', 's5_c_im')]
    cf = _s5_direction_consts(*[t[0] for t in prm], reverse=False)
    cb = _s5_direction_consts(*[t[1] for t in prm], reverse=True)
    yf, fin_f = _s5_pass(z, tiles(h0[:, 0]), cf, False)
    extra = (yf, p['s5_d'][l].reshape(1, S5_W), p['s5_glu_w'][l], p['s5_glu_b'][l].reshape(1, S5_W))
    y, fin_b = _s5_pass(z, tiles(h0[:, 1]), cb, True, extra)
    return y, jnp.stack([untiles(fin_f), untiles(fin_b)], axis=1)


def _axial_rope_tables(L):
    n_rows = L // GRID_W
    rows = jnp.repeat(jnp.arange(n_rows, dtype=F32), GRID_W)
    cols = jnp.tile(jnp.arange(GRID_W, dtype=F32), n_rows)
    n_freq = MLA_ROPE // 4
    inv = ROPE_BASE ** (-jnp.arange(n_freq, dtype=F32) / n_freq)
    a_r, a_c = rows[:, None] * inv, cols[:, None] * inv
    ang = jnp.concatenate([a_r, a_r, a_c, a_c], axis=-1)
    return jnp.cos(ang), jnp.sin(ang)


def _dot_nt(a, b):
    return lax.dot_general(a, b, (((1,), (1,)), ((), ())), preferred_element_type=F32)


def _dot_tn(a, b):
    return lax.dot_general(a, b, (((0,), (0,)), ((), ())), preferred_element_type=F32)


def _rms(x):
    return x * lax.rsqrt(jnp.mean(x * x, axis=-1, keepdims=True) + RMS_EPS)


MLA_ROWS = 256
KV_ROWS = 512
QA_BLOCK = 256
assert Z_QA % QA_BLOCK == 0 and Z_KVA % QA_BLOCK == 0 and MLA_Q_RANK % QA_BLOCK == 0 and MLA_KV_RANK % QA_BLOCK == 0


def _kv_expand_kernel(a0_ref, a1_ref, g_ref, w_ref, ckv_ref, kn_ref, v_ref, *, normalize):
    c = jnp.concatenate([a0_ref[...], a1_ref[...]], axis=-1)
    if normalize:
        c = _rms(c) * g_ref[...]
    ckv_ref[...] = c
    kv = _dot(c.astype(BF16), w_ref[...].astype(BF16))
    half = MLA_HEADS * MLA_NOPE
    kn_ref[...] = kv[:, :half].astype(kn_ref.dtype)
    v_ref[...] = kv[:, half:].astype(v_ref.dtype)


def _kv_expand(src, col0, gain2, w_kvb_p, normalize):
    rows = src.shape[0]
    cb = col0 // QA_BLOCK
    hw = MLA_HEADS * MLA_NOPE
    return pl.pallas_call(
        functools.partial(_kv_expand_kernel, normalize=normalize), grid=(rows // KV_ROWS,),
        in_specs=[pl.BlockSpec((KV_ROWS, QA_BLOCK), lambda i: (i, cb)),
                  pl.BlockSpec((KV_ROWS, QA_BLOCK), lambda i: (i, cb + 1)),
                  pl.BlockSpec((1, MLA_KV_RANK), lambda i: (0, 0)),
                  pl.BlockSpec(w_kvb_p.shape, lambda i: (0, 0))],
        out_specs=(pl.BlockSpec((KV_ROWS, MLA_KV_RANK), lambda i: (i, 0)),
                   pl.BlockSpec((KV_ROWS, hw), lambda i: (i, 0)),
                   pl.BlockSpec((KV_ROWS, hw), lambda i: (i, 0))),
        out_shape=(jax.ShapeDtypeStruct((rows, MLA_KV_RANK), F32),
                   jax.ShapeDtypeStruct((rows, hw), BF16),
                   jax.ShapeDtypeStruct((rows, hw), BF16)),
        compiler_params=_cparams("parallel"), name="kv_expand",
    )(src, src, gain2, w_kvb_p)


def _mla_attn_kernel(*refs, n_parts, rope):
    qa_refs, (qg_ref, wn_ref, wr_ref) = refs[:3], refs[3:6]
    pos = 6
    if rope:
        wrr_ref, cq_ref, sq_ref, ck_ref, sk_ref = refs[pos:pos + 5]
        pos += 5
    parts = [refs[pos + 3 * i: pos + 3 * i + 3] for i in range(n_parts)]
    o_ref = refs[pos + 3 * n_parts]

    qa = jnp.concatenate([r[...] for r in qa_refs], axis=-1)
    qa = (_rms(qa) * qg_ref[...]).astype(BF16)
    qn = _dot(qa, wn_ref[...].astype(BF16)).astype(BF16)
    qr = _dot(qa, wr_ref[...].astype(BF16))
    if rope:
        qr = qr * cq_ref[...] + _dot(qa, wrr_ref[...].astype(BF16)) * sq_ref[...]
    qr = qr.astype(BF16)
    scale = (MLA_NOPE + MLA_ROPE) ** -0.5

    scores = []
    for i, (kn_ref, kr_ref, _) in enumerate(parts):
        kr = kr_ref[...]
        if rope and i == n_parts - 1:
            lane = lax.broadcasted_iota(jnp.int32, kr.shape, 1)
            partner = jnp.where(lane % (MLA_ROPE // 2) < MLA_ROPE // 4,
                                pltpu.roll(kr, LANE - MLA_ROPE // 4, 1), pltpu.roll(kr, MLA_ROPE // 4, 1))
            kr = kr * ck_ref[...] + partner * sk_ref[...]
        scores.append((_dot_nt(qn, kn_ref[...]) + _dot_nt(qr, kr.astype(BF16))) * scale)
    m = scores[0].max(axis=-1, keepdims=True)
    for s in scores[1:]:
        m = jnp.maximum(m, s.max(axis=-1, keepdims=True))
    den = 0.0
    acc = 0.0
    for s, (_, _, v_ref) in zip(scores, parts):
        e = jnp.exp(s - m)
        den = den + e.sum(axis=-1, keepdims=True)
        acc = acc + _dot(e.astype(BF16), v_ref[...])
    o_ref[...] = (acc / den).astype(o_ref.dtype)


def _mla_attention(z, q_row_block0, n_seq, n_qblk, q_consts, rope_consts, parts):
    rope = rope_consts is not None
    qcol = Z_QA // QA_BLOCK
    q_rows = lambda b, h, j: q_row_block0 + b * n_qblk + j
    in_specs = [pl.BlockSpec((MLA_ROWS, QA_BLOCK), lambda b, h, j, c=c: (q_rows(b, h, j), qcol + c))
                for c in range(MLA_Q_RANK // QA_BLOCK)]
    head_w = pl.BlockSpec((None, MLA_Q_RANK, LANE), lambda b, h, j: (h, 0, 0))
    in_specs += [pl.BlockSpec((1, MLA_Q_RANK), lambda b, h, j: (0, 0)), head_w, head_w]
    args = [z] * (MLA_Q_RANK // QA_BLOCK) + list(q_consts)
    if rope:
        w_rot, cos_t, sin_t, sin_signed = rope_consts
        q_tab = pl.BlockSpec((MLA_ROWS, LANE), lambda b, h, j: (j, 0))
        k_tab = pl.BlockSpec(cos_t.shape, lambda b, h, j: (0, 0))
        in_specs += [head_w, q_tab, q_tab, k_tab, k_tab]
        args += [w_rot, cos_t, sin_t, cos_t, sin_signed]
    for kn, kr, kr_col, v, n_keys, row_block in parts:
        in_specs += [pl.BlockSpec((n_keys, LANE), lambda b, h, j, rb=row_block: (rb(b), h)),
                     pl.BlockSpec((n_keys, LANE), lambda b, h, j, rb=row_block, cc=kr_col: (rb(b), cc)),
                     pl.BlockSpec((n_keys, LANE), lambda b, h, j, rb=row_block: (rb(b), h))]
        args += [kn, kr, v]
    return pl.pallas_call(
        functools.partial(_mla_attn_kernel, n_parts=len(parts), rope=rope),
        grid=(n_seq, MLA_HEADS, n_qblk), in_specs=in_specs,
        out_specs=pl.BlockSpec((MLA_ROWS, LANE), lambda b, h, j: (b * n_qblk + j, h)),
        out_shape=jax.ShapeDtypeStruct((n_seq * n_qblk * MLA_ROWS, MLA_HEADS * MLA_V), BF16),
        compiler_params=_cparams("parallel", "parallel", "arbitrary"),
        name="mla_attn_rope" if rope else "mla_attn",
    )(*args)


def _rope_rot_cols(w):
    q = MLA_ROPE // 4
    return jnp.concatenate([-w[..., q:2 * q], w[..., :q], -w[..., 3 * q:], w[..., 2 * q:3 * q]], axis=-1)


def _mla_mixer(z, p, l, ckv_cache, kr_cache):
    H = MLA_HEADS
    wq = p['mla_wqb'][l].reshape(MLA_Q_RANK, H, MLA_NOPE + MLA_ROPE).transpose(1, 0, 2)
    w_nope = wq[..., :MLA_NOPE]
    lane_pad = lambda w: jnp.concatenate([w, jnp.zeros(w.shape[:-1] + (LANE - MLA_ROPE,), F32)], axis=-1)
    w_rope = lane_pad(wq[..., MLA_NOPE:])
    w_rope_rot = lane_pad(_rope_rot_cols(wq[..., MLA_NOPE:]))
    wkv = p['mla_wkvb'][l].reshape(MLA_KV_RANK, H, MLA_NOPE + MLA_V)
    w_kvb_p = jnp.concatenate([wkv[..., :MLA_NOPE].reshape(MLA_KV_RANK, -1),
                               wkv[..., MLA_NOPE:].reshape(MLA_KV_RANK, -1)], axis=-1)
    qg = p['mla_qa_g'][l].reshape(1, MLA_Q_RANK)
    kvg = p['mla_kva_g'][l].reshape(1, MLA_KV_RANK)

    c_kv, kn, v = _kv_expand(z, Z_KVA, kvg, w_kvb_p, True)
    _, kn_c, v_c = _kv_expand(ckv_cache.reshape(DEC_BATCH * PAST_LEN, MLA_KV_RANK), 0, kvg, w_kvb_p, False)
    kr_c = lane_pad(kr_cache.reshape(DEC_BATCH * PAST_LEN, MLA_ROPE))

    tail = Z_TAIL // LANE
    q_consts = (qg, w_nope, w_rope)
    y_ctx = _mla_attention(z, 0, BATCH, SEQ // MLA_ROWS, q_consts, None,
                           [(kn, z, tail, v, SEQ, lambda b: b)])
    cos, sin = _axial_rope_tables(DEC_SEQ)
    q4 = MLA_ROPE // 4
    sign = jnp.tile(jnp.concatenate([-jnp.ones((q4,), F32), jnp.ones((q4,), F32)]), 2)
    tab = lambda t: jnp.tile(t, (1, LANE // MLA_ROPE))
    lat0 = T_CTX // DEC_SEQ
    y_lat = _mla_attention(z, T_CTX // MLA_ROWS, DEC_BATCH, DEC_SEQ // MLA_ROWS, q_consts,
                           (w_rope_rot, tab(cos), tab(sin), tab(sin * sign)),
                           [(kn_c, kr_c, 0, v_c, PAST_LEN, lambda b: b),
                            (kn, z, tail, v, DEC_SEQ, lambda b: lat0 + b)])
    return jnp.concatenate([y_ctx, y_lat], axis=0), c_kv


GLA_SEG = S5_SEG
GLA_CHUNKS = GLA_SEG // GLA_CHUNK
GLA_QK = GLA_HEADS * GLA_DK
GLA_VW = GLA_HEADS * GLA_DV
assert Z_GQ % GLA_QK == 0 and Z_GK % GLA_QK == 0 and Z_GV % GLA_VW == 0 and Z_GR % GLA_VW == 0


def _gla_kernel(*refs, reverse):
    if reverse:
        (q_ref, k_ref, v_ref, r_ref, t_ref, wa_ref, ba_ref, s0_ref, of_ref, ng_ref,
         o_ref, fin_ref, state) = refs
    else:
        q_ref, k_ref, v_ref, t_ref, wa_ref, ba_ref, s0_ref, o_ref, fin_ref, state = refs
    i = pl.program_id(0)
    seg = (N_SEG - 1 - i) if reverse else i
    pos, n_pos = _s5_pos_in_seq(seg)
    is_start = (pos == n_pos - 1) if reverse else (pos == 0)

    @pl.when(is_start)
    def _():
        state[...] = s0_ref[...]

    log_a = jax.nn.log_sigmoid(_dot(t_ref[...].astype(BF16), wa_ref[...]) + ba_ref[...]) / GLA_GATE_TEMP
    C = GLA_CHUNK
    row = lax.broadcasted_iota(jnp.int32, (C, C), 0)
    col = lax.broadcasted_iota(jnp.int32, (C, C), 1)
    keep = (col >= row) if reverse else (col <= row)
    ones = keep.astype(F32)
    chunk_order = range(GLA_CHUNKS - 1, -1, -1) if reverse else range(GLA_CHUNKS)
    edge = 0 if reverse else C - 1

    head_out = []
    for h in range(GLA_HEADS):
        S = state[h]
        outs = [None] * GLA_CHUNKS
        for c in chunk_order:
            rows = slice(c * C, (c + 1) * C)
            dk = slice(h * GLA_DK, (h + 1) * GLA_DK)
            b = jnp.dot(ones, log_a[rows, dk], preferred_element_type=F32, precision=lax.Precision.HIGHEST)
            b_end = b[edge:edge + 1]
            q = q_ref[rows, dk] * GLA_DK ** -0.5
            k = k_ref[rows, dk]
            v = v_ref[rows, h * GLA_DV:(h + 1) * GLA_DV].astype(BF16)
            q_in = (q * jnp.exp(b)).astype(BF16)
            k_in = (k * jnp.exp(-b)).astype(BF16)
            k_end = (k * jnp.exp(b_end - b)).astype(BF16)
            s = jnp.where(keep, _dot_nt(q_in, k_in), 0.0)
            outs[c] = _dot(s.astype(BF16), v) + _dot_nt(q_in, S.astype(BF16))
            S = S * jnp.exp(b_end) + _dot_tn(v, k_end)
        state[h] = S
        head_out.append(jnp.concatenate(outs, axis=0))
    fin_ref[...] = state[...]

    if reverse:
        res = []
        for h in range(GLA_HEADS):
            dv = slice(h * GLA_DV, (h + 1) * GLA_DV)
            res.append(_rms(head_out[h] + of_ref[:, dv]) * ng_ref[...])
        r = r_ref[...]
        o_ref[...] = (jnp.concatenate(res, axis=-1) * (r * jax.nn.sigmoid(r))).astype(o_ref.dtype)
    else:
        o_ref[...] = jnp.concatenate(head_out, axis=-1)


def _gla_pass(z, s0, wa, ba, reverse, extra=()):
    seg_of = (lambda i: N_SEG - 1 - i) if reverse else (lambda i: i)
    rows = lambda width, col0: pl.BlockSpec((GLA_SEG, width), lambda i: (seg_of(i), col0 // width))
    seq_spec = pl.BlockSpec((None, GLA_HEADS, GLA_DV, GLA_DK), lambda i: (_s5_seq_of_seg(seg_of(i)), 0, 0, 0))
    whole = lambda a: pl.BlockSpec(a.shape, lambda i: (0,) * a.ndim)
    in_specs = [rows(GLA_QK, Z_GQ), rows(GLA_QK, Z_GK), rows(GLA_VW, Z_GV)]
    args = [z, z, z]
    if reverse:
        in_specs.append(rows(GLA_VW, Z_GR))
        args.append(z)
    in_specs += [rows(LANE, Z_TAIL), whole(wa), whole(ba), seq_spec]
    args += [z, wa, ba, s0]
    if reverse:
        of, ng = extra
        in_specs += [rows(GLA_VW, 0), whole(ng)]
        args += [of, ng]
    return pl.pallas_call(
        functools.partial(_gla_kernel, reverse=reverse), grid=(N_SEG,),
        in_specs=in_specs, out_specs=(rows(GLA_VW, 0), seq_spec),
        out_shape=(jax.ShapeDtypeStruct((T_ALL, GLA_VW), BF16 if reverse else F32),
                   jax.ShapeDtypeStruct((N_SEQ, GLA_HEADS, GLA_DV, GLA_DK), F32)),
        scratch_shapes=[pltpu.VMEM((GLA_HEADS, GLA_DV, GLA_DK), F32)],
        compiler_params=_cparams("arbitrary"), name="gla_bwd" if reverse else "gla_fwd",
    )(*args)


def _gla_mixer(z, p, l, state_lat):
    s0 = jnp.concatenate([jnp.zeros((BATCH, 2, GLA_HEADS, GLA_DK, GLA_DV), F32), state_lat], axis=0)
    s0 = s0.transpose(0, 1, 2, 4, 3)

    def gate_w(d, row0):
        w = jnp.zeros((LANE, GLA_QK), F32).at[row0:row0 + GLA_GATE_RANK].set(p['gla_wa2'][l, d])
        return w.astype(BF16), p['gla_ba2'][l, d].reshape(1, GLA_QK)

    of, fin_f = _gla_pass(z, s0[:, 0], *gate_w(0, TAIL_GAF), False)
    ng = p['gla_norm_g'][l].reshape(1, GLA_DV)
    y, fin_b = _gla_pass(z, s0[:, 1], *gate_w(1, TAIL_GAB), True, (of, ng))
    return y, jnp.stack([fin_f, fin_b], axis=1).transpose(0, 1, 2, 4, 3)


ROUTE_COLS = 512
ROUTE_OUT_ROWS = SUBLANE
assert TOP_K <= ROUTE_OUT_ROWS and N_EXPERTS <= LANE and EXPERTS_PER_GROUP == SUBLANE


def _route_kernel(lg_ref, b_ref, idx_ref, w_ref, rank_ref, cnt_ref, carry):
    @pl.when(pl.program_id(0) == 0)
    def _():
        carry[...] = jnp.zeros_like(carry)

    bt = lg_ref.shape[1]
    G, M = N_EXPERT_GROUPS, EXPERTS_PER_GROUP
    neg = -jnp.inf
    scores = jax.nn.sigmoid(lg_ref[...])
    x = (scores + b_ref[...]).reshape(G, M, bt)
    scores = scores.reshape(G, M, bt)
    member = lax.broadcasted_iota(jnp.int32, (G, M, bt), 1).astype(F32)
    group = lax.broadcasted_iota(jnp.int32, (G, M, bt), 0).astype(F32)
    expert = group * M + member

    m1 = jnp.max(x, axis=1, keepdims=True)
    i1 = jnp.min(jnp.where(x == m1, member, float(M)), axis=1, keepdims=True)
    m2 = jnp.max(jnp.where(member == i1, neg, x), axis=1, keepdims=True)
    gs = jnp.broadcast_to(m1 + m2, (G, M, bt))
    beaten = jnp.zeros((G, M, bt), F32)
    for g in range(G):
        o = gs[g]
        beaten = beaten + jnp.where((o > gs) | ((o == gs) & (g < group)), 1.0, 0.0)
    x = jnp.where(beaten < TOPK_GROUPS, x, neg)

    hits, picked_w = [], []
    for _ in range(TOP_K):
        m = jnp.max(jnp.max(x, axis=0), axis=0, keepdims=True)
        e = jnp.min(jnp.min(jnp.where(x == m, expert, float(N_EXPERTS)), axis=0), axis=0, keepdims=True)
        hit = expert == e
        hits.append(hit)
        picked_w.append(jnp.sum(jnp.sum(jnp.where(hit, scores, 0.0), axis=0), axis=0, keepdims=True))
        x = jnp.where(hit, neg, x)
    total = sum(picked_w)

    onehot = sum(jnp.where(h, 1.0, 0.0) for h in hits).reshape(N_EXPERTS, bt)
    earlier = (lax.broadcasted_iota(jnp.int32, (bt, bt), 0) < lax.broadcasted_iota(jnp.int32, (bt, bt), 1))
    before = (_dot(onehot.astype(BF16), earlier.astype(BF16)) + carry[...]).reshape(G, M, bt)

    out_row = lax.broadcasted_iota(jnp.int32, (ROUTE_OUT_ROWS, bt), 0)
    idx_o = jnp.full((ROUTE_OUT_ROWS, bt), -1.0, F32)
    w_o = jnp.zeros((ROUTE_OUT_ROWS, bt), F32)
    rank_o = jnp.zeros((ROUTE_OUT_ROWS, bt), F32)
    for k in range(TOP_K):
        e = jnp.sum(jnp.sum(jnp.where(hits[k], expert, 0.0), axis=0), axis=0, keepdims=True)
        r = jnp.sum(jnp.sum(jnp.where(hits[k], before, 0.0), axis=0), axis=0, keepdims=True)
        idx_o = jnp.where(out_row == k, e, idx_o)
        w_o = jnp.where(out_row == k, picked_w[k] / total * ROUTED_SCALE, w_o)
        rank_o = jnp.where(out_row == k, r, rank_o)
    idx_ref[...] = idx_o.astype(jnp.int32)
    w_ref[...] = w_o
    rank_ref[...] = rank_o.astype(jnp.int32)
    carry[...] += jnp.sum(onehot, axis=1, keepdims=True)
    cnt_ref[...] = jnp.broadcast_to(carry[...], cnt_ref.shape).astype(jnp.int32)


def _route(logits_t, router_b2):
    t = logits_t.shape[1]
    row_spec = pl.BlockSpec((ROUTE_OUT_ROWS, ROUTE_COLS), lambda i: (0, i))
    row_shape = lambda dt: jax.ShapeDtypeStruct((ROUTE_OUT_ROWS, t), dt)
    idx, w, rank, cnt = pl.pallas_call(
        _route_kernel, grid=(t // ROUTE_COLS,),
        in_specs=[pl.BlockSpec((N_EXPERTS, ROUTE_COLS), lambda i: (0, i)),
                  pl.BlockSpec((N_EXPERTS, 1), lambda i: (0, 0))],
        out_specs=(row_spec, row_spec, row_spec, pl.BlockSpec((N_EXPERTS, LANE), lambda i: (0, 0))),
        out_shape=(row_shape(jnp.int32), row_shape(F32), row_shape(jnp.int32),
                   jax.ShapeDtypeStruct((N_EXPERTS, LANE), jnp.int32)),
        scratch_shapes=[pltpu.VMEM((N_EXPERTS, 1), F32)],
        compiler_params=_cparams("arbitrary"), name="route",
    )(logits_t, router_b2)
    return idx[:TOP_K], w[:TOP_K], rank[:TOP_K], cnt[:, 0]


def _dispatch(idx, rank, counts, bm):
    t = idx.shape[1]
    n_asg = t * TOP_K
    padded = (counts + bm - 1) // bm * bm
    pad_end = jnp.cumsum(padded)
    pad_start = pad_end - padded
    dest = pad_start[idx] + rank
    n_blocks = n_asg // bm + N_EXPERTS
    n_slots = n_blocks * bm
    tok = jnp.broadcast_to(jnp.arange(t, dtype=jnp.int32)[None, :], (TOP_K, t))
    slot_tok = jnp.zeros((n_slots,), jnp.int32).at[dest.reshape(-1)].set(tok.reshape(-1))
    first_row = jnp.arange(n_blocks, dtype=jnp.int32)[:, None] * bm
    blk_e = jnp.minimum(jnp.sum((pad_end[None, :] <= first_row).astype(jnp.int32), axis=1), N_EXPERTS - 1)
    n_used = (pad_end[-1] // bm).astype(jnp.int32).reshape(1)
    return slot_tok, dest, blk_e, n_used


def _moe(h2, logits_t, p, l):
    t = h2.shape[0]
    idx, w, rank, counts = _route(logits_t, p['router_b'][l].reshape(N_EXPERTS, 1))
    slot_tok, dest, blk_e, n_used = _dispatch(idx, rank, counts, MOE_ROWS)
    xg = h2[slot_tok]
    mid = _ffn_up(xg, blk_e, n_used, p['exp_w_gate'], p['exp_w_up'], l, MOE_ROWS)
    y = _ffn_down(mid, blk_e, n_used, p['exp_w_down'], l, MOE_ROWS, BF16)
    routed = jnp.sum(y[dest] * w[..., None], axis=0)
    zero_e = jnp.zeros((t // SHARED_ROWS,), jnp.int32)
    all_used = jnp.full((1,), t // SHARED_ROWS, jnp.int32)
    mid_s = _ffn_up(h2, zero_e, all_used, p['sh_w_gate'][:, None], p['sh_w_up'][:, None], l, SHARED_ROWS)
    shared = _ffn_down(mid_s, zero_e, all_used, p['sh_w_down'][:, None], l, SHARED_ROWS, F32)
    return routed + shared


def _rows_per_token(mod, which):
    m = mod.reshape(N_MOD_GROUPS, N_MOD, D_MODEL)[:, which]
    return jnp.concatenate([jnp.broadcast_to(m[:1], (T_CTX, D_MODEL)),
                            jnp.repeat(m[1:], DEC_SEQ, axis=0)], axis=0)


def kernel(x_prompt, x_sample, c, cache_mla_ckv, cache_mla_krope, state_s5, state_gla, c_ctx, mod_w, mod_b, norm1_g, norm2_g, w_in, hy_conv_w, hy_conv_b, hy_w1, hy_b1, hy_w2, hy_b2, hy_w3, hy_freq, hy_decay, hy_skip, s5_lam_re, s5_lam_im, s5_log_step, s5_b_re, s5_b_im, s5_c_re, s5_c_im, s5_d, s5_glu_w, s5_glu_b, mla_qa_g, mla_wqb, mla_kva_g, mla_wkvb, gla_wa2, gla_ba2, gla_norm_g, w_branch, w_bgate, b_bgate, w_out, router_w, router_b, exp_w_gate, exp_w_up, exp_w_down, sh_w_gate, sh_w_up, sh_w_down, final_g):
    p = dict(locals())
    x = jnp.concatenate([x_prompt.reshape(T_CTX, D_MODEL), x_sample.reshape(T_LAT, D_MODEL)], axis=0)

    cvec = jnp.concatenate([c_ctx[None], c, jnp.zeros((ADALN_ROWS - N_MOD_GROUPS, D_MODEL), F32)], axis=0)
    cvec = jax.nn.silu(cvec).astype(BF16)
    mod_b3 = mod_b.reshape(DEPTH, 1, N_MOD * D_MODEL)
    norm1_g3 = norm1_g.reshape(DEPTH, 1, D_MODEL)
    norm2_g3 = norm2_g.reshape(DEPTH, 1, D_MODEL)
    b_bgate3 = b_bgate.reshape(DEPTH * N_BRANCH, 1, D_MODEL)
    w_in_p = _permute_w_in(w_in)
    router_w_t = router_w.transpose(0, 2, 1)
    dfts = {L: _dft_matrices(L) for L in (SEQ, DEC_SEQ)}

    new_ckv, new_kr, new_s5, new_gla = [], [], [], []
    for l in range(DEPTH):
        mod = _adaln(cvec, mod_w, mod_b3, l)[:N_MOD_GROUPS]
        mod3 = mod.reshape(N_MOD_GROUPS * N_MOD, 1, D_MODEL)
        h = _norm_mod(x, norm1_g3, mod3, l, which_scale=1, which_shift=0)
        z = _in_proj(h, w_in_p, l)
        y_hy = _hyena_mixer(z, p, l, dfts)
        y_s5, s5f = _s5_mixer(z, p, l, state_s5[:, l])
        y_mla, c_kv = _mla_mixer(z, p, l, cache_mla_ckv[:, l], cache_mla_krope[:, l])
        y_gla, glaf = _gla_mixer(z, p, l, state_gla[:, l])
        new_ckv.append(c_kv[:T_CTX].reshape(BATCH, SEQ, MLA_KV_RANK))
        new_kr.append(z[:T_CTX, Z_TAIL + TAIL_KROPE:Z_TAIL + TAIL_KROPE + MLA_ROPE].reshape(BATCH, SEQ, MLA_ROPE))
        new_s5.append(s5f[:BATCH])
        new_gla.append(glaf[:BATCH])
        ys = jnp.stack([y_hy, y_s5, y_mla, y_gla], axis=0)
        m = _merge(h, ys, w_bgate, b_bgate3, w_branch, l)
        x = _out_proj(m, w_out, x, mod3, l, which_gate=2)
        h2, logits_t = _norm_mod(x, norm2_g3, mod3, l, which_scale=4, which_shift=3, router_w=router_w_t)
        x = x + _rows_per_token(mod, 5) * _moe(h2, logits_t, p, l)

    y = _final_norm(x, final_g.reshape(1, D_MODEL))
    y_prompt = y[:T_CTX].reshape(BATCH, SEQ, D_MODEL)
    y_sample = y[T_CTX:].reshape(DEC_BATCH, DEC_SEQ, D_MODEL)
    return (y_prompt, y_sample, jnp.stack(new_ckv, axis=1), jnp.stack(new_kr, axis=1),
            jnp.stack(new_s5, axis=1), jnp.stack(new_gla, axis=1))
```

```python
import functools
import math

import jax
import jax.numpy as jnp
from jax import lax
from jax.experimental import pallas as pl
from jax.experimental.pallas import tpu as pltpu

D_MODEL = 4096
BATCH = 32
SEQ = 256
DEPTH = 2
DEC_BATCH = 2
DEC_SEQ = 4096
PAST_LEN = 256
GRID_W = 64
RMS_EPS = 1e-6
N_BRANCH = 4
BRANCH_W = 1024
HY_W = BRANCH_W
HY_POS_EMB = 33
HY_BANDS = (HY_POS_EMB - 1) // 2
HY_FFN = 64
S5_W = BRANCH_W
S5_GROUP = 16
S5_GROUPS = S5_W // S5_GROUP
S5_STATE = 64
MLA_HEADS = 8
MLA_NOPE = 128
MLA_ROPE = 64
MLA_V = BRANCH_W // MLA_HEADS
MLA_Q_RANK = 768
MLA_KV_RANK = 512
ROPE_BASE = 10000.0
Q_BLOCK = 128
GLA_HEADS = 4
GLA_DK = 128
GLA_DV = BRANCH_W // GLA_HEADS
GLA_GATE_RANK = 16
GLA_GATE_TEMP = 16.0
GLA_CHUNK = 64
N_EXPERTS = 64
TOP_K = 6
N_EXPERT_GROUPS = 8
EXPERTS_PER_GROUP = N_EXPERTS // N_EXPERT_GROUPS
TOPK_GROUPS = 4
D_EXPERT = 1024
D_SHARED = 1024
ROUTED_SCALE = 2.5
IN_SPLITS = (3 * HY_W, S5_W, MLA_Q_RANK, MLA_KV_RANK, MLA_ROPE,
             GLA_HEADS * GLA_DK, GLA_HEADS * GLA_DK, GLA_HEADS * GLA_DV, GLA_HEADS * GLA_DV,
             GLA_GATE_RANK, GLA_GATE_RANK)
N_IN = sum(IN_SPLITS)

T_CTX = BATCH * SEQ
T_LAT = DEC_BATCH * DEC_SEQ
T_ALL = T_CTX + T_LAT
N_MOD_GROUPS = 1 + DEC_BATCH
N_MOD = 6

F32 = jnp.float32
BF16 = jnp.bfloat16

LANE = 128
SUBLANE = 8

Z_HY = 0
Z_S5 = Z_HY + 3 * HY_W
Z_GV = Z_S5 + S5_W
Z_GR = Z_GV + GLA_HEADS * GLA_DV
Z_GQ = Z_GR + GLA_HEADS * GLA_DV
Z_GK = Z_GQ + GLA_HEADS * GLA_DK
Z_QA = Z_GK + GLA_HEADS * GLA_DK
Z_KVA = Z_QA + MLA_Q_RANK
Z_TAIL = Z_KVA + MLA_KV_RANK
TAIL_KROPE = 0
TAIL_GAF = TAIL_KROPE + MLA_ROPE
TAIL_GAB = TAIL_GAF + GLA_GATE_RANK
N_Z = Z_TAIL + LANE
assert Z_TAIL % LANE == 0 and TAIL_GAB + GLA_GATE_RANK <= LANE


def _permute_w_in(w_in):
    o = [0]
    for s in IN_SPLITS:
        o.append(o[-1] + s)
    hy, s5, qa, kva, krope, gq, gk, gv, gr, gaf, gab = range(len(IN_SPLITS))
    parts = [w_in[..., o[i]:o[i + 1]] for i in (hy, s5, gv, gr, gq, gk, qa, kva, krope, gaf, gab)]
    pad = jnp.zeros(w_in.shape[:-1] + (N_Z - N_IN,), w_in.dtype)
    return jnp.concatenate(parts + [pad], axis=-1)


VMEM_LIMIT_BYTES = 56 * 1024 * 1024
ROW_TILE = 1024
MOE_ROWS = 512
SHARED_ROWS = 1024
ADALN_ROWS = 16


def _cparams(*sem):
    return pltpu.CompilerParams(dimension_semantics=sem, vmem_limit_bytes=VMEM_LIMIT_BYTES)


def _mod_group(i, tm):
    n_ctx = T_CTX // tm
    per_req = DEC_SEQ // tm
    return jnp.where(i < n_ctx, 0, 1 + (i - n_ctx) // per_req)


def _dot(a, b):
    return jnp.dot(a, b, preferred_element_type=F32)


def _mm_bias_kernel(a_ref, w_ref, b_ref, o_ref):
    o_ref[...] = _dot(a_ref[...], w_ref[...].astype(BF16)) + b_ref[...]


def _adaln(a, mod_w, mod_b3, l, tn=1024):
    m, k = a.shape
    n = mod_w.shape[-1]
    return pl.pallas_call(
        _mm_bias_kernel,
        grid=(n // tn,),
        in_specs=[pl.BlockSpec((m, k), lambda j: (0, 0)),
                  pl.BlockSpec((None, k, tn), lambda j: (l, 0, j)),
                  pl.BlockSpec((None, 1, tn), lambda j: (l, 0, j))],
        out_specs=pl.BlockSpec((m, tn), lambda j: (0, j)),
        out_shape=jax.ShapeDtypeStruct((m, n), F32),
        compiler_params=_cparams("parallel"),
        name="adaln",
    )(a, mod_w, mod_b3)


def _norm_mod_kernel(x_ref, g_ref, sc_ref, sh_ref, o_ref):
    x = x_ref[...]
    y = x * lax.rsqrt(jnp.mean(x * x, axis=-1, keepdims=True) + RMS_EPS) * g_ref[...]
    o_ref[...] = (y * (1.0 + sc_ref[...]) + sh_ref[...]).astype(o_ref.dtype)


def _norm_mod_router_kernel(x_ref, g_ref, sc_ref, sh_ref, rw_ref, o_ref, lg_ref):
    x = x_ref[...]
    y = x * lax.rsqrt(jnp.mean(x * x, axis=-1, keepdims=True) + RMS_EPS) * g_ref[...]
    h = y * (1.0 + sc_ref[...]) + sh_ref[...]
    o_ref[...] = h.astype(o_ref.dtype)
    lg_ref[...] = lax.dot_general(rw_ref[...], h, (((1,), (1,)), ((), ())), preferred_element_type=F32,
                                  precision=lax.Precision.HIGHEST)


def _norm_mod(x, gain3, mod3, l, which_scale, which_shift, router_w=None, tm=256):
    t, d = x.shape
    in_specs = [pl.BlockSpec((tm, d), lambda i: (i, 0)),
                pl.BlockSpec((None, 1, d), lambda i: (l, 0, 0)),
                pl.BlockSpec((None, 1, d), lambda i: (_mod_group(i, tm) * N_MOD + which_scale, 0, 0)),
                pl.BlockSpec((None, 1, d), lambda i: (_mod_group(i, tm) * N_MOD + which_shift, 0, 0))]
    h_spec = pl.BlockSpec((tm, d), lambda i: (i, 0))
    h_shape = jax.ShapeDtypeStruct((t, d), BF16)
    if router_w is None:
        return pl.pallas_call(
            _norm_mod_kernel, grid=(t // tm,), in_specs=in_specs, out_specs=h_spec, out_shape=h_shape,
            compiler_params=_cparams("parallel"), name="norm_mod",
        )(x, gain3, mod3, mod3)
    return pl.pallas_call(
        _norm_mod_router_kernel, grid=(t // tm,),
        in_specs=in_specs + [pl.BlockSpec((None, N_EXPERTS, d), lambda i: (l, 0, 0))],
        out_specs=(h_spec, pl.BlockSpec((N_EXPERTS, tm), lambda i: (0, i))),
        out_shape=(h_shape, jax.ShapeDtypeStruct((N_EXPERTS, t), F32)),
        compiler_params=_cparams("parallel"), name="norm_mod_router",
    )(x, gain3, mod3, mod3, router_w)


def _final_norm_kernel(x_ref, g_ref, o_ref):
    x = x_ref[...]
    o_ref[...] = x * lax.rsqrt(jnp.mean(x * x, axis=-1, keepdims=True) + RMS_EPS) * g_ref[...]


def _final_norm(x, g2, tm=256):
    t, d = x.shape
    return pl.pallas_call(
        _final_norm_kernel, grid=(t // tm,),
        in_specs=[pl.BlockSpec((tm, d), lambda i: (i, 0)), pl.BlockSpec((1, d), lambda i: (0, 0))],
        out_specs=pl.BlockSpec((tm, d), lambda i: (i, 0)),
        out_shape=jax.ShapeDtypeStruct((t, d), F32),
        compiler_params=_cparams("parallel"), name="final_norm",
    )(x, g2)


def _mm_kernel(a_ref, w_ref, o_ref):
    o_ref[...] = _dot(a_ref[...], w_ref[...].astype(BF16)).astype(o_ref.dtype)


def _in_proj(h, w_in, l, tm=ROW_TILE, tn=512):
    t, k = h.shape
    n = w_in.shape[-1]
    return pl.pallas_call(
        _mm_kernel, grid=(t // tm, pl.cdiv(n, tn)),
        in_specs=[pl.BlockSpec((tm, k), lambda i, j: (i, 0)),
                  pl.BlockSpec((None, k, tn), lambda i, j: (l, 0, j))],
        out_specs=pl.BlockSpec((tm, tn), lambda i, j: (i, j)),
        out_shape=jax.ShapeDtypeStruct((t, n), F32),
        compiler_params=_cparams("parallel", "parallel"), name="in_proj",
    )(h, w_in)


def _merge_kernel(h_ref, wg_ref, bg_ref, y_ref, wb_ref, o_ref, acc_ref):
    i = pl.program_id(2)
    gate = jax.nn.sigmoid(_dot(h_ref[...], wg_ref[...].astype(BF16)) + bg_ref[...])
    contrib = gate * _dot(y_ref[...], wb_ref[...].astype(BF16))

    @pl.when(i == 0)
    def _():
        acc_ref[...] = contrib

    @pl.when(i > 0)
    def _():
        acc_ref[...] += contrib

    @pl.when(i == N_BRANCH - 1)
    def _():
        o_ref[...] = acc_ref[...].astype(o_ref.dtype)


def _merge(h, ys, w_bgate, b_bgate3, w_branch, l, tm=ROW_TILE, tn=256):
    t, d = h.shape
    bw = ys.shape[-1]
    return pl.pallas_call(
        _merge_kernel, grid=(t // tm, d // tn, N_BRANCH),
        in_specs=[pl.BlockSpec((tm, d), lambda m, n, i: (m, 0)),
                  pl.BlockSpec((None, None, d, tn), lambda m, n, i: (l, i, 0, n)),
                  pl.BlockSpec((None, 1, tn), lambda m, n, i: (l * N_BRANCH + i, 0, n)),
                  pl.BlockSpec((None, tm, bw), lambda m, n, i: (i, m, 0)),
                  pl.BlockSpec((None, None, bw, tn), lambda m, n, i: (l, i, 0, n))],
        out_specs=pl.BlockSpec((tm, tn), lambda m, n, i: (m, n)),
        out_shape=jax.ShapeDtypeStruct((t, d), BF16),
        scratch_shapes=[pltpu.VMEM((tm, tn), F32)],
        compiler_params=_cparams("parallel", "parallel", "arbitrary"), name="merge",
    )(h, w_bgate, b_bgate3, ys, w_branch)


def _mm_resid_kernel(a_ref, w_ref, x_ref, g_ref, o_ref):
    o_ref[...] = x_ref[...] + g_ref[...] * _dot(a_ref[...], w_ref[...].astype(BF16))


def _out_proj(m, w_out, x, mod3, l, which_gate, tm=ROW_TILE, tn=512):
    t, k = m.shape
    n = w_out.shape[-1]
    return pl.pallas_call(
        _mm_resid_kernel, grid=(t // tm, n // tn),
        in_specs=[pl.BlockSpec((tm, k), lambda i, j: (i, 0)),
                  pl.BlockSpec((None, k, tn), lambda i, j: (l, 0, j)),
                  pl.BlockSpec((tm, tn), lambda i, j: (i, j)),
                  pl.BlockSpec((None, 1, tn), lambda i, j: (_mod_group(i, tm) * N_MOD + which_gate, 0, j))],
        out_specs=pl.BlockSpec((tm, tn), lambda i, j: (i, j)),
        out_shape=jax.ShapeDtypeStruct((t, n), F32),
        compiler_params=_cparams("parallel", "parallel"), name="out_proj",
    )(m, w_out, x, mod3)


def _ffn_up_kernel(e_ref, nb_ref, x_ref, wg_ref, wu_ref, o_ref):
    @pl.when(pl.program_id(1) < nb_ref[0])
    def _():
        x = x_ref[...]
        g = _dot(x, wg_ref[...].astype(BF16))
        u = _dot(x, wu_ref[...].astype(BF16))
        o_ref[...] = (g * jax.nn.sigmoid(g) * u).astype(o_ref.dtype)

    @pl.when(pl.program_id(1) >= nb_ref[0])
    def _():
        o_ref[...] = jnp.zeros_like(o_ref)


def _ffn_up(xg, blk_e, n_used, w_gate, w_up, l, bm, tn=256):
    r, d = xg.shape
    f = w_gate.shape[-1]
    w_spec = pl.BlockSpec((None, None, d, tn), lambda j, b, e, nb: (l, e[b], 0, j))
    return pl.pallas_call(
        _ffn_up_kernel,
        grid_spec=pltpu.PrefetchScalarGridSpec(
            num_scalar_prefetch=2, grid=(f // tn, r // bm),
            in_specs=[pl.BlockSpec((bm, d), lambda j, b, e, nb: (b, 0)), w_spec, w_spec],
            out_specs=pl.BlockSpec((bm, tn), lambda j, b, e, nb: (b, j))),
        out_shape=jax.ShapeDtypeStruct((r, f), BF16),
        compiler_params=_cparams("parallel", "arbitrary"), name="ffn_up",
    )(blk_e, n_used, xg, w_gate, w_up)


def _ffn_down_kernel(e_ref, nb_ref, a_ref, w_ref, o_ref):
    @pl.when(pl.program_id(1) < nb_ref[0])
    def _():
        o_ref[...] = _dot(a_ref[...], w_ref[...].astype(BF16)).astype(o_ref.dtype)

    @pl.when(pl.program_id(1) >= nb_ref[0])
    def _():
        o_ref[...] = jnp.zeros_like(o_ref)


def _ffn_down(a, blk_e, n_used, w_down, l, bm, out_dtype, tn=1024):
    r, f = a.shape
    d = w_down.shape[-1]
    return pl.pallas_call(
        _ffn_down_kernel,
        grid_spec=pltpu.PrefetchScalarGridSpec(
            num_scalar_prefetch=2, grid=(d // tn, r // bm),
            in_specs=[pl.BlockSpec((bm, f), lambda j, b, e, nb: (b, 0)),
                      pl.BlockSpec((None, None, f, tn), lambda j, b, e, nb: (l, e[b], 0, j))],
            out_specs=pl.BlockSpec((bm, tn), lambda j, b, e, nb: (b, j))),
        out_shape=jax.ShapeDtypeStruct((r, d), out_dtype),
        compiler_params=_cparams("parallel", "arbitrary"), name="ffn_down",
    )(blk_e, n_used, a, w_down)


HY_FREQ_TILE = 256
HY_CH_TILE = 256


def _dft_matrices(L):
    n = 2 * L
    r = jnp.arange(L, dtype=jnp.int32)[:, None]
    c = jnp.arange(L, dtype=jnp.int32)[None, :]
    ang = ((r * c) % n).astype(F32) * (2.0 * math.pi / n)
    fr = jnp.cos(ang).astype(BF16)
    msin = -jnp.sin(ang)
    fi = jnp.where(r == 0, jnp.where(c % 2 == 0, 1.0, -1.0), msin).astype(BF16)
    gi = jnp.where(c == 0, jnp.where(r % 2 == 0, 1.0, -1.0), msin).astype(BF16)
    return fr, fi, fr, gi


def _mm_bf16_kernel(a_ref, b_ref, o_ref):
    o_ref[...] = _dot(a_ref[...], b_ref[...])


def _filter_spectra(f_stack, h, L):
    n = h.shape[1]
    tn = 512
    return pl.pallas_call(
        _mm_bf16_kernel, grid=(2 * L // HY_FREQ_TILE, n // tn),
        in_specs=[pl.BlockSpec((HY_FREQ_TILE, L), lambda i, j: (i, 0)), pl.BlockSpec((L, tn), lambda i, j: (0, j))],
        out_specs=pl.BlockSpec((HY_FREQ_TILE, tn), lambda i, j: (i, j)),
        out_shape=jax.ShapeDtypeStruct((2 * L, n), F32),
        compiler_params=_cparams("parallel", "parallel"), name="hyena_filter_dft",
    )(f_stack, h)


def _hyena_filters(L, w1, b1, w2, b2, w3, freq, decay):
    hp = lax.Precision.HIGHEST
    t = jnp.linspace(0.0, 1.0, L, dtype=F32)[:, None]
    w = 2.0 * math.pi * jnp.arange(L, dtype=F32)[:, None] / L
    f = jnp.linspace(1e-4, HY_BANDS - 1, HY_BANDS, dtype=F32)[None, :]
    z = jnp.concatenate([t, jnp.cos(f * w), -jnp.sin(f * w)], axis=-1)
    h = jnp.sin(freq[0] * (jnp.dot(z, w1, precision=hp) + b1))
    h = jnp.sin(freq[1] * (jnp.dot(h, w2, precision=hp) + b2))
    h = jnp.dot(h, w3, precision=hp) * jnp.exp(-t * jnp.abs(decay))
    h = h / (jnp.sum(jnp.abs(h), axis=0, keepdims=True) + 1e-6)
    return h.reshape(L, 2, 2, HY_W)


def _hyena_spectra(L, dft, p, l):
    fr, fi, _, _ = dft
    h = _hyena_filters(L, p['hy_w1'][l], p['hy_b1'][l], p['hy_w2'][l], p['hy_b2'][l],
                       p['hy_w3'][l], p['hy_freq'][l], p['hy_decay'][l])
    hf = h[:, :, 0].reshape(L, 2 * HY_W)
    hb = h[:, :, 1].reshape(L, 2 * HY_W)
    late = jnp.concatenate([jnp.zeros_like(hb[:1]), hb[:0:-1]], axis=0)
    spec = _filter_spectra(jnp.concatenate([fr, fi], axis=0), jnp.concatenate([hf, late], axis=1).astype(BF16), L)
    k = jnp.arange(L)[:, None]
    sign = jnp.where(k % 2 == 0, 1.0, -1.0)
    kr = spec[:L, :2 * HY_W] + sign * spec[:L, 2 * HY_W:]
    ki = spec[L:, :2 * HY_W] + sign * spec[L:, 2 * HY_W:]
    scale = jnp.where(k == 0, 1.0, 2.0) / (2 * L)
    a = kr * scale
    b = jnp.where(k == 0, 0.0, ki * scale)
    d = jnp.where(k == 0, ki, kr) * scale
    return [tuple(m[:, o * HY_W:(o + 1) * HY_W] for m in (a, b, d)) for o in range(2)]


def _short_conv_rows(x, w_ref, b_ref):
    n = x.shape[0]
    row = lax.broadcasted_iota(jnp.int32, x.shape, 0)
    prev = jnp.where(row == 0, 0.0, pltpu.roll(x, 1, 0))
    nxt = jnp.where(row == n - 1, 0.0, pltpu.roll(x, n - 1, 0))
    return prev * w_ref[0:1, :] + x * w_ref[1:2, :] + nxt * w_ref[2:3, :] + b_ref[...]


def _hyena_conv_kernel(*refs, conv_input):
    if conv_input:
        (a_ref, wa_ref, ba_ref, g_ref, wg_ref, bg_ref, skip_ref, fr_ref, fi_ref, gr_ref, gi_ref,
         sa_ref, sb_ref, sd_ref, o_ref, u_f32, u_bf, acc) = refs
    else:
        (a_ref, g_ref, wg_ref, bg_ref, skip_ref, fr_ref, fi_ref, gr_ref, gi_ref,
         sa_ref, sb_ref, sd_ref, o_ref, u_f32, u_bf, acc) = refs
    f = pl.program_id(2)

    @pl.when(f == 0)
    def _():
        u = a_ref[...]
        if conv_input:
            u = _short_conv_rows(u, wa_ref, ba_ref)
        u_f32[...] = u
        u_bf[...] = u.astype(BF16)
        acc[...] = jnp.zeros_like(acc)

    ub = u_bf[...]
    ur = _dot(fr_ref[...], ub)
    ui = _dot(fi_ref[...], ub)
    pr = ur * sa_ref[...] - ui * sb_ref[...]
    pi = ur * sb_ref[...] + ui * sd_ref[...]
    acc[...] += _dot(gr_ref[...], pr.astype(BF16)) + _dot(gi_ref[...], pi.astype(BF16))

    @pl.when(f == pl.num_programs(2) - 1)
    def _():
        gate = _short_conv_rows(g_ref[...], wg_ref, bg_ref)
        o_ref[...] = (gate * (acc[...] + skip_ref[...] * u_f32[...])).astype(o_ref.dtype)


def _hyena_conv(L, n_seq, tc, dft, spectra, skip2, a, a_row0, a_col0, a_conv, g, g_row0, g_col0, g_conv, out_dtype):
    fr, fi, gr, gi = dft
    n_f = L // HY_FREQ_TILE
    once = pl.Buffered(1) if L > HY_FREQ_TILE else None
    seq = lambda arr_row0, col0: pl.BlockSpec((L, tc), lambda b, c, f: (arr_row0 + b, col0 // tc + c),
                                              **({'pipeline_mode': once} if once else {}))
    chan = lambda rows: pl.BlockSpec((rows, tc), lambda b, c, f: (0, c))
    in_specs, args = [seq(a_row0, a_col0)], [a]
    if a_conv is not None:
        in_specs += [chan(3), chan(1)]
        args += list(a_conv)
    in_specs += [seq(g_row0, g_col0), chan(3), chan(1), chan(1),
                 pl.BlockSpec((HY_FREQ_TILE, L), lambda b, c, f: (f, 0)),
                 pl.BlockSpec((HY_FREQ_TILE, L), lambda b, c, f: (f, 0)),
                 pl.BlockSpec((L, HY_FREQ_TILE), lambda b, c, f: (0, f)),
                 pl.BlockSpec((L, HY_FREQ_TILE), lambda b, c, f: (0, f))]
    args += [g, *g_conv, skip2, fr, fi, gr, gi]
    in_specs += [pl.BlockSpec((HY_FREQ_TILE, tc), lambda b, c, f: (f, c))] * 3
    args += list(spectra)
    return pl.pallas_call(
        functools.partial(_hyena_conv_kernel, conv_input=a_conv is not None),
        grid=(n_seq, HY_W // tc, n_f), in_specs=in_specs,
        out_specs=pl.BlockSpec((L, tc), lambda b, c, f: (b, c), **({'pipeline_mode': once} if once else {})),
        out_shape=jax.ShapeDtypeStruct((n_seq * L, HY_W), out_dtype),
        scratch_shapes=[pltpu.VMEM((L, tc), F32), pltpu.VMEM((L, tc), BF16), pltpu.VMEM((L, tc), F32)],
        compiler_params=_cparams("parallel", "parallel", "arbitrary"), name="hyena_conv",
    )(*args)


def _hyena_mixer(z, p, l, dfts):
    cw, cb = p['hy_conv_w'][l], p['hy_conv_b'][l].reshape(1, 3 * HY_W)
    third = lambda i: (cw[:, i * HY_W:(i + 1) * HY_W], cb[:, i * HY_W:(i + 1) * HY_W])
    skip = p['hy_skip'][l]
    outs = []
    for L, n_seq, row0, tc in ((SEQ, BATCH, 0, HY_W), (DEC_SEQ, DEC_BATCH, T_CTX // DEC_SEQ, HY_CH_TILE)):
        spectra = _hyena_spectra(L, dfts[L], p, l)
        y1 = _hyena_conv(L, n_seq, tc, dfts[L], spectra[0], skip[0:1], z, row0, Z_HY + 2 * HY_W, third(2),
                         z, row0, Z_HY, third(0), F32)
        outs.append(_hyena_conv(L, n_seq, tc, dfts[L], spectra[1], skip[1:2], y1, 0, 0, None,
                                z, row0, Z_HY + HY_W, third(1), BF16))
    return jnp.concatenate(outs, axis=0)


S5_SEG = 256
S5_LANES = S5_GROUPS * S5_STATE
S5_LANE_TILES = S5_LANES // LANE
S5_CHUNKS = S5_W // LANE
S5_CHUNK_STATES = S5_LANES // S5_CHUNKS
S5_TILES_PER_CHUNK = S5_CHUNK_STATES // LANE
S5_SCAN_WIDTH = 8
N_SEQ = BATCH + DEC_BATCH
N_SEG = T_ALL // S5_SEG
N_SEG_CTX = T_CTX // S5_SEG
SEGS_PER_CTX = SEQ // S5_SEG
SEGS_PER_LAT = DEC_SEQ // S5_SEG


def _s5_seq_of_seg(seg):
    return jnp.where(seg < N_SEG_CTX, seg // SEGS_PER_CTX, BATCH + (seg - N_SEG_CTX) // SEGS_PER_LAT)


def _s5_pos_in_seq(seg):
    in_ctx = seg < N_SEG_CTX
    pos = jnp.where(in_ctx, seg % SEGS_PER_CTX, (seg - N_SEG_CTX) % SEGS_PER_LAT)
    return pos, jnp.where(in_ctx, SEGS_PER_CTX, SEGS_PER_LAT)


def _s5_kernel(*refs, reverse):
    if reverse:
        (u_ref, h0_ref, sc_ref, bre_ref, bim_ref, cre_ref, cim_ref, yf_ref, d_ref, gw_ref, gb_ref,
         y_ref, fin_ref, bu_re, bu_im, state) = refs
    else:
        (u_ref, h0_ref, sc_ref, bre_ref, bim_ref, cre_ref, cim_ref,
         y_ref, fin_ref, bu_re, bu_im, state) = refs
    i = pl.program_id(0)
    seg = (N_SEG - 1 - i) if reverse else i
    pos, n_pos = _s5_pos_in_seq(seg)
    is_start = (pos == n_pos - 1) if reverse else (pos == 0)

    @pl.when(is_start)
    def _():
        state[:, 0:2, :] = h0_ref[...]

    u = u_ref[...]
    ub = u.astype(BF16)
    for j in range(S5_CHUNKS):
        uj = ub[:, j * LANE:(j + 1) * LANE]
        pre = _dot(uj, bre_ref[j])
        pim = _dot(uj, bim_ref[j])
        for q in range(S5_TILES_PER_CHUNK):
            bu_re[j * S5_TILES_PER_CHUNK + q] = pre[:, q * LANE:(q + 1) * LANE]
            bu_im[j * S5_TILES_PER_CHUNK + q] = pim[:, q * LANE:(q + 1) * LANE]

    n_row_tiles = S5_SEG // SUBLANE

    last = 0 if reverse else SUBLANE - 1

    def lane_body(lg, _):
        lts = [lg * S5_SCAN_WIDTH + q for q in range(S5_SCAN_WIDTH)]
        s0 = tuple(state[lt, r:r + 1, :] for lt in lts for r in (0, 1))

        def row_body(jr, carry):
            rt = (n_row_tiles - 1 - jr) if reverse else jr
            rows = pl.ds(pl.multiple_of(rt * SUBLANE, SUBLANE), SUBLANE)
            out = []
            for q, lt in enumerate(lts):
                s_re, s_im = carry[2 * q], carry[2 * q + 1]
                xr = bu_re[lt, rows, :]
                xi = bu_im[lt, rows, :]
                for n, k in enumerate((1, 2, 4)):
                    mr, mi = sc_ref[2 * n, lt], sc_ref[2 * n + 1, lt]
                    shift = (SUBLANE - k) if reverse else k
                    rr = pltpu.roll(xr, shift, 0)
                    ri = pltpu.roll(xi, shift, 0)
                    xr, xi = xr + mr * rr - mi * ri, xi + mr * ri + mi * rr
                pr, pi = sc_ref[6, lt], sc_ref[7, lt]
                br = jnp.broadcast_to(s_re, (SUBLANE, LANE))
                bi = jnp.broadcast_to(s_im, (SUBLANE, LANE))
                xr, xi = xr + pr * br - pi * bi, xi + pr * bi + pi * br
                bu_re[lt, rows, :] = xr
                bu_im[lt, rows, :] = xi
                out += [xr[last:last + 1, :], xi[last:last + 1, :]]
            return tuple(out)

        fin = lax.fori_loop(0, n_row_tiles, row_body, s0)
        for q, lt in enumerate(lts):
            state[lt, 0:1, :] = fin[2 * q]
            state[lt, 1:2, :] = fin[2 * q + 1]
        return 0

    lax.fori_loop(0, S5_LANE_TILES // S5_SCAN_WIDTH, lane_body, 0)
    fin_ref[...] = state[:, 0:2, :]

    ys = []
    for j in range(S5_CHUNKS):
        tiles = range(j * S5_TILES_PER_CHUNK, (j + 1) * S5_TILES_PER_CHUNK)
        sr = jnp.concatenate([bu_re[t] for t in tiles], axis=-1).astype(BF16)
        si = jnp.concatenate([bu_im[t] for t in tiles], axis=-1).astype(BF16)
        ys.append(_dot(sr, cre_ref[j]) - _dot(si, cim_ref[j]))
    y = jnp.concatenate(ys, axis=-1)
    if reverse:
        y = jax.nn.gelu(y + yf_ref[...] + u * d_ref[...])
        gate = jax.nn.sigmoid(_dot(y.astype(BF16), gw_ref[...].astype(BF16)) + gb_ref[...])
        y_ref[...] = (y * gate).astype(y_ref.dtype)
    else:
        y_ref[...] = y


def _s5_direction_consts(lam_re, lam_im, log_step, b_re, b_im, c_re, c_im, reverse):
    lr = jnp.minimum(lam_re, -1e-4)
    dt = jnp.exp(log_step)[:, None]
    mag, ang = jnp.exp(lr * dt), lam_im * dt
    ar, ai = mag * jnp.cos(ang), mag * jnp.sin(ang)
    den = lr * lr + lam_im * lam_im
    kr = ((ar - 1.0) * lr + ai * lam_im) / den
    ki = (ai * lr - (ar - 1.0) * lam_im) / den
    bb_re = kr[..., None] * b_re - ki[..., None] * b_im
    bb_im = kr[..., None] * b_im + ki[..., None] * b_re
    gpc = S5_GROUPS // S5_CHUNKS
    eye = jnp.eye(gpc, dtype=F32)

    def in_blocks(bb):
        t = bb.reshape(S5_CHUNKS, gpc, S5_STATE, S5_GROUP).transpose(0, 1, 3, 2)
        return jnp.einsum('jgsp,gh->jgshp', t, eye).reshape(S5_CHUNKS, LANE, S5_CHUNK_STATES).astype(BF16)

    def out_blocks(cc):
        t = cc.reshape(S5_CHUNKS, gpc, S5_GROUP, S5_STATE).transpose(0, 1, 3, 2)
        return jnp.einsum('jgps,gh->jgphs', t, eye).reshape(S5_CHUNKS, S5_CHUNK_STATES, LANE).astype(BF16)

    def power(k):
        m = jnp.exp(lr * dt * k)
        return (m * jnp.cos(ang * k)).reshape(-1), (m * jnp.sin(ang * k)).reshape(-1)

    r = jnp.arange(SUBLANE)
    tiles = []
    for k in (1, 2, 4):
        keep = ((r < SUBLANE - k) if reverse else (r >= k)).astype(F32)[:, None]
        pr, pi = power(float(k))
        tiles += [keep * pr[None], keep * pi[None]]
    expo = ((SUBLANE - r) if reverse else (r + 1)).astype(F32)[:, None]
    m = jnp.exp((lr * dt).reshape(-1)[None] * expo)
    a = ang.reshape(-1)[None] * expo
    tiles += [m * jnp.cos(a), m * jnp.sin(a)]
    sc = jnp.stack(tiles).reshape(8, SUBLANE, S5_LANE_TILES, LANE).transpose(0, 2, 1, 3)
    return sc, in_blocks(bb_re), in_blocks(bb_im), out_blocks(c_re), out_blocks(c_im)


def _s5_pass(z, h0, consts, reverse, extra=()):
    seg_of = (lambda i: N_SEG - 1 - i) if reverse else (lambda i: i)
    u_col = Z_S5 // S5_W
    row_spec = lambda col: pl.BlockSpec((S5_SEG, S5_W), lambda i: (seg_of(i), col))
    seq_spec = pl.BlockSpec((None, S5_LANE_TILES, 2, LANE), lambda i: (_s5_seq_of_seg(seg_of(i)), 0, 0, 0))
    whole = lambda a: pl.BlockSpec(a.shape, lambda i: (0,) * a.ndim)
    in_specs = [row_spec(u_col), seq_spec] + [whole(a) for a in consts]
    if reverse:
        yf, d2, glu_w, glu_b2 = extra
        in_specs += [row_spec(0), whole(d2), whole(glu_w), whole(glu_b2)]
    return pl.pallas_call(
        functools.partial(_s5_kernel, reverse=reverse), grid=(N_SEG,),
        in_specs=in_specs,
        out_specs=(row_spec(0), seq_spec),
        out_shape=(jax.ShapeDtypeStruct((T_ALL, S5_W), BF16 if reverse else F32),
                   jax.ShapeDtypeStruct((N_SEQ, S5_LANE_TILES, 2, LANE), F32)),
        scratch_shapes=[pltpu.VMEM((S5_LANE_TILES, S5_SEG, LANE), F32),
                        pltpu.VMEM((S5_LANE_TILES, S5_SEG, LANE), F32),
                        pltpu.VMEM((S5_LANE_TILES, SUBLANE, LANE), F32)],
        compiler_params=_cparams("arbitrary"), name="s5_bwd" if reverse else "s5_fwd",
    )(z, h0, *consts, *extra)


def _s5_mixer(z, p, l, state_lat):
    def tiles(h):
        return h.reshape(N_SEQ, 2, S5_LANE_TILES, LANE).transpose(0, 2, 1, 3)

    def untiles(f):
        return f.transpose(0, 2, 1, 3).reshape(N_SEQ, 2, S5_GROUPS, S5_STATE)

    h0 = jnp.concatenate([jnp.zeros((BATCH, 2, 2, S5_GROUPS, S5_STATE), F32), state_lat], axis=0)
    prm = [p[n][l] for n in ('s5_lam_re', 's5_lam_im', 's5_log_step', 's5_b_re', 's5_b_im', 's5_c_re', 's5_c_im')]
    cf = _s5_direction_consts(*[t[0] for t in prm], reverse=False)
    cb = _s5_direction_consts(*[t[1] for t in prm], reverse=True)
    yf, fin_f = _s5_pass(z, tiles(h0[:, 0]), cf, False)
    extra = (yf, p['s5_d'][l].reshape(1, S5_W), p['s5_glu_w'][l], p['s5_glu_b'][l].reshape(1, S5_W))
    y, fin_b = _s5_pass(z, tiles(h0[:, 1]), cb, True, extra)
    return y, jnp.stack([untiles(fin_f), untiles(fin_b)], axis=1)


def _axial_rope_tables(L):
    n_rows = L // GRID_W
    rows = jnp.repeat(jnp.arange(n_rows, dtype=F32), GRID_W)
    cols = jnp.tile(jnp.arange(GRID_W, dtype=F32), n_rows)
    n_freq = MLA_ROPE // 4
    inv = ROPE_BASE ** (-jnp.arange(n_freq, dtype=F32) / n_freq)
    a_r, a_c = rows[:, None] * inv, cols[:, None] * inv
    ang = jnp.concatenate([a_r, a_r, a_c, a_c], axis=-1)
    return jnp.cos(ang), jnp.sin(ang)


def _dot_nt(a, b):
    return lax.dot_general(a, b, (((1,), (1,)), ((), ())), preferred_element_type=F32)


def _dot_tn(a, b):
    return lax.dot_general(a, b, (((0,), (0,)), ((), ())), preferred_element_type=F32)


def _rms(x):
    return x * lax.rsqrt(jnp.mean(x * x, axis=-1, keepdims=True) + RMS_EPS)


MLA_ROWS = 256
KV_ROWS = 512
QA_BLOCK = 256
assert Z_QA % QA_BLOCK == 0 and Z_KVA % QA_BLOCK == 0 and MLA_Q_RANK % QA_BLOCK == 0 and MLA_KV_RANK % QA_BLOCK == 0


def _kv_expand_kernel(a0_ref, a1_ref, g_ref, w_ref, ckv_ref, kn_ref, v_ref, *, normalize):
    c = jnp.concatenate([a0_ref[...], a1_ref[...]], axis=-1)
    if normalize:
        c = _rms(c) * g_ref[...]
    ckv_ref[...] = c
    kv = _dot(c.astype(BF16), w_ref[...].astype(BF16))
    half = MLA_HEADS * MLA_NOPE
    kn_ref[...] = kv[:, :half].astype(kn_ref.dtype)
    v_ref[...] = kv[:, half:].astype(v_ref.dtype)


def _kv_expand(src, col0, gain2, w_kvb_p, normalize):
    rows = src.shape[0]
    cb = col0 // QA_BLOCK
    hw = MLA_HEADS * MLA_NOPE
    return pl.pallas_call(
        functools.partial(_kv_expand_kernel, normalize=normalize), grid=(rows // KV_ROWS,),
        in_specs=[pl.BlockSpec((KV_ROWS, QA_BLOCK), lambda i: (i, cb)),
                  pl.BlockSpec((KV_ROWS, QA_BLOCK), lambda i: (i, cb + 1)),
                  pl.BlockSpec((1, MLA_KV_RANK), lambda i: (0, 0)),
                  pl.BlockSpec(w_kvb_p.shape, lambda i: (0, 0))],
        out_specs=(pl.BlockSpec((KV_ROWS, MLA_KV_RANK), lambda i: (i, 0)),
                   pl.BlockSpec((KV_ROWS, hw), lambda i: (i, 0)),
                   pl.BlockSpec((KV_ROWS, hw), lambda i: (i, 0))),
        out_shape=(jax.ShapeDtypeStruct((rows, MLA_KV_RANK), F32),
                   jax.ShapeDtypeStruct((rows, hw), BF16),
                   jax.ShapeDtypeStruct((rows, hw), BF16)),
        compiler_params=_cparams("parallel"), name="kv_expand",
    )(src, src, gain2, w_kvb_p)


def _mla_attn_kernel(*refs, n_parts, rope):
    qa_refs, (qg_ref, wn_ref, wr_ref) = refs[:3], refs[3:6]
    pos = 6
    if rope:
        wrr_ref, cq_ref, sq_ref, ck_ref, sk_ref = refs[pos:pos + 5]
        pos += 5
    parts = [refs[pos + 3 * i: pos + 3 * i + 3] for i in range(n_parts)]
    o_ref = refs[pos + 3 * n_parts]

    qa = jnp.concatenate([r[...] for r in qa_refs], axis=-1)
    qa = (_rms(qa) * qg_ref[...]).astype(BF16)
    qn = _dot(qa, wn_ref[...].astype(BF16)).astype(BF16)
    qr = _dot(qa, wr_ref[...].astype(BF16))
    if rope:
        qr = qr * cq_ref[...] + _dot(qa, wrr_ref[...].astype(BF16)) * sq_ref[...]
    qr = qr.astype(BF16)
    scale = (MLA_NOPE + MLA_ROPE) ** -0.5

    scores = []
    for i, (kn_ref, kr_ref, _) in enumerate(parts):
        kr = kr_ref[...]
        if rope and i == n_parts - 1:
            lane = lax.broadcasted_iota(jnp.int32, kr.shape, 1)
            partner = jnp.where(lane % (MLA_ROPE // 2) < MLA_ROPE // 4,
                                pltpu.roll(kr, LANE - MLA_ROPE // 4, 1), pltpu.roll(kr, MLA_ROPE // 4, 1))
            kr = kr * ck_ref[...] + partner * sk_ref[...]
        scores.append((_dot_nt(qn, kn_ref[...]) + _dot_nt(qr, kr.astype(BF16))) * scale)
    m = scores[0].max(axis=-1, keepdims=True)
    for s in scores[1:]:
        m = jnp.maximum(m, s.max(axis=-1, keepdims=True))
    den = 0.0
    acc = 0.0
    for s, (_, _, v_ref) in zip(scores, parts):
        e = jnp.exp(s - m)
        den = den + e.sum(axis=-1, keepdims=True)
        acc = acc + _dot(e.astype(BF16), v_ref[...])
    o_ref[...] = (acc / den).astype(o_ref.dtype)


def _mla_attention(z, q_row_block0, n_seq, n_qblk, q_consts, rope_consts, parts):
    rope = rope_consts is not None
    qcol = Z_QA // QA_BLOCK
    q_rows = lambda b, h, j: q_row_block0 + b * n_qblk + j
    in_specs = [pl.BlockSpec((MLA_ROWS, QA_BLOCK), lambda b, h, j, c=c: (q_rows(b, h, j), qcol + c))
                for c in range(MLA_Q_RANK // QA_BLOCK)]
    head_w = pl.BlockSpec((None, MLA_Q_RANK, LANE), lambda b, h, j: (h, 0, 0))
    in_specs += [pl.BlockSpec((1, MLA_Q_RANK), lambda b, h, j: (0, 0)), head_w, head_w]
    args = [z] * (MLA_Q_RANK // QA_BLOCK) + list(q_consts)
    if rope:
        w_rot, cos_t, sin_t, sin_signed = rope_consts
        q_tab = pl.BlockSpec((MLA_ROWS, LANE), lambda b, h, j: (j, 0))
        k_tab = pl.BlockSpec(cos_t.shape, lambda b, h, j: (0, 0))
        in_specs += [head_w, q_tab, q_tab, k_tab, k_tab]
        args += [w_rot, cos_t, sin_t, cos_t, sin_signed]
    for kn, kr, kr_col, v, n_keys, row_block in parts:
        in_specs += [pl.BlockSpec((n_keys, LANE), lambda b, h, j, rb=row_block: (rb(b), h)),
                     pl.BlockSpec((n_keys, LANE), lambda b, h, j, rb=row_block, cc=kr_col: (rb(b), cc)),
                     pl.BlockSpec((n_keys, LANE), lambda b, h, j, rb=row_block: (rb(b), h))]
        args += [kn, kr, v]
    return pl.pallas_call(
        functools.partial(_mla_attn_kernel, n_parts=len(parts), rope=rope),
        grid=(n_seq, MLA_HEADS, n_qblk), in_specs=in_specs,
        out_specs=pl.BlockSpec((MLA_ROWS, LANE), lambda b, h, j: (b * n_qblk + j, h)),
        out_shape=jax.ShapeDtypeStruct((n_seq * n_qblk * MLA_ROWS, MLA_HEADS * MLA_V), BF16),
        compiler_params=_cparams("parallel", "parallel", "arbitrary"),
        name="mla_attn_rope" if rope else "mla_attn",
    )(*args)


def _rope_rot_cols(w):
    q = MLA_ROPE // 4
    return jnp.concatenate([-w[..., q:2 * q], w[..., :q], -w[..., 3 * q:], w[..., 2 * q:3 * q]], axis=-1)


def _mla_mixer(z, p, l, ckv_cache, kr_cache):
    H = MLA_HEADS
    wq = p['mla_wqb'][l].reshape(MLA_Q_RANK, H, MLA_NOPE + MLA_ROPE).transpose(1, 0, 2)
    w_nope = wq[..., :MLA_NOPE]
    lane_pad = lambda w: jnp.concatenate([w, jnp.zeros(w.shape[:-1] + (LANE - MLA_ROPE,), F32)], axis=-1)
    w_rope = lane_pad(wq[..., MLA_NOPE:])
    w_rope_rot = lane_pad(_rope_rot_cols(wq[..., MLA_NOPE:]))
    wkv = p['mla_wkvb'][l].reshape(MLA_KV_RANK, H, MLA_NOPE + MLA_V)
    w_kvb_p = jnp.concatenate([wkv[..., :MLA_NOPE].reshape(MLA_KV_RANK, -1),
                               wkv[..., MLA_NOPE:].reshape(MLA_KV_RANK, -1)], axis=-1)
    qg = p['mla_qa_g'][l].reshape(1, MLA_Q_RANK)
    kvg = p['mla_kva_g'][l].reshape(1, MLA_KV_RANK)

    c_kv, kn, v = _kv_expand(z, Z_KVA, kvg, w_kvb_p, True)
    _, kn_c, v_c = _kv_expand(ckv_cache.reshape(DEC_BATCH * PAST_LEN, MLA_KV_RANK), 0, kvg, w_kvb_p, False)
    kr_c = lane_pad(kr_cache.reshape(DEC_BATCH * PAST_LEN, MLA_ROPE))

    tail = Z_TAIL // LANE
    q_consts = (qg, w_nope, w_rope)
    y_ctx = _mla_attention(z, 0, BATCH, SEQ // MLA_ROWS, q_consts, None,
                           [(kn, z, tail, v, SEQ, lambda b: b)])
    cos, sin = _axial_rope_tables(DEC_SEQ)
    q4 = MLA_ROPE // 4
    sign = jnp.tile(jnp.concatenate([-jnp.ones((q4,), F32), jnp.ones((q4,), F32)]), 2)
    tab = lambda t: jnp.tile(t, (1, LANE // MLA_ROPE))
    lat0 = T_CTX // DEC_SEQ
    y_lat = _mla_attention(z, T_CTX // MLA_ROWS, DEC_BATCH, DEC_SEQ // MLA_ROWS, q_consts,
                           (w_rope_rot, tab(cos), tab(sin), tab(sin * sign)),
                           [(kn_c, kr_c, 0, v_c, PAST_LEN, lambda b: b),
                            (kn, z, tail, v, DEC_SEQ, lambda b: lat0 + b)])
    return jnp.concatenate([y_ctx, y_lat], axis=0), c_kv


GLA_SEG = S5_SEG
GLA_CHUNKS = GLA_SEG // GLA_CHUNK
GLA_QK = GLA_HEADS * GLA_DK
GLA_VW = GLA_HEADS * GLA_DV
assert Z_GQ % GLA_QK == 0 and Z_GK % GLA_QK == 0 and Z_GV % GLA_VW == 0 and Z_GR % GLA_VW == 0


def _gla_kernel(*refs, reverse):
    if reverse:
        (q_ref, k_ref, v_ref, r_ref, t_ref, wa_ref, ba_ref, s0_ref, of_ref, ng_ref,
         o_ref, fin_ref, state) = refs
    else:
        q_ref, k_ref, v_ref, t_ref, wa_ref, ba_ref, s0_ref, o_ref, fin_ref, state = refs
    i = pl.program_id(0)
    seg = (N_SEG - 1 - i) if reverse else i
    pos, n_pos = _s5_pos_in_seq(seg)
    is_start = (pos == n_pos - 1) if reverse else (pos == 0)

    @pl.when(is_start)
    def _():
        state[...] = s0_ref[...]

    log_a = jax.nn.log_sigmoid(_dot(t_ref[...].astype(BF16), wa_ref[...]) + ba_ref[...]) / GLA_GATE_TEMP
    C = GLA_CHUNK
    row = lax.broadcasted_iota(jnp.int32, (C, C), 0)
    col = lax.broadcasted_iota(jnp.int32, (C, C), 1)
    keep = (col >= row) if reverse else (col <= row)
    ones = keep.astype(F32)
    chunk_order = range(GLA_CHUNKS - 1, -1, -1) if reverse else range(GLA_CHUNKS)
    edge = 0 if reverse else C - 1

    head_out = []
    for h in range(GLA_HEADS):
        S = state[h]
        outs = [None] * GLA_CHUNKS
        for c in chunk_order:
            rows = slice(c * C, (c + 1) * C)
            dk = slice(h * GLA_DK, (h + 1) * GLA_DK)
            b = jnp.dot(ones, log_a[rows, dk], preferred_element_type=F32, precision=lax.Precision.HIGHEST)
            b_end = b[edge:edge + 1]
            q = q_ref[rows, dk] * GLA_DK ** -0.5
            k = k_ref[rows, dk]
            v = v_ref[rows, h * GLA_DV:(h + 1) * GLA_DV].astype(BF16)
            q_in = (q * jnp.exp(b)).astype(BF16)
            k_in = (k * jnp.exp(-b)).astype(BF16)
            k_end = (k * jnp.exp(b_end - b)).astype(BF16)
            s = jnp.where(keep, _dot_nt(q_in, k_in), 0.0)
            outs[c] = _dot(s.astype(BF16), v) + _dot_nt(q_in, S.astype(BF16))
            S = S * jnp.exp(b_end) + _dot_tn(v, k_end)
        state[h] = S
        head_out.append(jnp.concatenate(outs, axis=0))
    fin_ref[...] = state[...]

    if reverse:
        res = []
        for h in range(GLA_HEADS):
            dv = slice(h * GLA_DV, (h + 1) * GLA_DV)
            res.append(_rms(head_out[h] + of_ref[:, dv]) * ng_ref[...])
        r = r_ref[...]
        o_ref[...] = (jnp.concatenate(res, axis=-1) * (r * jax.nn.sigmoid(r))).astype(o_ref.dtype)
    else:
        o_ref[...] = jnp.concatenate(head_out, axis=-1)


def _gla_pass(z, s0, wa, ba, reverse, extra=()):
    seg_of = (lambda i: N_SEG - 1 - i) if reverse else (lambda i: i)
    rows = lambda width, col0: pl.BlockSpec((GLA_SEG, width), lambda i: (seg_of(i), col0 // width))
    seq_spec = pl.BlockSpec((None, GLA_HEADS, GLA_DV, GLA_DK), lambda i: (_s5_seq_of_seg(seg_of(i)), 0, 0, 0))
    whole = lambda a: pl.BlockSpec(a.shape, lambda i: (0,) * a.ndim)
    in_specs = [rows(GLA_QK, Z_GQ), rows(GLA_QK, Z_GK), rows(GLA_VW, Z_GV)]
    args = [z, z, z]
    if reverse:
        in_specs.append(rows(GLA_VW, Z_GR))
        args.append(z)
    in_specs += [rows(LANE, Z_TAIL), whole(wa), whole(ba), seq_spec]
    args += [z, wa, ba, s0]
    if reverse:
        of, ng = extra
        in_specs += [rows(GLA_VW, 0), whole(ng)]
        args += [of, ng]
    return pl.pallas_call(
        functools.partial(_gla_kernel, reverse=reverse), grid=(N_SEG,),
        in_specs=in_specs, out_specs=(rows(GLA_VW, 0), seq_spec),
        out_shape=(jax.ShapeDtypeStruct((T_ALL, GLA_VW), BF16 if reverse else F32),
                   jax.ShapeDtypeStruct((N_SEQ, GLA_HEADS, GLA_DV, GLA_DK), F32)),
        scratch_shapes=[pltpu.VMEM((GLA_HEADS, GLA_DV, GLA_DK), F32)],
        compiler_params=_cparams("arbitrary"), name="gla_bwd" if reverse else "gla_fwd",
    )(*args)


def _gla_mixer(z, p, l, state_lat):
    s0 = jnp.concatenate([jnp.zeros((BATCH, 2, GLA_HEADS, GLA_DK, GLA_DV), F32), state_lat], axis=0)
    s0 = s0.transpose(0, 1, 2, 4, 3)

    def gate_w(d, row0):
        w = jnp.zeros((LANE, GLA_QK), F32).at[row0:row0 + GLA_GATE_RANK].set(p['gla_wa2'][l, d])
        return w.astype(BF16), p['gla_ba2'][l, d].reshape(1, GLA_QK)

    of, fin_f = _gla_pass(z, s0[:, 0], *gate_w(0, TAIL_GAF), False)
    ng = p['gla_norm_g'][l].reshape(1, GLA_DV)
    y, fin_b = _gla_pass(z, s0[:, 1], *gate_w(1, TAIL_GAB), True, (of, ng))
    return y, jnp.stack([fin_f, fin_b], axis=1).transpose(0, 1, 2, 4, 3)


ROUTE_COLS = 512
ROUTE_OUT_ROWS = SUBLANE
assert TOP_K <= ROUTE_OUT_ROWS and N_EXPERTS <= LANE and EXPERTS_PER_GROUP == SUBLANE


def _route_kernel(lg_ref, b_ref, idx_ref, w_ref, rank_ref, cnt_ref, carry):
    @pl.when(pl.program_id(0) == 0)
    def _():
        carry[...] = jnp.zeros_like(carry)

    bt = lg_ref.shape[1]
    G, M = N_EXPERT_GROUPS, EXPERTS_PER_GROUP
    neg = -jnp.inf
    scores = jax.nn.sigmoid(lg_ref[...])
    x = (scores + b_ref[...]).reshape(G, M, bt)
    scores = scores.reshape(G, M, bt)
    member = lax.broadcasted_iota(jnp.int32, (G, M, bt), 1).astype(F32)
    group = lax.broadcasted_iota(jnp.int32, (G, M, bt), 0).astype(F32)
    expert = group * M + member

    m1 = jnp.max(x, axis=1, keepdims=True)
    i1 = jnp.min(jnp.where(x == m1, member, float(M)), axis=1, keepdims=True)
    m2 = jnp.max(jnp.where(member == i1, neg, x), axis=1, keepdims=True)
    gs = jnp.broadcast_to(m1 + m2, (G, M, bt))
    beaten = jnp.zeros((G, M, bt), F32)
    for g in range(G):
        o = gs[g]
        beaten = beaten + jnp.where((o > gs) | ((o == gs) & (g < group)), 1.0, 0.0)
    x = jnp.where(beaten < TOPK_GROUPS, x, neg)

    hits, picked_w = [], []
    for _ in range(TOP_K):
        m = jnp.max(jnp.max(x, axis=0), axis=0, keepdims=True)
        e = jnp.min(jnp.min(jnp.where(x == m, expert, float(N_EXPERTS)), axis=0), axis=0, keepdims=True)
        hit = expert == e
        hits.append(hit)
        picked_w.append(jnp.sum(jnp.sum(jnp.where(hit, scores, 0.0), axis=0), axis=0, keepdims=True))
        x = jnp.where(hit, neg, x)
    total = sum(picked_w)

    onehot = sum(jnp.where(h, 1.0, 0.0) for h in hits).reshape(N_EXPERTS, bt)
    earlier = (lax.broadcasted_iota(jnp.int32, (bt, bt), 0) < lax.broadcasted_iota(jnp.int32, (bt, bt), 1))
    before = (_dot(onehot.astype(BF16), earlier.astype(BF16)) + carry[...]).reshape(G, M, bt)

    out_row = lax.broadcasted_iota(jnp.int32, (ROUTE_OUT_ROWS, bt), 0)
    idx_o = jnp.full((ROUTE_OUT_ROWS, bt), -1.0, F32)
    w_o = jnp.zeros((ROUTE_OUT_ROWS, bt), F32)
    rank_o = jnp.zeros((ROUTE_OUT_ROWS, bt), F32)
    for k in range(TOP_K):
        e = jnp.sum(jnp.sum(jnp.where(hits[k], expert, 0.0), axis=0), axis=0, keepdims=True)
        r = jnp.sum(jnp.sum(jnp.where(hits[k], before, 0.0), axis=0), axis=0, keepdims=True)
        idx_o = jnp.where(out_row == k, e, idx_o)
        w_o = jnp.where(out_row == k, picked_w[k] / total * ROUTED_SCALE, w_o)
        rank_o = jnp.where(out_row == k, r, rank_o)
    idx_ref[...] = idx_o.astype(jnp.int32)
    w_ref[...] = w_o
    rank_ref[...] = rank_o.astype(jnp.int32)
    carry[...] += jnp.sum(onehot, axis=1, keepdims=True)
    cnt_ref[...] = jnp.broadcast_to(carry[...], cnt_ref.shape).astype(jnp.int32)


def _route(logits_t, router_b2):
    t = logits_t.shape[1]
    row_spec = pl.BlockSpec((ROUTE_OUT_ROWS, ROUTE_COLS), lambda i: (0, i))
    row_shape = lambda dt: jax.ShapeDtypeStruct((ROUTE_OUT_ROWS, t), dt)
    idx, w, rank, cnt = pl.pallas_call(
        _route_kernel, grid=(t // ROUTE_COLS,),
        in_specs=[pl.BlockSpec((N_EXPERTS, ROUTE_COLS), lambda i: (0, i)),
                  pl.BlockSpec((N_EXPERTS, 1), lambda i: (0, 0))],
        out_specs=(row_spec, row_spec, row_spec, pl.BlockSpec((N_EXPERTS, LANE), lambda i: (0, 0))),
        out_shape=(row_shape(jnp.int32), row_shape(F32), row_shape(jnp.int32),
                   jax.ShapeDtypeStruct((N_EXPERTS, LANE), jnp.int32)),
        scratch_shapes=[pltpu.VMEM((N_EXPERTS, 1), F32)],
        compiler_params=_cparams("arbitrary"), name="route",
    )(logits_t, router_b2)
    return idx, w, rank, cnt[:, 0]


def _dispatch(idx, rank, counts, bm):
    t = idx.shape[1]
    n_asg = t * TOP_K
    padded = (counts + bm - 1) // bm * bm
    pad_end = jnp.cumsum(padded)
    pad_start = pad_end - padded
    experts = jnp.arange(N_EXPERTS, dtype=jnp.int32)
    dest = jnp.sum(jnp.where(idx[..., None] == experts, pad_start.astype(jnp.int32), 0), axis=-1) + rank
    n_blocks = n_asg // bm + N_EXPERTS
    n_slots = n_blocks * bm
    tok = jnp.broadcast_to(jnp.arange(t, dtype=jnp.int32)[None, :], (TOP_K, t))
    slot_tok = jnp.zeros((n_slots,), jnp.int32).at[dest.reshape(-1)].set(tok.reshape(-1))
    first_row = jnp.arange(n_blocks, dtype=jnp.int32)[:, None] * bm
    blk_e = jnp.minimum(jnp.sum((pad_end[None, :] <= first_row).astype(jnp.int32), axis=1), N_EXPERTS - 1)
    n_used = (pad_end[-1] // bm).astype(jnp.int32).reshape(1)
    return slot_tok, dest, blk_e, n_used


COMBINE_ROWS = 128


def _combine_kernel(x_ref, g_ref, sh_ref, w_ref, y_ref, o_ref):
    w = w_ref[...]
    acc = sh_ref[...]
    for k in range(TOP_K):
        acc = acc + w[:, k:k + 1] * y_ref[k].astype(F32)
    o_ref[...] = x_ref[...] + g_ref[...] * acc


def _combine(x, mod3, which_gate, shared, w_tok, y_tok):
    t, d = x.shape
    tm = COMBINE_ROWS
    tile = pl.BlockSpec((tm, d), lambda i: (i, 0))
    return pl.pallas_call(
        _combine_kernel, grid=(t // tm,),
        in_specs=[tile,
                  pl.BlockSpec((None, 1, d), lambda i: (_mod_group(i, tm) * N_MOD + which_gate, 0, 0)),
                  tile,
                  pl.BlockSpec((tm, ROUTE_OUT_ROWS), lambda i: (i, 0)),
                  pl.BlockSpec((TOP_K, tm, d), lambda i: (0, i, 0))],
        out_specs=tile, out_shape=jax.ShapeDtypeStruct((t, d), F32),
        compiler_params=_cparams("parallel"), name="moe_combine",
    )(x, mod3, shared, w_tok, y_tok)


def _moe_residual(x, h2, logits_t, mod3, p, l):
    t = h2.shape[0]
    idx, w, rank, counts = _route(logits_t, p['router_b'][l].reshape(N_EXPERTS, 1))
    slot_tok, dest, blk_e, n_used = _dispatch(idx[:TOP_K], rank[:TOP_K], counts, MOE_ROWS)
    xg = h2[slot_tok]
    mid = _ffn_up(xg, blk_e, n_used, p['exp_w_gate'], p['exp_w_up'], l, MOE_ROWS)
    y = _ffn_down(mid, blk_e, n_used, p['exp_w_down'], l, MOE_ROWS, BF16)
    zero_e = jnp.zeros((t // SHARED_ROWS,), jnp.int32)
    all_used = jnp.full((1,), t // SHARED_ROWS, jnp.int32)
    mid_s = _ffn_up(h2, zero_e, all_used, p['sh_w_gate'][:, None], p['sh_w_up'][:, None], l, SHARED_ROWS)
    shared = _ffn_down(mid_s, zero_e, all_used, p['sh_w_down'][:, None], l, SHARED_ROWS, F32)
    return _combine(x, mod3, 5, shared, w.T, y[dest])


def kernel(x_prompt, x_sample, c, cache_mla_ckv, cache_mla_krope, state_s5, state_gla, c_ctx, mod_w, mod_b, norm1_g, norm2_g, w_in, hy_conv_w, hy_conv_b, hy_w1, hy_b1, hy_w2, hy_b2, hy_w3, hy_freq, hy_decay, hy_skip, s5_lam_re, s5_lam_im, s5_log_step, s5_b_re, s5_b_im, s5_c_re, s5_c_im, s5_d, s5_glu_w, s5_glu_b, mla_qa_g, mla_wqb, mla_kva_g, mla_wkvb, gla_wa2, gla_ba2, gla_norm_g, w_branch, w_bgate, b_bgate, w_out, router_w, router_b, exp_w_gate, exp_w_up, exp_w_down, sh_w_gate, sh_w_up, sh_w_down, final_g):
    p = dict(locals())
    x = jnp.concatenate([x_prompt.reshape(T_CTX, D_MODEL), x_sample.reshape(T_LAT, D_MODEL)], axis=0)

    cvec = jnp.concatenate([c_ctx[None], c, jnp.zeros((ADALN_ROWS - N_MOD_GROUPS, D_MODEL), F32)], axis=0)
    cvec = jax.nn.silu(cvec).astype(BF16)
    mod_b3 = mod_b.reshape(DEPTH, 1, N_MOD * D_MODEL)
    norm1_g3 = norm1_g.reshape(DEPTH, 1, D_MODEL)
    norm2_g3 = norm2_g.reshape(DEPTH, 1, D_MODEL)
    b_bgate3 = b_bgate.reshape(DEPTH * N_BRANCH, 1, D_MODEL)
    w_in_p = _permute_w_in(w_in)
    router_w_t = router_w.transpose(0, 2, 1)
    dfts = {L: _dft_matrices(L) for L in (SEQ, DEC_SEQ)}

    new_ckv, new_kr, new_s5, new_gla = [], [], [], []
    for l in range(DEPTH):
        mod = _adaln(cvec, mod_w, mod_b3, l)[:N_MOD_GROUPS]
        mod3 = mod.reshape(N_MOD_GROUPS * N_MOD, 1, D_MODEL)
        h = _norm_mod(x, norm1_g3, mod3, l, which_scale=1, which_shift=0)
        z = _in_proj(h, w_in_p, l)
        y_hy = _hyena_mixer(z, p, l, dfts)
        y_s5, s5f = _s5_mixer(z, p, l, state_s5[:, l])
        y_mla, c_kv = _mla_mixer(z, p, l, cache_mla_ckv[:, l], cache_mla_krope[:, l])
        y_gla, glaf = _gla_mixer(z, p, l, state_gla[:, l])
        new_ckv.append(c_kv[:T_CTX].reshape(BATCH, SEQ, MLA_KV_RANK))
        new_kr.append(z[:T_CTX, Z_TAIL + TAIL_KROPE:Z_TAIL + TAIL_KROPE + MLA_ROPE].reshape(BATCH, SEQ, MLA_ROPE))
        new_s5.append(s5f[:BATCH])
        new_gla.append(glaf[:BATCH])
        ys = jnp.stack([y_hy, y_s5, y_mla, y_gla], axis=0)
        m = _merge(h, ys, w_bgate, b_bgate3, w_branch, l)
        x = _out_proj(m, w_out, x, mod3, l, which_gate=2)
        h2, logits_t = _norm_mod(x, norm2_g3, mod3, l, which_scale=4, which_shift=3, router_w=router_w_t)
        x = _moe_residual(x, h2, logits_t, mod3, p, l)

    y = _final_norm(x, final_g.reshape(1, D_MODEL))
    y_prompt = y[:T_CTX].reshape(BATCH, SEQ, D_MODEL)
    y_sample = y[T_CTX:].reshape(DEC_BATCH, DEC_SEQ, D_MODEL)
    return (y_prompt, y_sample, jnp.stack(new_ckv, axis=1), jnp.stack(new_kr, axis=1),
            jnp.stack(new_s5, axis=1), jnp.stack(new_gla, axis=1))
```

```python
import functools
import math

import jax
import jax.numpy as jnp
from jax import lax
from jax.experimental import pallas as pl
from jax.experimental.pallas import tpu as pltpu

D_MODEL = 4096
BATCH = 32
SEQ = 256
DEPTH = 2
DEC_BATCH = 2
DEC_SEQ = 4096
PAST_LEN = 256
GRID_W = 64
RMS_EPS = 1e-6
N_BRANCH = 4
BRANCH_W = 1024
HY_W = BRANCH_W
HY_POS_EMB = 33
HY_BANDS = (HY_POS_EMB - 1) // 2
HY_FFN = 64
S5_W = BRANCH_W
S5_GROUP = 16
S5_GROUPS = S5_W // S5_GROUP
S5_STATE = 64
MLA_HEADS = 8
MLA_NOPE = 128
MLA_ROPE = 64
MLA_V = BRANCH_W // MLA_HEADS
MLA_Q_RANK = 768
MLA_KV_RANK = 512
ROPE_BASE = 10000.0
Q_BLOCK = 128
GLA_HEADS = 4
GLA_DK = 128
GLA_DV = BRANCH_W // GLA_HEADS
GLA_GATE_RANK = 16
GLA_GATE_TEMP = 16.0
GLA_CHUNK = 64
N_EXPERTS = 64
TOP_K = 6
N_EXPERT_GROUPS = 8
EXPERTS_PER_GROUP = N_EXPERTS // N_EXPERT_GROUPS
TOPK_GROUPS = 4
D_EXPERT = 1024
D_SHARED = 1024
ROUTED_SCALE = 2.5
IN_SPLITS = (3 * HY_W, S5_W, MLA_Q_RANK, MLA_KV_RANK, MLA_ROPE,
             GLA_HEADS * GLA_DK, GLA_HEADS * GLA_DK, GLA_HEADS * GLA_DV, GLA_HEADS * GLA_DV,
             GLA_GATE_RANK, GLA_GATE_RANK)
N_IN = sum(IN_SPLITS)

T_CTX = BATCH * SEQ
T_LAT = DEC_BATCH * DEC_SEQ
T_ALL = T_CTX + T_LAT
N_MOD_GROUPS = 1 + DEC_BATCH
N_MOD = 6

F32 = jnp.float32
BF16 = jnp.bfloat16

LANE = 128
SUBLANE = 8

Z_HY = 0
Z_S5 = Z_HY + 3 * HY_W
Z_GV = Z_S5 + S5_W
Z_GR = Z_GV + GLA_HEADS * GLA_DV
Z_GQ = Z_GR + GLA_HEADS * GLA_DV
Z_GK = Z_GQ + GLA_HEADS * GLA_DK
Z_QA = Z_GK + GLA_HEADS * GLA_DK
Z_KVA = Z_QA + MLA_Q_RANK
Z_TAIL = Z_KVA + MLA_KV_RANK
TAIL_KROPE = 0
TAIL_GAF = TAIL_KROPE + MLA_ROPE
TAIL_GAB = TAIL_GAF + GLA_GATE_RANK
N_Z = Z_TAIL + LANE
assert Z_TAIL % LANE == 0 and TAIL_GAB + GLA_GATE_RANK <= LANE


def _permute_w_in(w_in):
    o = [0]
    for s in IN_SPLITS:
        o.append(o[-1] + s)
    hy, s5, qa, kva, krope, gq, gk, gv, gr, gaf, gab = range(len(IN_SPLITS))
    parts = [w_in[..., o[i]:o[i + 1]] for i in (hy, s5, gv, gr, gq, gk, qa, kva, krope, gaf, gab)]
    pad = jnp.zeros(w_in.shape[:-1] + (N_Z - N_IN,), w_in.dtype)
    return jnp.concatenate(parts + [pad], axis=-1)


VMEM_LIMIT_BYTES = 56 * 1024 * 1024
ROW_TILE = 1024
MOE_ROWS = 512
MOE_UP_COLS = 512
MOE_CHUNKS = 4
SHARED_ROWS = 1024
ADALN_ROWS = 16


def _cparams(*sem):
    return pltpu.CompilerParams(dimension_semantics=sem, vmem_limit_bytes=VMEM_LIMIT_BYTES)


def _mod_group(i, tm):
    n_ctx = T_CTX // tm
    per_req = DEC_SEQ // tm
    return jnp.where(i < n_ctx, 0, 1 + (i - n_ctx) // per_req)


def _dot(a, b):
    return jnp.dot(a, b, preferred_element_type=F32)


def _mm_bias_kernel(a_ref, w_ref, b_ref, o_ref):
    o_ref[...] = _dot(a_ref[...], w_ref[...].astype(BF16)) + b_ref[...]


def _adaln(a, mod_w, mod_b3, l, tn=1024):
    m, k = a.shape
    n = mod_w.shape[-1]
    return pl.pallas_call(
        _mm_bias_kernel,
        grid=(n // tn,),
        in_specs=[pl.BlockSpec((m, k), lambda j: (0, 0)),
                  pl.BlockSpec((None, k, tn), lambda j: (l, 0, j)),
                  pl.BlockSpec((None, 1, tn), lambda j: (l, 0, j))],
        out_specs=pl.BlockSpec((m, tn), lambda j: (0, j)),
        out_shape=jax.ShapeDtypeStruct((m, n), F32),
        compiler_params=_cparams("parallel"),
        name="adaln",
    )(a, mod_w, mod_b3)


def _norm_mod_kernel(x_ref, g_ref, sc_ref, sh_ref, o_ref):
    x = x_ref[...]
    y = x * lax.rsqrt(jnp.mean(x * x, axis=-1, keepdims=True) + RMS_EPS) * g_ref[...]
    o_ref[...] = (y * (1.0 + sc_ref[...]) + sh_ref[...]).astype(o_ref.dtype)


def _norm_mod_router_kernel(x_ref, g_ref, sc_ref, sh_ref, rw_ref, o_ref, lg_ref):
    x = x_ref[...]
    y = x * lax.rsqrt(jnp.mean(x * x, axis=-1, keepdims=True) + RMS_EPS) * g_ref[...]
    h = y * (1.0 + sc_ref[...]) + sh_ref[...]
    o_ref[...] = h.astype(o_ref.dtype)
    lg_ref[...] = lax.dot_general(rw_ref[...], h, (((1,), (1,)), ((), ())), preferred_element_type=F32,
                                  precision=lax.Precision.HIGHEST)


def _norm_mod(x, gain3, mod3, l, which_scale, which_shift, router_w=None, tm=256):
    t, d = x.shape
    in_specs = [pl.BlockSpec((tm, d), lambda i: (i, 0)),
                pl.BlockSpec((None, 1, d), lambda i: (l, 0, 0)),
                pl.BlockSpec((None, 1, d), lambda i: (_mod_group(i, tm) * N_MOD + which_scale, 0, 0)),
                pl.BlockSpec((None, 1, d), lambda i: (_mod_group(i, tm) * N_MOD + which_shift, 0, 0))]
    h_spec = pl.BlockSpec((tm, d), lambda i: (i, 0))
    h_shape = jax.ShapeDtypeStruct((t, d), BF16)
    if router_w is None:
        return pl.pallas_call(
            _norm_mod_kernel, grid=(t // tm,), in_specs=in_specs, out_specs=h_spec, out_shape=h_shape,
            compiler_params=_cparams("parallel"), name="norm_mod",
        )(x, gain3, mod3, mod3)
    return pl.pallas_call(
        _norm_mod_router_kernel, grid=(t // tm,),
        in_specs=in_specs + [pl.BlockSpec((None, N_EXPERTS, d), lambda i: (l, 0, 0))],
        out_specs=(h_spec, pl.BlockSpec((N_EXPERTS, tm), lambda i: (0, i))),
        out_shape=(h_shape, jax.ShapeDtypeStruct((N_EXPERTS, t), F32)),
        compiler_params=_cparams("parallel"), name="norm_mod_router",
    )(x, gain3, mod3, mod3, router_w)


def _final_norm_kernel(x_ref, g_ref, o_ref):
    x = x_ref[...]
    o_ref[...] = x * lax.rsqrt(jnp.mean(x * x, axis=-1, keepdims=True) + RMS_EPS) * g_ref[...]


def _final_norm(x, g2, tm=256):
    t, d = x.shape
    return pl.pallas_call(
        _final_norm_kernel, grid=(t // tm,),
        in_specs=[pl.BlockSpec((tm, d), lambda i: (i, 0)), pl.BlockSpec((1, d), lambda i: (0, 0))],
        out_specs=pl.BlockSpec((tm, d), lambda i: (i, 0)),
        out_shape=jax.ShapeDtypeStruct((t, d), F32),
        compiler_params=_cparams("parallel"), name="final_norm",
    )(x, g2)


def _mm_kernel(a_ref, w_ref, o_ref):
    o_ref[...] = _dot(a_ref[...], w_ref[...].astype(BF16)).astype(o_ref.dtype)


def _in_proj(h, w_in, l, tm=ROW_TILE, tn=512):
    t, k = h.shape
    n = w_in.shape[-1]
    return pl.pallas_call(
        _mm_kernel, grid=(t // tm, pl.cdiv(n, tn)),
        in_specs=[pl.BlockSpec((tm, k), lambda i, j: (i, 0)),
                  pl.BlockSpec((None, k, tn), lambda i, j: (l, 0, j))],
        out_specs=pl.BlockSpec((tm, tn), lambda i, j: (i, j)),
        out_shape=jax.ShapeDtypeStruct((t, n), F32),
        compiler_params=_cparams("parallel", "parallel"), name="in_proj",
    )(h, w_in)


def _merge_kernel(h_ref, wg_ref, bg_ref, y_ref, wb_ref, o_ref, acc_ref):
    i = pl.program_id(2)
    gate = jax.nn.sigmoid(_dot(h_ref[...], wg_ref[...].astype(BF16)) + bg_ref[...])
    contrib = gate * _dot(y_ref[...], wb_ref[...].astype(BF16))

    @pl.when(i == 0)
    def _():
        acc_ref[...] = contrib

    @pl.when(i > 0)
    def _():
        acc_ref[...] += contrib

    @pl.when(i == N_BRANCH - 1)
    def _():
        o_ref[...] = acc_ref[...].astype(o_ref.dtype)


def _merge(h, ys, w_bgate, b_bgate3, w_branch, l, tm=ROW_TILE, tn=256):
    t, d = h.shape
    bw = ys.shape[-1]
    return pl.pallas_call(
        _merge_kernel, grid=(t // tm, d // tn, N_BRANCH),
        in_specs=[pl.BlockSpec((tm, d), lambda m, n, i: (m, 0)),
                  pl.BlockSpec((None, None, d, tn), lambda m, n, i: (l, i, 0, n)),
                  pl.BlockSpec((None, 1, tn), lambda m, n, i: (l * N_BRANCH + i, 0, n)),
                  pl.BlockSpec((None, tm, bw), lambda m, n, i: (i, m, 0)),
                  pl.BlockSpec((None, None, bw, tn), lambda m, n, i: (l, i, 0, n))],
        out_specs=pl.BlockSpec((tm, tn), lambda m, n, i: (m, n)),
        out_shape=jax.ShapeDtypeStruct((t, d), BF16),
        scratch_shapes=[pltpu.VMEM((tm, tn), F32)],
        compiler_params=_cparams("parallel", "parallel", "arbitrary"), name="merge",
    )(h, w_bgate, b_bgate3, ys, w_branch)


def _mm_resid_kernel(a_ref, w_ref, x_ref, g_ref, o_ref):
    o_ref[...] = x_ref[...] + g_ref[...] * _dot(a_ref[...], w_ref[...].astype(BF16))


def _out_proj(m, w_out, x, mod3, l, which_gate, tm=ROW_TILE, tn=512):
    t, k = m.shape
    n = w_out.shape[-1]
    return pl.pallas_call(
        _mm_resid_kernel, grid=(t // tm, n // tn),
        in_specs=[pl.BlockSpec((tm, k), lambda i, j: (i, 0)),
                  pl.BlockSpec((None, k, tn), lambda i, j: (l, 0, j)),
                  pl.BlockSpec((tm, tn), lambda i, j: (i, j)),
                  pl.BlockSpec((None, 1, tn), lambda i, j: (_mod_group(i, tm) * N_MOD + which_gate, 0, j))],
        out_specs=pl.BlockSpec((tm, tn), lambda i, j: (i, j)),
        out_shape=jax.ShapeDtypeStruct((t, n), F32),
        compiler_params=_cparams("parallel", "parallel"), name="out_proj",
    )(m, w_out, x, mod3)


def _ffn_up_kernel(e_ref, nb_ref, x_ref, wg_ref, wu_ref, o_ref):
    @pl.when(pl.program_id(1) < nb_ref[0])
    def _():
        x = x_ref[...]
        g = _dot(x, wg_ref[...].astype(BF16))
        u = _dot(x, wu_ref[...].astype(BF16))
        o_ref[...] = (g * jax.nn.sigmoid(g) * u).astype(o_ref.dtype)

    @pl.when(pl.program_id(1) >= nb_ref[0])
    def _():
        o_ref[...] = jnp.zeros_like(o_ref)


def _ffn_up(xg, blk_e, n_used, w_gate, w_up, l, bm, tn=256):
    r, d = xg.shape
    f = w_gate.shape[-1]
    w_spec = pl.BlockSpec((None, None, d, tn), lambda j, b, e, nb: (l, e[b], 0, j))
    return pl.pallas_call(
        _ffn_up_kernel,
        grid_spec=pltpu.PrefetchScalarGridSpec(
            num_scalar_prefetch=2, grid=(f // tn, r // bm),
            in_specs=[pl.BlockSpec((bm, d), lambda j, b, e, nb: (b, 0)), w_spec, w_spec],
            out_specs=pl.BlockSpec((bm, tn), lambda j, b, e, nb: (b, j))),
        out_shape=jax.ShapeDtypeStruct((r, f), BF16),
        compiler_params=_cparams("parallel", "arbitrary"), name="ffn_up",
    )(blk_e, n_used, xg, w_gate, w_up)


def _ffn_down_kernel(e_ref, nb_ref, a_ref, w_ref, o_ref):
    @pl.when(pl.program_id(1) < nb_ref[0])
    def _():
        o_ref[...] = _dot(a_ref[...], w_ref[...].astype(BF16)).astype(o_ref.dtype)

    @pl.when(pl.program_id(1) >= nb_ref[0])
    def _():
        o_ref[...] = jnp.zeros_like(o_ref)


def _ffn_down(a, blk_e, n_used, w_down, l, bm, out_dtype, tn=1024):
    r, f = a.shape
    d = w_down.shape[-1]
    return pl.pallas_call(
        _ffn_down_kernel,
        grid_spec=pltpu.PrefetchScalarGridSpec(
            num_scalar_prefetch=2, grid=(d // tn, r // bm),
            in_specs=[pl.BlockSpec((bm, f), lambda j, b, e, nb: (b, 0)),
                      pl.BlockSpec((None, None, f, tn), lambda j, b, e, nb: (l, e[b], 0, j))],
            out_specs=pl.BlockSpec((bm, tn), lambda j, b, e, nb: (b, j))),
        out_shape=jax.ShapeDtypeStruct((r, d), out_dtype),
        compiler_params=_cparams("parallel", "arbitrary"), name="ffn_down",
    )(blk_e, n_used, a, w_down)


HY_FREQ_TILE = 256
HY_CH_TILE = 256


def _dft_matrices(L):
    n = 2 * L
    r = jnp.arange(L, dtype=jnp.int32)[:, None]
    c = jnp.arange(L, dtype=jnp.int32)[None, :]
    ang = ((r * c) % n).astype(F32) * (2.0 * math.pi / n)
    fr = jnp.cos(ang).astype(BF16)
    msin = -jnp.sin(ang)
    fi = jnp.where(r == 0, jnp.where(c % 2 == 0, 1.0, -1.0), msin).astype(BF16)
    gi = jnp.where(c == 0, jnp.where(r % 2 == 0, 1.0, -1.0), msin).astype(BF16)
    return fr, fi, fr, gi


def _mm_bf16_kernel(a_ref, b_ref, o_ref):
    o_ref[...] = _dot(a_ref[...], b_ref[...])


def _filter_spectra(f_stack, h, L):
    n = h.shape[1]
    tn = 512
    return pl.pallas_call(
        _mm_bf16_kernel, grid=(2 * L // HY_FREQ_TILE, n // tn),
        in_specs=[pl.BlockSpec((HY_FREQ_TILE, L), lambda i, j: (i, 0)), pl.BlockSpec((L, tn), lambda i, j: (0, j))],
        out_specs=pl.BlockSpec((HY_FREQ_TILE, tn), lambda i, j: (i, j)),
        out_shape=jax.ShapeDtypeStruct((2 * L, n), F32),
        compiler_params=_cparams("parallel", "parallel"), name="hyena_filter_dft",
    )(f_stack, h)


def _hyena_filters(L, w1, b1, w2, b2, w3, freq, decay):
    hp = lax.Precision.HIGHEST
    t = jnp.linspace(0.0, 1.0, L, dtype=F32)[:, None]
    w = 2.0 * math.pi * jnp.arange(L, dtype=F32)[:, None] / L
    f = jnp.linspace(1e-4, HY_BANDS - 1, HY_BANDS, dtype=F32)[None, :]
    z = jnp.concatenate([t, jnp.cos(f * w), -jnp.sin(f * w)], axis=-1)
    h = jnp.sin(freq[0] * (jnp.dot(z, w1, precision=hp) + b1))
    h = jnp.sin(freq[1] * (jnp.dot(h, w2, precision=hp) + b2))
    h = jnp.dot(h, w3, precision=hp) * jnp.exp(-t * jnp.abs(decay))
    h = h / (jnp.sum(jnp.abs(h), axis=0, keepdims=True) + 1e-6)
    return h.reshape(L, 2, 2, HY_W)


def _hyena_spectra(L, dft, p, l):
    fr, fi, _, _ = dft
    h = _hyena_filters(L, p['hy_w1'][l], p['hy_b1'][l], p['hy_w2'][l], p['hy_b2'][l],
                       p['hy_w3'][l], p['hy_freq'][l], p['hy_decay'][l])
    hf = h[:, :, 0].reshape(L, 2 * HY_W)
    hb = h[:, :, 1].reshape(L, 2 * HY_W)
    late = jnp.concatenate([jnp.zeros_like(hb[:1]), hb[:0:-1]], axis=0)
    spec = _filter_spectra(jnp.concatenate([fr, fi], axis=0), jnp.concatenate([hf, late], axis=1).astype(BF16), L)
    k = jnp.arange(L)[:, None]
    sign = jnp.where(k % 2 == 0, 1.0, -1.0)
    kr = spec[:L, :2 * HY_W] + sign * spec[:L, 2 * HY_W:]
    ki = spec[L:, :2 * HY_W] + sign * spec[L:, 2 * HY_W:]
    scale = jnp.where(k == 0, 1.0, 2.0) / (2 * L)
    a = kr * scale
    b = jnp.where(k == 0, 0.0, ki * scale)
    d = jnp.where(k == 0, ki, kr) * scale
    return [tuple(m[:, o * HY_W:(o + 1) * HY_W] for m in (a, b, d)) for o in range(2)]


def _short_conv_rows(x, w_ref, b_ref):
    n = x.shape[0]
    row = lax.broadcasted_iota(jnp.int32, x.shape, 0)
    prev = jnp.where(row == 0, 0.0, pltpu.roll(x, 1, 0))
    nxt = jnp.where(row == n - 1, 0.0, pltpu.roll(x, n - 1, 0))
    return prev * w_ref[0:1, :] + x * w_ref[1:2, :] + nxt * w_ref[2:3, :] + b_ref[...]


def _hyena_conv_kernel(*refs, conv_input):
    if conv_input:
        (a_ref, wa_ref, ba_ref, g_ref, wg_ref, bg_ref, skip_ref, fr_ref, fi_ref, gr_ref, gi_ref,
         sa_ref, sb_ref, sd_ref, o_ref, u_f32, u_bf, acc) = refs
    else:
        (a_ref, g_ref, wg_ref, bg_ref, skip_ref, fr_ref, fi_ref, gr_ref, gi_ref,
         sa_ref, sb_ref, sd_ref, o_ref, u_f32, u_bf, acc) = refs
    f = pl.program_id(2)

    @pl.when(f == 0)
    def _():
        u = a_ref[...]
        if conv_input:
            u = _short_conv_rows(u, wa_ref, ba_ref)
        u_f32[...] = u
        u_bf[...] = u.astype(BF16)
        acc[...] = jnp.zeros_like(acc)

    ub = u_bf[...]
    ur = _dot(fr_ref[...], ub)
    ui = _dot(fi_ref[...], ub)
    pr = ur * sa_ref[...] - ui * sb_ref[...]
    pi = ur * sb_ref[...] + ui * sd_ref[...]
    acc[...] += _dot(gr_ref[...], pr.astype(BF16)) + _dot(gi_ref[...], pi.astype(BF16))

    @pl.when(f == pl.num_programs(2) - 1)
    def _():
        gate = _short_conv_rows(g_ref[...], wg_ref, bg_ref)
        o_ref[...] = (gate * (acc[...] + skip_ref[...] * u_f32[...])).astype(o_ref.dtype)


def _hyena_conv(L, n_seq, tc, dft, spectra, skip2, a, a_row0, a_col0, a_conv, g, g_row0, g_col0, g_conv, out_dtype):
    fr, fi, gr, gi = dft
    n_f = L // HY_FREQ_TILE
    once = pl.Buffered(1) if L > HY_FREQ_TILE else None
    seq = lambda arr_row0, col0: pl.BlockSpec((L, tc), lambda b, c, f: (arr_row0 + b, col0 // tc + c),
                                              **({'pipeline_mode': once} if once else {}))
    chan = lambda rows: pl.BlockSpec((rows, tc), lambda b, c, f: (0, c))
    in_specs, args = [seq(a_row0, a_col0)], [a]
    if a_conv is not None:
        in_specs += [chan(3), chan(1)]
        args += list(a_conv)
    in_specs += [seq(g_row0, g_col0), chan(3), chan(1), chan(1),
                 pl.BlockSpec((HY_FREQ_TILE, L), lambda b, c, f: (f, 0)),
                 pl.BlockSpec((HY_FREQ_TILE, L), lambda b, c, f: (f, 0)),
                 pl.BlockSpec((L, HY_FREQ_TILE), lambda b, c, f: (0, f)),
                 pl.BlockSpec((L, HY_FREQ_TILE), lambda b, c, f: (0, f))]
    args += [g, *g_conv, skip2, fr, fi, gr, gi]
    in_specs += [pl.BlockSpec((HY_FREQ_TILE, tc), lambda b, c, f: (f, c))] * 3
    args += list(spectra)
    return pl.pallas_call(
        functools.partial(_hyena_conv_kernel, conv_input=a_conv is not None),
        grid=(n_seq, HY_W // tc, n_f), in_specs=in_specs,
        out_specs=pl.BlockSpec((L, tc), lambda b, c, f: (b, c), **({'pipeline_mode': once} if once else {})),
        out_shape=jax.ShapeDtypeStruct((n_seq * L, HY_W), out_dtype),
        scratch_shapes=[pltpu.VMEM((L, tc), F32), pltpu.VMEM((L, tc), BF16), pltpu.VMEM((L, tc), F32)],
        compiler_params=_cparams("parallel", "parallel", "arbitrary"), name="hyena_conv",
    )(*args)


def _hyena_mixer(z, p, l, dfts):
    cw, cb = p['hy_conv_w'][l], p['hy_conv_b'][l].reshape(1, 3 * HY_W)
    third = lambda i: (cw[:, i * HY_W:(i + 1) * HY_W], cb[:, i * HY_W:(i + 1) * HY_W])
    skip = p['hy_skip'][l]
    outs = []
    for L, n_seq, row0, tc in ((SEQ, BATCH, 0, HY_W), (DEC_SEQ, DEC_BATCH, T_CTX // DEC_SEQ, HY_CH_TILE)):
        spectra = _hyena_spectra(L, dfts[L], p, l)
        y1 = _hyena_conv(L, n_seq, tc, dfts[L], spectra[0], skip[0:1], z, row0, Z_HY + 2 * HY_W, third(2),
                         z, row0, Z_HY, third(0), F32)
        outs.append(_hyena_conv(L, n_seq, tc, dfts[L], spectra[1], skip[1:2], y1, 0, 0, None,
                                z, row0, Z_HY + HY_W, third(1), BF16))
    return jnp.concatenate(outs, axis=0)


S5_SEG = 256
S5_LANES = S5_GROUPS * S5_STATE
S5_LANE_TILES = S5_LANES // LANE
S5_CHUNKS = S5_W // LANE
S5_CHUNK_STATES = S5_LANES // S5_CHUNKS
S5_TILES_PER_CHUNK = S5_CHUNK_STATES // LANE
S5_SCAN_WIDTH = 8
N_SEQ = BATCH + DEC_BATCH
N_SEG = T_ALL // S5_SEG
N_SEG_CTX = T_CTX // S5_SEG
SEGS_PER_CTX = SEQ // S5_SEG
SEGS_PER_LAT = DEC_SEQ // S5_SEG


def _s5_seq_of_seg(seg):
    return jnp.where(seg < N_SEG_CTX, seg // SEGS_PER_CTX, BATCH + (seg - N_SEG_CTX) // SEGS_PER_LAT)


def _s5_pos_in_seq(seg):
    in_ctx = seg < N_SEG_CTX
    pos = jnp.where(in_ctx, seg % SEGS_PER_CTX, (seg - N_SEG_CTX) % SEGS_PER_LAT)
    return pos, jnp.where(in_ctx, SEGS_PER_CTX, SEGS_PER_LAT)


def _s5_kernel(*refs, reverse):
    if reverse:
        (u_ref, h0_ref, sc_ref, bre_ref, bim_ref, cre_ref, cim_ref, yf_ref, d_ref, gw_ref, gb_ref,
         y_ref, fin_ref, bu_re, bu_im, state) = refs
    else:
        (u_ref, h0_ref, sc_ref, bre_ref, bim_ref, cre_ref, cim_ref,
         y_ref, fin_ref, bu_re, bu_im, state) = refs
    i = pl.program_id(0)
    seg = (N_SEG - 1 - i) if reverse else i
    pos, n_pos = _s5_pos_in_seq(seg)
    is_start = (pos == n_pos - 1) if reverse else (pos == 0)

    @pl.when(is_start)
    def _():
        state[:, 0:2, :] = h0_ref[...]

    u = u_ref[...]
    ub = u.astype(BF16)
    for j in range(S5_CHUNKS):
        uj = ub[:, j * LANE:(j + 1) * LANE]
        pre = _dot(uj, bre_ref[j])
        pim = _dot(uj, bim_ref[j])
        for q in range(S5_TILES_PER_CHUNK):
            bu_re[j * S5_TILES_PER_CHUNK + q] = pre[:, q * LANE:(q + 1) * LANE]
            bu_im[j * S5_TILES_PER_CHUNK + q] = pim[:, q * LANE:(q + 1) * LANE]

    n_row_tiles = S5_SEG // SUBLANE

    last = 0 if reverse else SUBLANE - 1

    def lane_body(lg, _):
        lts = [lg * S5_SCAN_WIDTH + q for q in range(S5_SCAN_WIDTH)]
        s0 = tuple(state[lt, r:r + 1, :] for lt in lts for r in (0, 1))

        def row_body(jr, carry):
            rt = (n_row_tiles - 1 - jr) if reverse else jr
            rows = pl.ds(pl.multiple_of(rt * SUBLANE, SUBLANE), SUBLANE)
            out = []
            for q, lt in enumerate(lts):
                s_re, s_im = carry[2 * q], carry[2 * q + 1]
                xr = bu_re[lt, rows, :]
                xi = bu_im[lt, rows, :]
                for n, k in enumerate((1, 2, 4)):
                    mr, mi = sc_ref[2 * n, lt], sc_ref[2 * n + 1, lt]
                    shift = (SUBLANE - k) if reverse else k
                    rr = pltpu.roll(xr, shift, 0)
                    ri = pltpu.roll(xi, shift, 0)
                    xr, xi = xr + mr * rr - mi * ri, xi + mr * ri + mi * rr
                pr, pi = sc_ref[6, lt], sc_ref[7, lt]
                br = jnp.broadcast_to(s_re, (SUBLANE, LANE))
                bi = jnp.broadcast_to(s_im, (SUBLANE, LANE))
                xr, xi = xr + pr * br - pi * bi, xi + pr * bi + pi * br
                bu_re[lt, rows, :] = xr
                bu_im[lt, rows, :] = xi
                out += [xr[last:last + 1, :], xi[last:last + 1, :]]
            return tuple(out)

        fin = lax.fori_loop(0, n_row_tiles, row_body, s0)
        for q, lt in enumerate(lts):
            state[lt, 0:1, :] = fin[2 * q]
            state[lt, 1:2, :] = fin[2 * q + 1]
        return 0

    lax.fori_loop(0, S5_LANE_TILES // S5_SCAN_WIDTH, lane_body, 0)
    fin_ref[...] = state[:, 0:2, :]

    ys = []
    for j in range(S5_CHUNKS):
        tiles = range(j * S5_TILES_PER_CHUNK, (j + 1) * S5_TILES_PER_CHUNK)
        sr = jnp.concatenate([bu_re[t] for t in tiles], axis=-1).astype(BF16)
        si = jnp.concatenate([bu_im[t] for t in tiles], axis=-1).astype(BF16)
        ys.append(_dot(sr, cre_ref[j]) - _dot(si, cim_ref[j]))
    y = jnp.concatenate(ys, axis=-1)
    if reverse:
        y = jax.nn.gelu(y + yf_ref[...] + u * d_ref[...])
        gate = jax.nn.sigmoid(_dot(y.astype(BF16), gw_ref[...].astype(BF16)) + gb_ref[...])
        y_ref[...] = (y * gate).astype(y_ref.dtype)
    else:
        y_ref[...] = y


def _s5_direction_consts(lam_re, lam_im, log_step, b_re, b_im, c_re, c_im, reverse):
    lr = jnp.minimum(lam_re, -1e-4)
    dt = jnp.exp(log_step)[:, None]
    mag, ang = jnp.exp(lr * dt), lam_im * dt
    ar, ai = mag * jnp.cos(ang), mag * jnp.sin(ang)
    den = lr * lr + lam_im * lam_im
    kr = ((ar - 1.0) * lr + ai * lam_im) / den
    ki = (ai * lr - (ar - 1.0) * lam_im) / den
    bb_re = kr[..., None] * b_re - ki[..., None] * b_im
    bb_im = kr[..., None] * b_im + ki[..., None] * b_re
    gpc = S5_GROUPS // S5_CHUNKS
    eye = jnp.eye(gpc, dtype=F32)

    def in_blocks(bb):
        t = bb.reshape(S5_CHUNKS, gpc, S5_STATE, S5_GROUP).transpose(0, 1, 3, 2)
        return jnp.einsum('jgsp,gh->jgshp', t, eye).reshape(S5_CHUNKS, LANE, S5_CHUNK_STATES).astype(BF16)

    def out_blocks(cc):
        t = cc.reshape(S5_CHUNKS, gpc, S5_GROUP, S5_STATE).transpose(0, 1, 3, 2)
        return jnp.einsum('jgps,gh->jgphs', t, eye).reshape(S5_CHUNKS, S5_CHUNK_STATES, LANE).astype(BF16)

    def power(k):
        m = jnp.exp(lr * dt * k)
        return (m * jnp.cos(ang * k)).reshape(-1), (m * jnp.sin(ang * k)).reshape(-1)

    r = jnp.arange(SUBLANE)
    tiles = []
    for k in (1, 2, 4):
        keep = ((r < SUBLANE - k) if reverse else (r >= k)).astype(F32)[:, None]
        pr, pi = power(float(k))
        tiles += [keep * pr[None], keep * pi[None]]
    expo = ((SUBLANE - r) if reverse else (r + 1)).astype(F32)[:, None]
    m = jnp.exp((lr * dt).reshape(-1)[None] * expo)
    a = ang.reshape(-1)[None] * expo
    tiles += [m * jnp.cos(a), m * jnp.sin(a)]
    sc = jnp.stack(tiles).reshape(8, SUBLANE, S5_LANE_TILES, LANE).transpose(0, 2, 1, 3)
    return sc, in_blocks(bb_re), in_blocks(bb_im), out_blocks(c_re), out_blocks(c_im)


def _s5_pass(z, h0, consts, reverse, extra=()):
    seg_of = (lambda i: N_SEG - 1 - i) if reverse else (lambda i: i)
    u_col = Z_S5 // S5_W
    row_spec = lambda col: pl.BlockSpec((S5_SEG, S5_W), lambda i: (seg_of(i), col))
    seq_spec = pl.BlockSpec((None, S5_LANE_TILES, 2, LANE), lambda i: (_s5_seq_of_seg(seg_of(i)), 0, 0, 0))
    whole = lambda a: pl.BlockSpec(a.shape, lambda i: (0,) * a.ndim)
    in_specs = [row_spec(u_col), seq_spec] + [whole(a) for a in consts]
    if reverse:
        yf, d2, glu_w, glu_b2 = extra
        in_specs += [row_spec(0), whole(d2), whole(glu_w), whole(glu_b2)]
    return pl.pallas_call(
        functools.partial(_s5_kernel, reverse=reverse), grid=(N_SEG,),
        in_specs=in_specs,
        out_specs=(row_spec(0), seq_spec),
        out_shape=(jax.ShapeDtypeStruct((T_ALL, S5_W), BF16 if reverse else F32),
                   jax.ShapeDtypeStruct((N_SEQ, S5_LANE_TILES, 2, LANE), F32)),
        scratch_shapes=[pltpu.VMEM((S5_LANE_TILES, S5_SEG, LANE), F32),
                        pltpu.VMEM((S5_LANE_TILES, S5_SEG, LANE), F32),
                        pltpu.VMEM((S5_LANE_TILES, SUBLANE, LANE), F32)],
        compiler_params=_cparams("arbitrary"), name="s5_bwd" if reverse else "s5_fwd",
    )(z, h0, *consts, *extra)


def _s5_mixer(z, p, l, state_lat):
    def tiles(h):
        return h.reshape(N_SEQ, 2, S5_LANE_TILES, LANE).transpose(0, 2, 1, 3)

    def untiles(f):
        return f.transpose(0, 2, 1, 3).reshape(N_SEQ, 2, S5_GROUPS, S5_STATE)

    h0 = jnp.concatenate([jnp.zeros((BATCH, 2, 2, S5_GROUPS, S5_STATE), F32), state_lat], axis=0)
    prm = [p[n][l] for n in ('s5_lam_re', 's5_lam_im', 's5_log_step', 's5_b_re', 's5_b_im', 's5_c_re', 's5_c_im')]
    cf = _s5_direction_consts(*[t[0] for t in prm], reverse=False)
    cb = _s5_direction_consts(*[t[1] for t in prm], reverse=True)
    yf, fin_f = _s5_pass(z, tiles(h0[:, 0]), cf, False)
    extra = (yf, p['s5_d'][l].reshape(1, S5_W), p['s5_glu_w'][l], p['s5_glu_b'][l].reshape(1, S5_W))
    y, fin_b = _s5_pass(z, tiles(h0[:, 1]), cb, True, extra)
    return y, jnp.stack([untiles(fin_f), untiles(fin_b)], axis=1)


def _axial_rope_tables(L):
    n_rows = L // GRID_W
    rows = jnp.repeat(jnp.arange(n_rows, dtype=F32), GRID_W)
    cols = jnp.tile(jnp.arange(GRID_W, dtype=F32), n_rows)
    n_freq = MLA_ROPE // 4
    inv = ROPE_BASE ** (-jnp.arange(n_freq, dtype=F32) / n_freq)
    a_r, a_c = rows[:, None] * inv, cols[:, None] * inv
    ang = jnp.concatenate([a_r, a_r, a_c, a_c], axis=-1)
    return jnp.cos(ang), jnp.sin(ang)


def _dot_nt(a, b):
    return lax.dot_general(a, b, (((1,), (1,)), ((), ())), preferred_element_type=F32)


def _dot_tn(a, b):
    return lax.dot_general(a, b, (((0,), (0,)), ((), ())), preferred_element_type=F32)


def _rms(x):
    return x * lax.rsqrt(jnp.mean(x * x, axis=-1, keepdims=True) + RMS_EPS)


MLA_ROWS = 256
KV_ROWS = 512
QA_BLOCK = 256
assert Z_QA % QA_BLOCK == 0 and Z_KVA % QA_BLOCK == 0 and MLA_Q_RANK % QA_BLOCK == 0 and MLA_KV_RANK % QA_BLOCK == 0


def _kv_expand_kernel(a0_ref, a1_ref, g_ref, w_ref, ckv_ref, kn_ref, v_ref, *, normalize):
    c = jnp.concatenate([a0_ref[...], a1_ref[...]], axis=-1)
    if normalize:
        c = _rms(c) * g_ref[...]
    ckv_ref[...] = c
    kv = _dot(c.astype(BF16), w_ref[...].astype(BF16))
    half = MLA_HEADS * MLA_NOPE
    kn_ref[...] = kv[:, :half].astype(kn_ref.dtype)
    v_ref[...] = kv[:, half:].astype(v_ref.dtype)


def _kv_expand(src, col0, gain2, w_kvb_p, normalize):
    rows = src.shape[0]
    cb = col0 // QA_BLOCK
    hw = MLA_HEADS * MLA_NOPE
    return pl.pallas_call(
        functools.partial(_kv_expand_kernel, normalize=normalize), grid=(rows // KV_ROWS,),
        in_specs=[pl.BlockSpec((KV_ROWS, QA_BLOCK), lambda i: (i, cb)),
                  pl.BlockSpec((KV_ROWS, QA_BLOCK), lambda i: (i, cb + 1)),
                  pl.BlockSpec((1, MLA_KV_RANK), lambda i: (0, 0)),
                  pl.BlockSpec(w_kvb_p.shape, lambda i: (0, 0))],
        out_specs=(pl.BlockSpec((KV_ROWS, MLA_KV_RANK), lambda i: (i, 0)),
                   pl.BlockSpec((KV_ROWS, hw), lambda i: (i, 0)),
                   pl.BlockSpec((KV_ROWS, hw), lambda i: (i, 0))),
        out_shape=(jax.ShapeDtypeStruct((rows, MLA_KV_RANK), F32),
                   jax.ShapeDtypeStruct((rows, hw), BF16),
                   jax.ShapeDtypeStruct((rows, hw), BF16)),
        compiler_params=_cparams("parallel"), name="kv_expand",
    )(src, src, gain2, w_kvb_p)


def _mla_attn_kernel(*refs, n_parts, rope):
    qa_refs, (qg_ref, wn_ref, wr_ref) = refs[:3], refs[3:6]
    pos = 6
    if rope:
        wrr_ref, cq_ref, sq_ref, ck_ref, sk_ref = refs[pos:pos + 5]
        pos += 5
    parts = [refs[pos + 3 * i: pos + 3 * i + 3] for i in range(n_parts)]
    o_ref = refs[pos + 3 * n_parts]
    kcat = refs[pos + 3 * n_parts + 1:]

    @pl.when(pl.program_id(2) == 0)
    def _():
        for i, (kn_ref, kr_ref, _) in enumerate(parts):
            kr = kr_ref[...]
            if rope and i == n_parts - 1:
                lane = lax.broadcasted_iota(jnp.int32, kr.shape, 1)
                partner = jnp.where(lane % (MLA_ROPE // 2) < MLA_ROPE // 4,
                                    pltpu.roll(kr, LANE - MLA_ROPE // 4, 1), pltpu.roll(kr, MLA_ROPE // 4, 1))
                kr = kr * ck_ref[...] + partner * sk_ref[...]
            kcat[i][:, :LANE] = kn_ref[...]
            kcat[i][:, LANE:] = kr.astype(BF16)

    qa = jnp.concatenate([r[...] for r in qa_refs], axis=-1)
    qa = (_rms(qa) * qg_ref[...]).astype(BF16)
    qn = _dot(qa, wn_ref[...].astype(BF16)).astype(BF16)
    qr = _dot(qa, wr_ref[...].astype(BF16))
    if rope:
        qr = qr * cq_ref[...] + _dot(qa, wrr_ref[...].astype(BF16)) * sq_ref[...]
    q = jnp.concatenate([qn, qr.astype(BF16)], axis=-1)
    scale = (MLA_NOPE + MLA_ROPE) ** -0.5
    scores = [_dot_nt(q, kc[...]) * scale for kc in kcat]
    m = scores[0].max(axis=-1, keepdims=True)
    for s in scores[1:]:
        m = jnp.maximum(m, s.max(axis=-1, keepdims=True))
    den = 0.0
    acc = 0.0
    for s, (_, _, v_ref) in zip(scores, parts):
        e = jnp.exp(s - m)
        den = den + e.sum(axis=-1, keepdims=True)
        acc = acc + _dot(e.astype(BF16), v_ref[...])
    o_ref[...] = (acc / den).astype(o_ref.dtype)


def _mla_attention(z, q_row_block0, n_seq, n_qblk, q_consts, rope_consts, parts):
    rope = rope_consts is not None
    qcol = Z_QA // QA_BLOCK
    q_rows = lambda b, h, j: q_row_block0 + b * n_qblk + j
    in_specs = [pl.BlockSpec((MLA_ROWS, QA_BLOCK), lambda b, h, j, c=c: (q_rows(b, h, j), qcol + c))
                for c in range(MLA_Q_RANK // QA_BLOCK)]
    head_w = pl.BlockSpec((None, MLA_Q_RANK, LANE), lambda b, h, j: (h, 0, 0))
    in_specs += [pl.BlockSpec((1, MLA_Q_RANK), lambda b, h, j: (0, 0)), head_w, head_w]
    args = [z] * (MLA_Q_RANK // QA_BLOCK) + list(q_consts)
    if rope:
        w_rot, cos_t, sin_t, sin_signed = rope_consts
        q_tab = pl.BlockSpec((MLA_ROWS, LANE), lambda b, h, j: (j, 0))
        k_tab = pl.BlockSpec(cos_t.shape, lambda b, h, j: (0, 0))
        in_specs += [head_w, q_tab, q_tab, k_tab, k_tab]
        args += [w_rot, cos_t, sin_t, cos_t, sin_signed]
    for kn, kr, kr_col, v, n_keys, row_block in parts:
        in_specs += [pl.BlockSpec((n_keys, LANE), lambda b, h, j, rb=row_block: (rb(b), h)),
                     pl.BlockSpec((n_keys, LANE), lambda b, h, j, rb=row_block, cc=kr_col: (rb(b), cc)),
                     pl.BlockSpec((n_keys, LANE), lambda b, h, j, rb=row_block: (rb(b), h))]
        args += [kn, kr, v]
    return pl.pallas_call(
        functools.partial(_mla_attn_kernel, n_parts=len(parts), rope=rope),
        grid=(n_seq, MLA_HEADS, n_qblk), in_specs=in_specs,
        out_specs=pl.BlockSpec((MLA_ROWS, LANE), lambda b, h, j: (b * n_qblk + j, h)),
        out_shape=jax.ShapeDtypeStruct((n_seq * n_qblk * MLA_ROWS, MLA_HEADS * MLA_V), BF16),
        scratch_shapes=[pltpu.VMEM((part[4], 2 * LANE), BF16) for part in parts],
        compiler_params=_cparams("parallel", "parallel", "arbitrary"),
        name="mla_attn_rope" if rope else "mla_attn",
    )(*args)


def _rope_rot_cols(w):
    q = MLA_ROPE // 4
    return jnp.concatenate([-w[..., q:2 * q], w[..., :q], -w[..., 3 * q:], w[..., 2 * q:3 * q]], axis=-1)


def _mla_mixer(z, p, l, ckv_cache, kr_cache):
    H = MLA_HEADS
    wq = p['mla_wqb'][l].reshape(MLA_Q_RANK, H, MLA_NOPE + MLA_ROPE).transpose(1, 0, 2)
    w_nope = wq[..., :MLA_NOPE]
    lane_pad = lambda w: jnp.concatenate([w, jnp.zeros(w.shape[:-1] + (LANE - MLA_ROPE,), F32)], axis=-1)
    w_rope = lane_pad(wq[..., MLA_NOPE:])
    w_rope_rot = lane_pad(_rope_rot_cols(wq[..., MLA_NOPE:]))
    wkv = p['mla_wkvb'][l].reshape(MLA_KV_RANK, H, MLA_NOPE + MLA_V)
    w_kvb_p = jnp.concatenate([wkv[..., :MLA_NOPE].reshape(MLA_KV_RANK, -1),
                               wkv[..., MLA_NOPE:].reshape(MLA_KV_RANK, -1)], axis=-1)
    qg = p['mla_qa_g'][l].reshape(1, MLA_Q_RANK)
    kvg = p['mla_kva_g'][l].reshape(1, MLA_KV_RANK)

    c_kv, kn, v = _kv_expand(z, Z_KVA, kvg, w_kvb_p, True)
    _, kn_c, v_c = _kv_expand(ckv_cache.reshape(DEC_BATCH * PAST_LEN, MLA_KV_RANK), 0, kvg, w_kvb_p, False)
    kr_c = lane_pad(kr_cache.reshape(DEC_BATCH * PAST_LEN, MLA_ROPE))

    tail = Z_TAIL // LANE
    q_consts = (qg, w_nope, w_rope)
    y_ctx = _mla_attention(z, 0, BATCH, SEQ // MLA_ROWS, q_consts, None,
                           [(kn, z, tail, v, SEQ, lambda b: b)])
    cos, sin = _axial_rope_tables(DEC_SEQ)
    q4 = MLA_ROPE // 4
    sign = jnp.tile(jnp.concatenate([-jnp.ones((q4,), F32), jnp.ones((q4,), F32)]), 2)
    tab = lambda t: jnp.tile(t, (1, LANE // MLA_ROPE))
    lat0 = T_CTX // DEC_SEQ
    y_lat = _mla_attention(z, T_CTX // MLA_ROWS, DEC_BATCH, DEC_SEQ // MLA_ROWS, q_consts,
                           (w_rope_rot, tab(cos), tab(sin), tab(sin * sign)),
                           [(kn_c, kr_c, 0, v_c, PAST_LEN, lambda b: b),
                            (kn, z, tail, v, DEC_SEQ, lambda b: lat0 + b)])
    return jnp.concatenate([y_ctx, y_lat], axis=0), c_kv


GLA_SEG = S5_SEG
GLA_CHUNKS = GLA_SEG // GLA_CHUNK
GLA_QK = GLA_HEADS * GLA_DK
GLA_VW = GLA_HEADS * GLA_DV
assert Z_GQ % GLA_QK == 0 and Z_GK % GLA_QK == 0 and Z_GV % GLA_VW == 0 and Z_GR % GLA_VW == 0


def _gla_kernel(*refs, reverse):
    if reverse:
        (q_ref, k_ref, v_ref, r_ref, t_ref, wa_ref, ba_ref, s0_ref, of_ref, ng_ref,
         o_ref, fin_ref, state) = refs
    else:
        q_ref, k_ref, v_ref, t_ref, wa_ref, ba_ref, s0_ref, o_ref, fin_ref, state = refs
    i = pl.program_id(0)
    seg = (N_SEG - 1 - i) if reverse else i
    pos, n_pos = _s5_pos_in_seq(seg)
    is_start = (pos == n_pos - 1) if reverse else (pos == 0)

    @pl.when(is_start)
    def _():
        state[...] = s0_ref[...]

    log_a = jax.nn.log_sigmoid(_dot(t_ref[...].astype(BF16), wa_ref[...]) + ba_ref[...]) / GLA_GATE_TEMP
    C = GLA_CHUNK
    row = lax.broadcasted_iota(jnp.int32, (C, C), 0)
    col = lax.broadcasted_iota(jnp.int32, (C, C), 1)
    keep = (col >= row) if reverse else (col <= row)
    ones = keep.astype(F32)
    chunk_order = range(GLA_CHUNKS - 1, -1, -1) if reverse else range(GLA_CHUNKS)
    edge = 0 if reverse else C - 1

    head_out = []
    for h in range(GLA_HEADS):
        S = state[h]
        outs = [None] * GLA_CHUNKS
        for c in chunk_order:
            rows = slice(c * C, (c + 1) * C)
            dk = slice(h * GLA_DK, (h + 1) * GLA_DK)
            b = jnp.dot(ones, log_a[rows, dk], preferred_element_type=F32, precision=lax.Precision.HIGHEST)
            b_end = b[edge:edge + 1]
            q = q_ref[rows, dk] * GLA_DK ** -0.5
            k = k_ref[rows, dk]
            v = v_ref[rows, h * GLA_DV:(h + 1) * GLA_DV].astype(BF16)
            q_in = (q * jnp.exp(b)).astype(BF16)
            k_in = (k * jnp.exp(-b)).astype(BF16)
            k_end = (k * jnp.exp(b_end - b)).astype(BF16)
            s = jnp.where(keep, _dot_nt(q_in, k_in), 0.0)
            outs[c] = _dot(s.astype(BF16), v) + _dot_nt(q_in, S.astype(BF16))
            S = S * jnp.exp(b_end) + _dot_tn(v, k_end)
        state[h] = S
        head_out.append(jnp.concatenate(outs, axis=0))
    fin_ref[...] = state[...]

    if reverse:
        res = []
        for h in range(GLA_HEADS):
            dv = slice(h * GLA_DV, (h + 1) * GLA_DV)
            res.append(_rms(head_out[h] + of_ref[:, dv]) * ng_ref[...])
        r = r_ref[...]
        o_ref[...] = (jnp.concatenate(res, axis=-1) * (r * jax.nn.sigmoid(r))).astype(o_ref.dtype)
    else:
        o_ref[...] = jnp.concatenate(head_out, axis=-1)


def _gla_pass(z, s0, wa, ba, reverse, extra=()):
    seg_of = (lambda i: N_SEG - 1 - i) if reverse else (lambda i: i)
    rows = lambda width, col0: pl.BlockSpec((GLA_SEG, width), lambda i: (seg_of(i), col0 // width))
    seq_spec = pl.BlockSpec((None, GLA_HEADS, GLA_DV, GLA_DK), lambda i: (_s5_seq_of_seg(seg_of(i)), 0, 0, 0))
    whole = lambda a: pl.BlockSpec(a.shape, lambda i: (0,) * a.ndim)
    in_specs = [rows(GLA_QK, Z_GQ), rows(GLA_QK, Z_GK), rows(GLA_VW, Z_GV)]
    args = [z, z, z]
    if reverse:
        in_specs.append(rows(GLA_VW, Z_GR))
        args.append(z)
    in_specs += [rows(LANE, Z_TAIL), whole(wa), whole(ba), seq_spec]
    args += [z, wa, ba, s0]
    if reverse:
        of, ng = extra
        in_specs += [rows(GLA_VW, 0), whole(ng)]
        args += [of, ng]
    return pl.pallas_call(
        functools.partial(_gla_kernel, reverse=reverse), grid=(N_SEG,),
        in_specs=in_specs, out_specs=(rows(GLA_VW, 0), seq_spec),
        out_shape=(jax.ShapeDtypeStruct((T_ALL, GLA_VW), BF16 if reverse else F32),
                   jax.ShapeDtypeStruct((N_SEQ, GLA_HEADS, GLA_DV, GLA_DK), F32)),
        scratch_shapes=[pltpu.VMEM((GLA_HEADS, GLA_DV, GLA_DK), F32)],
        compiler_params=_cparams("arbitrary"), name="gla_bwd" if reverse else "gla_fwd",
    )(*args)


def _gla_mixer(z, p, l, state_lat):
    s0 = jnp.concatenate([jnp.zeros((BATCH, 2, GLA_HEADS, GLA_DK, GLA_DV), F32), state_lat], axis=0)
    s0 = s0.transpose(0, 1, 2, 4, 3)

    def gate_w(d, row0):
        w = jnp.zeros((LANE, GLA_QK), F32).at[row0:row0 + GLA_GATE_RANK].set(p['gla_wa2'][l, d])
        return w.astype(BF16), p['gla_ba2'][l, d].reshape(1, GLA_QK)

    of, fin_f = _gla_pass(z, s0[:, 0], *gate_w(0, TAIL_GAF), False)
    ng = p['gla_norm_g'][l].reshape(1, GLA_DV)
    y, fin_b = _gla_pass(z, s0[:, 1], *gate_w(1, TAIL_GAB), True, (of, ng))
    return y, jnp.stack([fin_f, fin_b], axis=1).transpose(0, 1, 2, 4, 3)


ROUTE_COLS = 512
ROUTE_OUT_ROWS = SUBLANE
assert TOP_K <= ROUTE_OUT_ROWS and N_EXPERTS <= LANE and EXPERTS_PER_GROUP == SUBLANE


def _route_kernel(lg_ref, b_ref, idx_ref, w_ref, rank_ref, cnt_ref, carry):
    @pl.when(pl.program_id(0) == 0)
    def _():
        carry[...] = jnp.zeros_like(carry)

    bt = lg_ref.shape[1]
    G, M = N_EXPERT_GROUPS, EXPERTS_PER_GROUP
    neg = -jnp.inf
    scores = jax.nn.sigmoid(lg_ref[...])
    x = (scores + b_ref[...]).reshape(G, M, bt)
    scores = scores.reshape(G, M, bt)
    member = lax.broadcasted_iota(jnp.int32, (G, M, bt), 1).astype(F32)
    group = lax.broadcasted_iota(jnp.int32, (G, M, bt), 0).astype(F32)
    expert = group * M + member

    m1 = jnp.max(x, axis=1, keepdims=True)
    i1 = jnp.min(jnp.where(x == m1, member, float(M)), axis=1, keepdims=True)
    m2 = jnp.max(jnp.where(member == i1, neg, x), axis=1, keepdims=True)
    gs = jnp.broadcast_to(m1 + m2, (G, M, bt))
    beaten = jnp.zeros((G, M, bt), F32)
    for g in range(G):
        o = gs[g]
        beaten = beaten + jnp.where((o > gs) | ((o == gs) & (g < group)), 1.0, 0.0)
    x = jnp.where(beaten < TOPK_GROUPS, x, neg)

    hits, picked_w = [], []
    for _ in range(TOP_K):
        m = jnp.max(jnp.max(x, axis=0), axis=0, keepdims=True)
        e = jnp.min(jnp.min(jnp.where(x == m, expert, float(N_EXPERTS)), axis=0), axis=0, keepdims=True)
        hit = expert == e
        hits.append(hit)
        picked_w.append(jnp.sum(jnp.sum(jnp.where(hit, scores, 0.0), axis=0), axis=0, keepdims=True))
        x = jnp.where(hit, neg, x)
    total = sum(picked_w)

    onehot = sum(jnp.where(h, 1.0, 0.0) for h in hits).reshape(N_EXPERTS, bt)
    earlier = (lax.broadcasted_iota(jnp.int32, (bt, bt), 0) < lax.broadcasted_iota(jnp.int32, (bt, bt), 1))
    before = (_dot(onehot.astype(BF16), earlier.astype(BF16)) + carry[...]).reshape(G, M, bt)

    out_row = lax.broadcasted_iota(jnp.int32, (ROUTE_OUT_ROWS, bt), 0)
    idx_o = jnp.full((ROUTE_OUT_ROWS, bt), -1.0, F32)
    w_o = jnp.zeros((ROUTE_OUT_ROWS, bt), F32)
    rank_o = jnp.zeros((ROUTE_OUT_ROWS, bt), F32)
    for k in range(TOP_K):
        e = jnp.sum(jnp.sum(jnp.where(hits[k], expert, 0.0), axis=0), axis=0, keepdims=True)
        r = jnp.sum(jnp.sum(jnp.where(hits[k], before, 0.0), axis=0), axis=0, keepdims=True)
        idx_o = jnp.where(out_row == k, e, idx_o)
        w_o = jnp.where(out_row == k, picked_w[k] / total * ROUTED_SCALE, w_o)
        rank_o = jnp.where(out_row == k, r, rank_o)
    idx_ref[...] = idx_o.astype(jnp.int32)
    w_ref[...] = w_o
    rank_ref[...] = rank_o.astype(jnp.int32)
    carry[...] += jnp.sum(onehot, axis=1, keepdims=True)
    cnt_ref[...] = jnp.broadcast_to(carry[...], cnt_ref.shape).astype(jnp.int32)


def _route(logits_t, router_b2):
    t = logits_t.shape[1]
    row_spec = pl.BlockSpec((ROUTE_OUT_ROWS, ROUTE_COLS), lambda i: (0, i))
    row_shape = lambda dt: jax.ShapeDtypeStruct((ROUTE_OUT_ROWS, t), dt)
    idx, w, rank, cnt = pl.pallas_call(
        _route_kernel, grid=(t // ROUTE_COLS,),
        in_specs=[pl.BlockSpec((N_EXPERTS, ROUTE_COLS), lambda i: (0, i)),
                  pl.BlockSpec((N_EXPERTS, 1), lambda i: (0, 0))],
        out_specs=(row_spec, row_spec, row_spec, pl.BlockSpec((N_EXPERTS, LANE), lambda i: (0, 0))),
        out_shape=(row_shape(jnp.int32), row_shape(F32), row_shape(jnp.int32),
                   jax.ShapeDtypeStruct((N_EXPERTS, LANE), jnp.int32)),
        scratch_shapes=[pltpu.VMEM((N_EXPERTS, 1), F32)],
        compiler_params=_cparams("arbitrary"), name="route",
    )(logits_t, router_b2)
    return idx, w, rank, cnt[:, 0]


def _dispatch(idx, rank, counts, bm):
    t = idx.shape[1]
    n_asg = t * TOP_K
    padded = (counts + bm - 1) // bm * bm
    pad_end = jnp.cumsum(padded)
    pad_start = pad_end - padded
    experts = jnp.arange(N_EXPERTS, dtype=jnp.int32)
    dest = jnp.sum(jnp.where(idx[..., None] == experts, pad_start.astype(jnp.int32), 0), axis=-1) + rank
    n_blocks = n_asg // bm + N_EXPERTS
    n_slots = n_blocks * bm
    tok = jnp.broadcast_to(jnp.arange(t, dtype=jnp.int32)[None, :], (TOP_K, t))
    slot_tok = jnp.zeros((n_slots,), jnp.int32).at[dest.reshape(-1)].set(tok.reshape(-1))
    first_row = jnp.arange(n_blocks, dtype=jnp.int32)[:, None] * bm
    blk_e = jnp.minimum(jnp.sum((pad_end[None, :] <= first_row).astype(jnp.int32), axis=1), N_EXPERTS - 1)
    n_used = (pad_end[-1] // bm).astype(jnp.int32).reshape(1)
    return slot_tok, dest, blk_e, n_used


COMBINE_ROWS = 128


def _combine_kernel(x_ref, g_ref, sh_ref, w_ref, y_ref, o_ref):
    w = w_ref[...]
    acc = sh_ref[...]
    for k in range(TOP_K):
        acc = acc + w[:, k:k + 1] * y_ref[k].astype(F32)
    o_ref[...] = x_ref[...] + g_ref[...] * acc


def _combine(x, mod3, which_gate, shared, w_tok, y_tok):
    t, d = x.shape
    tm = COMBINE_ROWS
    tile = pl.BlockSpec((tm, d), lambda i: (i, 0))
    return pl.pallas_call(
        _combine_kernel, grid=(t // tm,),
        in_specs=[tile,
                  pl.BlockSpec((None, 1, d), lambda i: (_mod_group(i, tm) * N_MOD + which_gate, 0, 0)),
                  tile,
                  pl.BlockSpec((tm, ROUTE_OUT_ROWS), lambda i: (i, 0)),
                  pl.BlockSpec((TOP_K, tm, d), lambda i: (0, i, 0))],
        out_specs=tile, out_shape=jax.ShapeDtypeStruct((t, d), F32),
        compiler_params=_cparams("parallel"), name="moe_combine",
    )(x, mod3, shared, w_tok, y_tok)


def _moe_residual(x, h2, logits_t, mod3, p, l):
    t = h2.shape[0]
    idx, w, rank, counts = _route(logits_t, p['router_b'][l].reshape(N_EXPERTS, 1))
    slot_tok, dest, blk_e, n_used = _dispatch(idx[:TOP_K], rank[:TOP_K], counts, MOE_ROWS)
    nb = blk_e.shape[0] // MOE_CHUNKS
    ys = []
    for c in range(MOE_CHUNKS):
        xg = h2[slot_tok[c * nb * MOE_ROWS:(c + 1) * nb * MOE_ROWS]]
        e_c = blk_e[c * nb:(c + 1) * nb]
        used_c = jnp.clip(n_used - c * nb, 0, nb)
        mid = _ffn_up(xg, e_c, used_c, p['exp_w_gate'], p['exp_w_up'], l, MOE_ROWS, tn=MOE_UP_COLS)
        ys.append(_ffn_down(mid, e_c, used_c, p['exp_w_down'], l, MOE_ROWS, BF16))
    y = jnp.concatenate(ys, axis=0)
    zero_e = jnp.zeros((t // SHARED_ROWS,), jnp.int32)
    all_used = jnp.full((1,), t // SHARED_ROWS, jnp.int32)
    mid_s = _ffn_up(h2, zero_e, all_used, p['sh_w_gate'][:, None], p['sh_w_up'][:, None], l, SHARED_ROWS)
    shared = _ffn_down(mid_s, zero_e, all_used, p['sh_w_down'][:, None], l, SHARED_ROWS, F32)
    return _combine(x, mod3, 5, shared, w.T, y[dest])


def kernel(x_prompt, x_sample, c, cache_mla_ckv, cache_mla_krope, state_s5, state_gla, c_ctx, mod_w, mod_b, norm1_g, norm2_g, w_in, hy_conv_w, hy_conv_b, hy_w1, hy_b1, hy_w2, hy_b2, hy_w3, hy_freq, hy_decay, hy_skip, s5_lam_re, s5_lam_im, s5_log_step, s5_b_re, s5_b_im, s5_c_re, s5_c_im, s5_d, s5_glu_w, s5_glu_b, mla_qa_g, mla_wqb, mla_kva_g, mla_wkvb, gla_wa2, gla_ba2, gla_norm_g, w_branch, w_bgate, b_bgate, w_out, router_w, router_b, exp_w_gate, exp_w_up, exp_w_down, sh_w_gate, sh_w_up, sh_w_down, final_g):
    p = dict(locals())
    x = jnp.concatenate([x_prompt.reshape(T_CTX, D_MODEL), x_sample.reshape(T_LAT, D_MODEL)], axis=0)

    cvec = jnp.concatenate([c_ctx[None], c, jnp.zeros((ADALN_ROWS - N_MOD_GROUPS, D_MODEL), F32)], axis=0)
    cvec = jax.nn.silu(cvec).astype(BF16)
    mod_b3 = mod_b.reshape(DEPTH, 1, N_MOD * D_MODEL)
    norm1_g3 = norm1_g.reshape(DEPTH, 1, D_MODEL)
    norm2_g3 = norm2_g.reshape(DEPTH, 1, D_MODEL)
    b_bgate3 = b_bgate.reshape(DEPTH * N_BRANCH, 1, D_MODEL)
    w_in_p = _permute_w_in(w_in)
    router_w_t = router_w.transpose(0, 2, 1)
    dfts = {L: _dft_matrices(L) for L in (SEQ, DEC_SEQ)}

    new_ckv, new_kr, new_s5, new_gla = [], [], [], []
    for l in range(DEPTH):
        mod = _adaln(cvec, mod_w, mod_b3, l)[:N_MOD_GROUPS]
        mod3 = mod.reshape(N_MOD_GROUPS * N_MOD, 1, D_MODEL)
        h = _norm_mod(x, norm1_g3, mod3, l, which_scale=1, which_shift=0)
        z = _in_proj(h, w_in_p, l)
        y_hy = _hyena_mixer(z, p, l, dfts)
        y_s5, s5f = _s5_mixer(z, p, l, state_s5[:, l])
        y_mla, c_kv = _mla_mixer(z, p, l, cache_mla_ckv[:, l], cache_mla_krope[:, l])
        y_gla, glaf = _gla_mixer(z, p, l, state_gla[:, l])
        new_ckv.append(c_kv[:T_CTX].reshape(BATCH, SEQ, MLA_KV_RANK))
        new_kr.append(z[:T_CTX, Z_TAIL + TAIL_KROPE:Z_TAIL + TAIL_KROPE + MLA_ROPE].reshape(BATCH, SEQ, MLA_ROPE))
        new_s5.append(s5f[:BATCH])
        new_gla.append(glaf[:BATCH])
        ys = jnp.stack([y_hy, y_s5, y_mla, y_gla], axis=0)
        m = _merge(h, ys, w_bgate, b_bgate3, w_branch, l)
        x = _out_proj(m, w_out, x, mod3, l, which_gate=2)
        h2, logits_t = _norm_mod(x, norm2_g3, mod3, l, which_scale=4, which_shift=3, router_w=router_w_t)
        x = _moe_residual(x, h2, logits_t, mod3, p, l)

    y = _final_norm(x, final_g.reshape(1, D_MODEL))
    y_prompt = y[:T_CTX].reshape(BATCH, SEQ, D_MODEL)
    y_sample = y[T_CTX:].reshape(DEC_BATCH, DEC_SEQ, D_MODEL)
    return (y_prompt, y_sample, jnp.stack(new_ckv, axis=1), jnp.stack(new_kr, axis=1),
            jnp.stack(new_s5, axis=1), jnp.stack(new_gla, axis=1))
```

```python
import functools
import math

import jax
import jax.numpy as jnp
from jax import lax
from jax.experimental import pallas as pl
from jax.experimental.pallas import tpu as pltpu

D_MODEL = 4096
BATCH = 32
SEQ = 256
DEPTH = 2
DEC_BATCH = 2
DEC_SEQ = 4096
PAST_LEN = 256
GRID_W = 64
RMS_EPS = 1e-6
N_BRANCH = 4
BRANCH_W = 1024
HY_W = BRANCH_W
HY_POS_EMB = 33
HY_BANDS = (HY_POS_EMB - 1) // 2
HY_FFN = 64
S5_W = BRANCH_W
S5_GROUP = 16
S5_GROUPS = S5_W // S5_GROUP
S5_STATE = 64
MLA_HEADS = 8
MLA_NOPE = 128
MLA_ROPE = 64
MLA_V = BRANCH_W // MLA_HEADS
MLA_Q_RANK = 768
MLA_KV_RANK = 512
ROPE_BASE = 10000.0
Q_BLOCK = 128
GLA_HEADS = 4
GLA_DK = 128
GLA_DV = BRANCH_W // GLA_HEADS
GLA_GATE_RANK = 16
GLA_GATE_TEMP = 16.0
GLA_CHUNK = 64
N_EXPERTS = 64
TOP_K = 6
N_EXPERT_GROUPS = 8
EXPERTS_PER_GROUP = N_EXPERTS // N_EXPERT_GROUPS
TOPK_GROUPS = 4
D_EXPERT = 1024
D_SHARED = 1024
ROUTED_SCALE = 2.5
IN_SPLITS = (3 * HY_W, S5_W, MLA_Q_RANK, MLA_KV_RANK, MLA_ROPE,
             GLA_HEADS * GLA_DK, GLA_HEADS * GLA_DK, GLA_HEADS * GLA_DV, GLA_HEADS * GLA_DV,
             GLA_GATE_RANK, GLA_GATE_RANK)
N_IN = sum(IN_SPLITS)

T_CTX = BATCH * SEQ
T_LAT = DEC_BATCH * DEC_SEQ
T_ALL = T_CTX + T_LAT
N_MOD_GROUPS = 1 + DEC_BATCH
N_MOD = 6

F32 = jnp.float32
BF16 = jnp.bfloat16

LANE = 128
SUBLANE = 8

Z_HY = 0
Z_S5 = Z_HY + 3 * HY_W
Z_GV = Z_S5 + S5_W
Z_GR = Z_GV + GLA_HEADS * GLA_DV
Z_GQ = Z_GR + GLA_HEADS * GLA_DV
Z_GK = Z_GQ + GLA_HEADS * GLA_DK
Z_QA = Z_GK + GLA_HEADS * GLA_DK
Z_KVA = Z_QA + MLA_Q_RANK
Z_TAIL = Z_KVA + MLA_KV_RANK
TAIL_KROPE = 0
TAIL_GAF = TAIL_KROPE + MLA_ROPE
TAIL_GAB = TAIL_GAF + GLA_GATE_RANK
N_Z = Z_TAIL + LANE
assert Z_TAIL % LANE == 0 and TAIL_GAB + GLA_GATE_RANK <= LANE


def _permute_w_in(w_in):
    o = [0]
    for s in IN_SPLITS:
        o.append(o[-1] + s)
    hy, s5, qa, kva, krope, gq, gk, gv, gr, gaf, gab = range(len(IN_SPLITS))
    parts = [w_in[..., o[i]:o[i + 1]] for i in (hy, s5, gv, gr, gq, gk, qa, kva, krope, gaf, gab)]
    pad = jnp.zeros(w_in.shape[:-1] + (N_Z - N_IN,), w_in.dtype)
    return jnp.concatenate(parts + [pad], axis=-1)


VMEM_LIMIT_BYTES = 56 * 1024 * 1024
ROW_TILE = 1024
MOE_ROWS = 512
MOE_CHUNKS = 4
SHARED_ROWS = 1024
ADALN_ROWS = 16


def _cparams(*sem):
    return pltpu.CompilerParams(dimension_semantics=sem, vmem_limit_bytes=VMEM_LIMIT_BYTES)


def _mod_group(i, tm):
    n_ctx = T_CTX // tm
    per_req = DEC_SEQ // tm
    return jnp.where(i < n_ctx, 0, 1 + (i - n_ctx) // per_req)


def _dot(a, b):
    return jnp.dot(a, b, preferred_element_type=F32)


def _mm_bias_kernel(a_ref, w_ref, b_ref, o_ref):
    o_ref[...] = _dot(a_ref[...], w_ref[...].astype(BF16)) + b_ref[...]


def _adaln(a, mod_w, mod_b3, l, tn=1024):
    m, k = a.shape
    n = mod_w.shape[-1]
    return pl.pallas_call(
        _mm_bias_kernel,
        grid=(n // tn,),
        in_specs=[pl.BlockSpec((m, k), lambda j: (0, 0)),
                  pl.BlockSpec((None, k, tn), lambda j: (l, 0, j)),
                  pl.BlockSpec((None, 1, tn), lambda j: (l, 0, j))],
        out_specs=pl.BlockSpec((m, tn), lambda j: (0, j)),
        out_shape=jax.ShapeDtypeStruct((m, n), F32),
        compiler_params=_cparams("parallel"),
        name="adaln",
    )(a, mod_w, mod_b3)


def _norm_mod_kernel(x_ref, g_ref, sc_ref, sh_ref, o_ref):
    x = x_ref[...]
    y = x * lax.rsqrt(jnp.mean(x * x, axis=-1, keepdims=True) + RMS_EPS) * g_ref[...]
    o_ref[...] = (y * (1.0 + sc_ref[...]) + sh_ref[...]).astype(o_ref.dtype)


def _norm_mod_router_kernel(x_ref, g_ref, sc_ref, sh_ref, rw_ref, o_ref, lg_ref):
    x = x_ref[...]
    y = x * lax.rsqrt(jnp.mean(x * x, axis=-1, keepdims=True) + RMS_EPS) * g_ref[...]
    h = y * (1.0 + sc_ref[...]) + sh_ref[...]
    o_ref[...] = h.astype(o_ref.dtype)
    lg_ref[...] = lax.dot_general(rw_ref[...], h, (((1,), (1,)), ((), ())), preferred_element_type=F32,
                                  precision=lax.Precision.HIGHEST)


def _norm_mod(x, gain3, mod3, l, which_scale, which_shift, router_w=None, tm=256):
    t, d = x.shape
    in_specs = [pl.BlockSpec((tm, d), lambda i: (i, 0)),
                pl.BlockSpec((None, 1, d), lambda i: (l, 0, 0)),
                pl.BlockSpec((None, 1, d), lambda i: (_mod_group(i, tm) * N_MOD + which_scale, 0, 0)),
                pl.BlockSpec((None, 1, d), lambda i: (_mod_group(i, tm) * N_MOD + which_shift, 0, 0))]
    h_spec = pl.BlockSpec((tm, d), lambda i: (i, 0))
    h_shape = jax.ShapeDtypeStruct((t, d), BF16)
    if router_w is None:
        return pl.pallas_call(
            _norm_mod_kernel, grid=(t // tm,), in_specs=in_specs, out_specs=h_spec, out_shape=h_shape,
            compiler_params=_cparams("parallel"), name="norm_mod",
        )(x, gain3, mod3, mod3)
    return pl.pallas_call(
        _norm_mod_router_kernel, grid=(t // tm,),
        in_specs=in_specs + [pl.BlockSpec((None, N_EXPERTS, d), lambda i: (l, 0, 0))],
        out_specs=(h_spec, pl.BlockSpec((N_EXPERTS, tm), lambda i: (0, i))),
        out_shape=(h_shape, jax.ShapeDtypeStruct((N_EXPERTS, t), F32)),
        compiler_params=_cparams("parallel"), name="norm_mod_router",
    )(x, gain3, mod3, mod3, router_w)


def _final_norm_kernel(x_ref, g_ref, o_ref):
    x = x_ref[...]
    o_ref[...] = x * lax.rsqrt(jnp.mean(x * x, axis=-1, keepdims=True) + RMS_EPS) * g_ref[...]


def _final_norm(x, g2, tm=256):
    t, d = x.shape
    return pl.pallas_call(
        _final_norm_kernel, grid=(t // tm,),
        in_specs=[pl.BlockSpec((tm, d), lambda i: (i, 0)), pl.BlockSpec((1, d), lambda i: (0, 0))],
        out_specs=pl.BlockSpec((tm, d), lambda i: (i, 0)),
        out_shape=jax.ShapeDtypeStruct((t, d), F32),
        compiler_params=_cparams("parallel"), name="final_norm",
    )(x, g2)


def _mm_kernel(a_ref, w_ref, o_ref):
    o_ref[...] = _dot(a_ref[...], w_ref[...].astype(BF16)).astype(o_ref.dtype)


def _in_proj(h, w_in, l, tm=ROW_TILE, tn=512):
    t, k = h.shape
    n = w_in.shape[-1]
    return pl.pallas_call(
        _mm_kernel, grid=(t // tm, pl.cdiv(n, tn)),
        in_specs=[pl.BlockSpec((tm, k), lambda i, j: (i, 0)),
                  pl.BlockSpec((None, k, tn), lambda i, j: (l, 0, j))],
        out_specs=pl.BlockSpec((tm, tn), lambda i, j: (i, j)),
        out_shape=jax.ShapeDtypeStruct((t, n), F32),
        compiler_params=_cparams("parallel", "parallel"), name="in_proj",
    )(h, w_in)


def _merge_kernel(h_ref, wg_ref, bg_ref, y_ref, wb_ref, o_ref, acc_ref):
    i = pl.program_id(2)
    gate = jax.nn.sigmoid(_dot(h_ref[...], wg_ref[...].astype(BF16)) + bg_ref[...])
    contrib = gate * _dot(y_ref[...], wb_ref[...].astype(BF16))

    @pl.when(i == 0)
    def _():
        acc_ref[...] = contrib

    @pl.when(i > 0)
    def _():
        acc_ref[...] += contrib

    @pl.when(i == N_BRANCH - 1)
    def _():
        o_ref[...] = acc_ref[...].astype(o_ref.dtype)


def _merge(h, ys, w_bgate, b_bgate3, w_branch, l, tm=ROW_TILE, tn=256):
    t, d = h.shape
    bw = ys.shape[-1]
    return pl.pallas_call(
        _merge_kernel, grid=(t // tm, d // tn, N_BRANCH),
        in_specs=[pl.BlockSpec((tm, d), lambda m, n, i: (m, 0)),
                  pl.BlockSpec((None, None, d, tn), lambda m, n, i: (l, i, 0, n)),
                  pl.BlockSpec((None, 1, tn), lambda m, n, i: (l * N_BRANCH + i, 0, n)),
                  pl.BlockSpec((None, tm, bw), lambda m, n, i: (i, m, 0)),
                  pl.BlockSpec((None, None, bw, tn), lambda m, n, i: (l, i, 0, n))],
        out_specs=pl.BlockSpec((tm, tn), lambda m, n, i: (m, n)),
        out_shape=jax.ShapeDtypeStruct((t, d), BF16),
        scratch_shapes=[pltpu.VMEM((tm, tn), F32)],
        compiler_params=_cparams("parallel", "parallel", "arbitrary"), name="merge",
    )(h, w_bgate, b_bgate3, ys, w_branch)


def _mm_resid_kernel(a_ref, w_ref, x_ref, g_ref, o_ref):
    o_ref[...] = x_ref[...] + g_ref[...] * _dot(a_ref[...], w_ref[...].astype(BF16))


def _out_proj(m, w_out, x, mod3, l, which_gate, tm=ROW_TILE, tn=512):
    t, k = m.shape
    n = w_out.shape[-1]
    return pl.pallas_call(
        _mm_resid_kernel, grid=(t // tm, n // tn),
        in_specs=[pl.BlockSpec((tm, k), lambda i, j: (i, 0)),
                  pl.BlockSpec((None, k, tn), lambda i, j: (l, 0, j)),
                  pl.BlockSpec((tm, tn), lambda i, j: (i, j)),
                  pl.BlockSpec((None, 1, tn), lambda i, j: (_mod_group(i, tm) * N_MOD + which_gate, 0, j))],
        out_specs=pl.BlockSpec((tm, tn), lambda i, j: (i, j)),
        out_shape=jax.ShapeDtypeStruct((t, n), F32),
        compiler_params=_cparams("parallel", "parallel"), name="out_proj",
    )(m, w_out, x, mod3)


def _ffn_up_kernel(e_ref, nb_ref, x_ref, wg_ref, wu_ref, o_ref):
    @pl.when(pl.program_id(1) < nb_ref[0])
    def _():
        x = x_ref[...]
        g = _dot(x, wg_ref[...].astype(BF16))
        u = _dot(x, wu_ref[...].astype(BF16))
        o_ref[...] = (g * jax.nn.sigmoid(g) * u).astype(o_ref.dtype)

    @pl.when(pl.program_id(1) >= nb_ref[0])
    def _():
        o_ref[...] = jnp.zeros_like(o_ref)


def _ffn_up(xg, blk_e, n_used, w_gate, w_up, l, bm, tn=256):
    r, d = xg.shape
    f = w_gate.shape[-1]
    w_spec = pl.BlockSpec((None, None, d, tn), lambda j, b, e, nb: (l, e[b], 0, j))
    return pl.pallas_call(
        _ffn_up_kernel,
        grid_spec=pltpu.PrefetchScalarGridSpec(
            num_scalar_prefetch=2, grid=(f // tn, r // bm),
            in_specs=[pl.BlockSpec((bm, d), lambda j, b, e, nb: (b, 0)), w_spec, w_spec],
            out_specs=pl.BlockSpec((bm, tn), lambda j, b, e, nb: (b, j))),
        out_shape=jax.ShapeDtypeStruct((r, f), BF16),
        compiler_params=_cparams("parallel", "arbitrary"), name="ffn_up",
    )(blk_e, n_used, xg, w_gate, w_up)


def _ffn_down_kernel(e_ref, nb_ref, a_ref, w_ref, *rest):
    o_ref = rest[-1]

    @pl.when(pl.program_id(1) < nb_ref[0])
    def _():
        o_ref[...] = _dot(a_ref[...], w_ref[...].astype(BF16)).astype(o_ref.dtype)

    @pl.when(pl.program_id(1) >= nb_ref[0])
    def _():
        o_ref[...] = jnp.zeros_like(o_ref)


def _ffn_down(a, blk_e, n_used, w_down, l, bm, out_dtype, tn=1024, out_rows=None, row_block0=0, into=None):
    r, f = a.shape
    d = w_down.shape[-1]
    in_specs = [pl.BlockSpec((bm, f), lambda j, b, e, nb: (b, 0)),
                pl.BlockSpec((None, None, f, tn), lambda j, b, e, nb: (l, e[b], 0, j))]
    args = [blk_e, n_used, a, w_down]
    aliases = {}
    if into is not None:
        in_specs.append(pl.BlockSpec(memory_space=pl.ANY))
        args.append(into)
        aliases = {len(args) - 1: 0}
    return pl.pallas_call(
        _ffn_down_kernel,
        grid_spec=pltpu.PrefetchScalarGridSpec(
            num_scalar_prefetch=2, grid=(d // tn, r // bm), in_specs=in_specs,
            out_specs=pl.BlockSpec((bm, tn), lambda j, b, e, nb: (row_block0 + b, j))),
        out_shape=jax.ShapeDtypeStruct((out_rows or r, d), out_dtype),
        input_output_aliases=aliases,
        compiler_params=_cparams("parallel", "arbitrary"), name="ffn_down",
    )(*args)


HY_FREQ_TILE = 256
HY_CH_TILE = 256


def _dft_matrices(L):
    n = 2 * L
    r = jnp.arange(L, dtype=jnp.int32)[:, None]
    c = jnp.arange(L, dtype=jnp.int32)[None, :]
    ang = ((r * c) % n).astype(F32) * (2.0 * math.pi / n)
    fr = jnp.cos(ang).astype(BF16)
    msin = -jnp.sin(ang)
    fi = jnp.where(r == 0, jnp.where(c % 2 == 0, 1.0, -1.0), msin).astype(BF16)
    gi = jnp.where(c == 0, jnp.where(r % 2 == 0, 1.0, -1.0), msin).astype(BF16)
    return fr, fi, fr, gi


def _mm_bf16_kernel(a_ref, b_ref, o_ref):
    o_ref[...] = _dot(a_ref[...], b_ref[...])


def _filter_spectra(f_stack, h, L):
    n = h.shape[1]
    tn = 512
    return pl.pallas_call(
        _mm_bf16_kernel, grid=(2 * L // HY_FREQ_TILE, n // tn),
        in_specs=[pl.BlockSpec((HY_FREQ_TILE, L), lambda i, j: (i, 0)), pl.BlockSpec((L, tn), lambda i, j: (0, j))],
        out_specs=pl.BlockSpec((HY_FREQ_TILE, tn), lambda i, j: (i, j)),
        out_shape=jax.ShapeDtypeStruct((2 * L, n), F32),
        compiler_params=_cparams("parallel", "parallel"), name="hyena_filter_dft",
    )(f_stack, h)


def _hyena_filters(L, w1, b1, w2, b2, w3, freq, decay):
    hp = lax.Precision.HIGHEST
    t = jnp.linspace(0.0, 1.0, L, dtype=F32)[:, None]
    w = 2.0 * math.pi * jnp.arange(L, dtype=F32)[:, None] / L
    f = jnp.linspace(1e-4, HY_BANDS - 1, HY_BANDS, dtype=F32)[None, :]
    z = jnp.concatenate([t, jnp.cos(f * w), -jnp.sin(f * w)], axis=-1)
    h = jnp.sin(freq[0] * (jnp.dot(z, w1, precision=hp) + b1))
    h = jnp.sin(freq[1] * (jnp.dot(h, w2, precision=hp) + b2))
    h = jnp.dot(h, w3, precision=hp) * jnp.exp(-t * jnp.abs(decay))
    h = h / (jnp.sum(jnp.abs(h), axis=0, keepdims=True) + 1e-6)
    return h.reshape(L, 2, 2, HY_W)


def _hyena_spectra(L, dft, p, l):
    fr, fi, _, _ = dft
    h = _hyena_filters(L, p['hy_w1'][l], p['hy_b1'][l], p['hy_w2'][l], p['hy_b2'][l],
                       p['hy_w3'][l], p['hy_freq'][l], p['hy_decay'][l])
    hf = h[:, :, 0].reshape(L, 2 * HY_W)
    hb = h[:, :, 1].reshape(L, 2 * HY_W)
    late = jnp.concatenate([jnp.zeros_like(hb[:1]), hb[:0:-1]], axis=0)
    spec = _filter_spectra(jnp.concatenate([fr, fi], axis=0), jnp.concatenate([hf, late], axis=1).astype(BF16), L)
    k = jnp.arange(L)[:, None]
    sign = jnp.where(k % 2 == 0, 1.0, -1.0)
    kr = spec[:L, :2 * HY_W] + sign * spec[:L, 2 * HY_W:]
    ki = spec[L:, :2 * HY_W] + sign * spec[L:, 2 * HY_W:]
    scale = jnp.where(k == 0, 1.0, 2.0) / (2 * L)
    a = kr * scale
    b = jnp.where(k == 0, 0.0, ki * scale)
    d = jnp.where(k == 0, ki, kr) * scale
    return [tuple(m[:, o * HY_W:(o + 1) * HY_W] for m in (a, b, d)) for o in range(2)]


def _short_conv_rows(x, w_ref, b_ref):
    n = x.shape[0]
    row = lax.broadcasted_iota(jnp.int32, x.shape, 0)
    prev = jnp.where(row == 0, 0.0, pltpu.roll(x, 1, 0))
    nxt = jnp.where(row == n - 1, 0.0, pltpu.roll(x, n - 1, 0))
    return prev * w_ref[0:1, :] + x * w_ref[1:2, :] + nxt * w_ref[2:3, :] + b_ref[...]


def _hyena_conv_kernel(*refs, conv_input):
    if conv_input:
        (a_ref, wa_ref, ba_ref, g_ref, wg_ref, bg_ref, skip_ref, fr_ref, fi_ref, gr_ref, gi_ref,
         sa_ref, sb_ref, sd_ref, o_ref, u_f32, u_bf, acc) = refs
    else:
        (a_ref, g_ref, wg_ref, bg_ref, skip_ref, fr_ref, fi_ref, gr_ref, gi_ref,
         sa_ref, sb_ref, sd_ref, o_ref, u_f32, u_bf, acc) = refs
    f = pl.program_id(2)

    @pl.when(f == 0)
    def _():
        u = a_ref[...]
        if conv_input:
            u = _short_conv_rows(u, wa_ref, ba_ref)
        u_f32[...] = u
        u_bf[...] = u.astype(BF16)
        acc[...] = jnp.zeros_like(acc)

    ub = u_bf[...]
    ur = _dot(fr_ref[...], ub)
    ui = _dot(fi_ref[...], ub)
    pr = ur * sa_ref[...] - ui * sb_ref[...]
    pi = ur * sb_ref[...] + ui * sd_ref[...]
    acc[...] += _dot(gr_ref[...], pr.astype(BF16)) + _dot(gi_ref[...], pi.astype(BF16))

    @pl.when(f == pl.num_programs(2) - 1)
    def _():
        gate = _short_conv_rows(g_ref[...], wg_ref, bg_ref)
        o_ref[...] = (gate * (acc[...] + skip_ref[...] * u_f32[...])).astype(o_ref.dtype)


def _hyena_conv(L, n_seq, tc, dft, spectra, skip2, a, a_row0, a_col0, a_conv, g, g_row0, g_col0, g_conv, out_dtype):
    fr, fi, gr, gi = dft
    n_f = L // HY_FREQ_TILE
    once = pl.Buffered(1) if L > HY_FREQ_TILE else None
    seq = lambda arr_row0, col0: pl.BlockSpec((L, tc), lambda b, c, f: (arr_row0 + b, col0 // tc + c),
                                              **({'pipeline_mode': once} if once else {}))
    chan = lambda rows: pl.BlockSpec((rows, tc), lambda b, c, f: (0, c))
    in_specs, args = [seq(a_row0, a_col0)], [a]
    if a_conv is not None:
        in_specs += [chan(3), chan(1)]
        args += list(a_conv)
    in_specs += [seq(g_row0, g_col0), chan(3), chan(1), chan(1),
                 pl.BlockSpec((HY_FREQ_TILE, L), lambda b, c, f: (f, 0)),
                 pl.BlockSpec((HY_FREQ_TILE, L), lambda b, c, f: (f, 0)),
                 pl.BlockSpec((L, HY_FREQ_TILE), lambda b, c, f: (0, f)),
                 pl.BlockSpec((L, HY_FREQ_TILE), lambda b, c, f: (0, f))]
    args += [g, *g_conv, skip2, fr, fi, gr, gi]
    in_specs += [pl.BlockSpec((HY_FREQ_TILE, tc), lambda b, c, f: (f, c))] * 3
    args += list(spectra)
    return pl.pallas_call(
        functools.partial(_hyena_conv_kernel, conv_input=a_conv is not None),
        grid=(n_seq, HY_W // tc, n_f), in_specs=in_specs,
        out_specs=pl.BlockSpec((L, tc), lambda b, c, f: (b, c), **({'pipeline_mode': once} if once else {})),
        out_shape=jax.ShapeDtypeStruct((n_seq * L, HY_W), out_dtype),
        scratch_shapes=[pltpu.VMEM((L, tc), F32), pltpu.VMEM((L, tc), BF16), pltpu.VMEM((L, tc), F32)],
        compiler_params=_cparams("parallel", "parallel", "arbitrary"), name="hyena_conv",
    )(*args)


def _hyena_mixer(z, p, l, dfts):
    cw, cb = p['hy_conv_w'][l], p['hy_conv_b'][l].reshape(1, 3 * HY_W)
    third = lambda i: (cw[:, i * HY_W:(i + 1) * HY_W], cb[:, i * HY_W:(i + 1) * HY_W])
    skip = p['hy_skip'][l]
    outs = []
    for L, n_seq, row0, tc in ((SEQ, BATCH, 0, HY_W), (DEC_SEQ, DEC_BATCH, T_CTX // DEC_SEQ, HY_CH_TILE)):
        spectra = _hyena_spectra(L, dfts[L], p, l)
        y1 = _hyena_conv(L, n_seq, tc, dfts[L], spectra[0], skip[0:1], z, row0, Z_HY + 2 * HY_W, third(2),
                         z, row0, Z_HY, third(0), F32)
        outs.append(_hyena_conv(L, n_seq, tc, dfts[L], spectra[1], skip[1:2], y1, 0, 0, None,
                                z, row0, Z_HY + HY_W, third(1), BF16))
    return jnp.concatenate(outs, axis=0)


S5_SEG = 256
S5_LANES = S5_GROUPS * S5_STATE
S5_LANE_TILES = S5_LANES // LANE
S5_CHUNKS = S5_W // LANE
S5_CHUNK_STATES = S5_LANES // S5_CHUNKS
S5_TILES_PER_CHUNK = S5_CHUNK_STATES // LANE
S5_SCAN_WIDTH = 8
N_SEQ = BATCH + DEC_BATCH
N_SEG = T_ALL // S5_SEG
N_SEG_CTX = T_CTX // S5_SEG
SEGS_PER_CTX = SEQ // S5_SEG
SEGS_PER_LAT = DEC_SEQ // S5_SEG


def _s5_seq_of_seg(seg):
    return jnp.where(seg < N_SEG_CTX, seg // SEGS_PER_CTX, BATCH + (seg - N_SEG_CTX) // SEGS_PER_LAT)


def _s5_pos_in_seq(seg):
    in_ctx = seg < N_SEG_CTX
    pos = jnp.where(in_ctx, seg % SEGS_PER_CTX, (seg - N_SEG_CTX) % SEGS_PER_LAT)
    return pos, jnp.where(in_ctx, SEGS_PER_CTX, SEGS_PER_LAT)


def _s5_kernel(*refs, reverse):
    if reverse:
        (u_ref, h0_ref, sc_ref, bre_ref, bim_ref, cre_ref, cim_ref, yf_ref, d_ref, gw_ref, gb_ref,
         y_ref, fin_ref, bu_re, bu_im, state) = refs
    else:
        (u_ref, h0_ref, sc_ref, bre_ref, bim_ref, cre_ref, cim_ref,
         y_ref, fin_ref, bu_re, bu_im, state) = refs
    i = pl.program_id(0)
    seg = (N_SEG - 1 - i) if reverse else i
    pos, n_pos = _s5_pos_in_seq(seg)
    is_start = (pos == n_pos - 1) if reverse else (pos == 0)

    @pl.when(is_start)
    def _():
        state[:, 0:2, :] = h0_ref[...]

    u = u_ref[...]
    ub = u.astype(BF16)
    for j in range(S5_CHUNKS):
        uj = ub[:, j * LANE:(j + 1) * LANE]
        pre = _dot(uj, bre_ref[j])
        pim = _dot(uj, bim_ref[j])
        for q in range(S5_TILES_PER_CHUNK):
            bu_re[j * S5_TILES_PER_CHUNK + q] = pre[:, q * LANE:(q + 1) * LANE]
            bu_im[j * S5_TILES_PER_CHUNK + q] = pim[:, q * LANE:(q + 1) * LANE]

    n_row_tiles = S5_SEG // SUBLANE

    last = 0 if reverse else SUBLANE - 1

    def lane_body(lg, _):
        lts = [lg * S5_SCAN_WIDTH + q for q in range(S5_SCAN_WIDTH)]
        s0 = tuple(state[lt, r:r + 1, :] for lt in lts for r in (0, 1))

        def row_body(jr, carry):
            rt = (n_row_tiles - 1 - jr) if reverse else jr
            rows = pl.ds(pl.multiple_of(rt * SUBLANE, SUBLANE), SUBLANE)
            out = []
            for q, lt in enumerate(lts):
                s_re, s_im = carry[2 * q], carry[2 * q + 1]
                xr = bu_re[lt, rows, :]
                xi = bu_im[lt, rows, :]
                for n, k in enumerate((1, 2, 4)):
                    mr, mi = sc_ref[2 * n, lt], sc_ref[2 * n + 1, lt]
                    shift = (SUBLANE - k) if reverse else k
                    rr = pltpu.roll(xr, shift, 0)
                    ri = pltpu.roll(xi, shift, 0)
                    xr, xi = xr + mr * rr - mi * ri, xi + mr * ri + mi * rr
                pr, pi = sc_ref[6, lt], sc_ref[7, lt]
                br = jnp.broadcast_to(s_re, (SUBLANE, LANE))
                bi = jnp.broadcast_to(s_im, (SUBLANE, LANE))
                xr, xi = xr + pr * br - pi * bi, xi + pr * bi + pi * br
                bu_re[lt, rows, :] = xr
                bu_im[lt, rows, :] = xi
                out += [xr[last:last + 1, :], xi[last:last + 1, :]]
            return tuple(out)

        fin = lax.fori_loop(0, n_row_tiles, row_body, s0)
        for q, lt in enumerate(lts):
            state[lt, 0:1, :] = fin[2 * q]
            state[lt, 1:2, :] = fin[2 * q + 1]
        return 0

    lax.fori_loop(0, S5_LANE_TILES // S5_SCAN_WIDTH, lane_body, 0)
    fin_ref[...] = state[:, 0:2, :]

    ys = []
    for j in range(S5_CHUNKS):
        tiles = range(j * S5_TILES_PER_CHUNK, (j + 1) * S5_TILES_PER_CHUNK)
        sr = jnp.concatenate([bu_re[t] for t in tiles], axis=-1).astype(BF16)
        si = jnp.concatenate([bu_im[t] for t in tiles], axis=-1).astype(BF16)
        ys.append(_dot(sr, cre_ref[j]) - _dot(si, cim_ref[j]))
    y = jnp.concatenate(ys, axis=-1)
    if reverse:
        y = jax.nn.gelu(y + yf_ref[...] + u * d_ref[...])
        gate = jax.nn.sigmoid(_dot(y.astype(BF16), gw_ref[...].astype(BF16)) + gb_ref[...])
        y_ref[...] = (y * gate).astype(y_ref.dtype)
    else:
        y_ref[...] = y


def _s5_direction_consts(lam_re, lam_im, log_step, b_re, b_im, c_re, c_im, reverse):
    lr = jnp.minimum(lam_re, -1e-4)
    dt = jnp.exp(log_step)[:, None]
    mag, ang = jnp.exp(lr * dt), lam_im * dt
    ar, ai = mag * jnp.cos(ang), mag * jnp.sin(ang)
    den = lr * lr + lam_im * lam_im
    kr = ((ar - 1.0) * lr + ai * lam_im) / den
    ki = (ai * lr - (ar - 1.0) * lam_im) / den
    bb_re = kr[..., None] * b_re - ki[..., None] * b_im
    bb_im = kr[..., None] * b_im + ki[..., None] * b_re
    gpc = S5_GROUPS // S5_CHUNKS
    eye = jnp.eye(gpc, dtype=F32)

    def in_blocks(bb):
        t = bb.reshape(S5_CHUNKS, gpc, S5_STATE, S5_GROUP).transpose(0, 1, 3, 2)
        return jnp.einsum('jgsp,gh->jgshp', t, eye).reshape(S5_CHUNKS, LANE, S5_CHUNK_STATES).astype(BF16)

    def out_blocks(cc):
        t = cc.reshape(S5_CHUNKS, gpc, S5_GROUP, S5_STATE).transpose(0, 1, 3, 2)
        return jnp.einsum('jgps,gh->jgphs', t, eye).reshape(S5_CHUNKS, S5_CHUNK_STATES, LANE).astype(BF16)

    def power(k):
        m = jnp.exp(lr * dt * k)
        return (m * jnp.cos(ang * k)).reshape(-1), (m * jnp.sin(ang * k)).reshape(-1)

    r = jnp.arange(SUBLANE)
    tiles = []
    for k in (1, 2, 4):
        keep = ((r < SUBLANE - k) if reverse else (r >= k)).astype(F32)[:, None]
        pr, pi = power(float(k))
        tiles += [keep * pr[None], keep * pi[None]]
    expo = ((SUBLANE - r) if reverse else (r + 1)).astype(F32)[:, None]
    m = jnp.exp((lr * dt).reshape(-1)[None] * expo)
    a = ang.reshape(-1)[None] * expo
    tiles += [m * jnp.cos(a), m * jnp.sin(a)]
    sc = jnp.stack(tiles).reshape(8, SUBLANE, S5_LANE_TILES, LANE).transpose(0, 2, 1, 3)
    return sc, in_blocks(bb_re), in_blocks(bb_im), out_blocks(c_re), out_blocks(c_im)


def _s5_pass(z, h0, consts, reverse, extra=()):
    seg_of = (lambda i: N_SEG - 1 - i) if reverse else (lambda i: i)
    u_col = Z_S5 // S5_W
    row_spec = lambda col: pl.BlockSpec((S5_SEG, S5_W), lambda i: (seg_of(i), col))
    seq_spec = pl.BlockSpec((None, S5_LANE_TILES, 2, LANE), lambda i: (_s5_seq_of_seg(seg_of(i)), 0, 0, 0))
    whole = lambda a: pl.BlockSpec(a.shape, lambda i: (0,) * a.ndim)
    in_specs = [row_spec(u_col), seq_spec] + [whole(a) for a in consts]
    if reverse:
        yf, d2, glu_w, glu_b2 = extra
        in_specs += [row_spec(0), whole(d2), whole(glu_w), whole(glu_b2)]
    return pl.pallas_call(
        functools.partial(_s5_kernel, reverse=reverse), grid=(N_SEG,),
        in_specs=in_specs,
        out_specs=(row_spec(0), seq_spec),
        out_shape=(jax.ShapeDtypeStruct((T_ALL, S5_W), BF16 if reverse else F32),
                   jax.ShapeDtypeStruct((N_SEQ, S5_LANE_TILES, 2, LANE), F32)),
        scratch_shapes=[pltpu.VMEM((S5_LANE_TILES, S5_SEG, LANE), F32),
                        pltpu.VMEM((S5_LANE_TILES, S5_SEG, LANE), F32),
                        pltpu.VMEM((S5_LANE_TILES, SUBLANE, LANE), F32)],
        compiler_params=_cparams("arbitrary"), name="s5_bwd" if reverse else "s5_fwd",
    )(z, h0, *consts, *extra)


def _s5_mixer(z, p, l, state_lat):
    def tiles(h):
        return h.reshape(N_SEQ, 2, S5_LANE_TILES, LANE).transpose(0, 2, 1, 3)

    def untiles(f):
        return f.transpose(0, 2, 1, 3).reshape(N_SEQ, 2, S5_GROUPS, S5_STATE)

    h0 = jnp.concatenate([jnp.zeros((BATCH, 2, 2, S5_GROUPS, S5_STATE), F32), state_lat], axis=0)
    prm = [p[n][l] for n in ('s5_lam_re', 's5_lam_im', 's5_log_step', 's5_b_re', 's5_b_im', 's5_c_re', 's5_c_im')]
    cf = _s5_direction_consts(*[t[0] for t in prm], reverse=False)
    cb = _s5_direction_consts(*[t[1] for t in prm], reverse=True)
    yf, fin_f = _s5_pass(z, tiles(h0[:, 0]), cf, False)
    extra = (yf, p['s5_d'][l].reshape(1, S5_W), p['s5_glu_w'][l], p['s5_glu_b'][l].reshape(1, S5_W))
    y, fin_b = _s5_pass(z, tiles(h0[:, 1]), cb, True, extra)
    return y, jnp.stack([untiles(fin_f), untiles(fin_b)], axis=1)


def _axial_rope_tables(L):
    n_rows = L // GRID_W
    rows = jnp.repeat(jnp.arange(n_rows, dtype=F32), GRID_W)
    cols = jnp.tile(jnp.arange(GRID_W, dtype=F32), n_rows)
    n_freq = MLA_ROPE // 4
    inv = ROPE_BASE ** (-jnp.arange(n_freq, dtype=F32) / n_freq)
    a_r, a_c = rows[:, None] * inv, cols[:, None] * inv
    ang = jnp.concatenate([a_r, a_r, a_c, a_c], axis=-1)
    return jnp.cos(ang), jnp.sin(ang)


def _dot_nt(a, b):
    return lax.dot_general(a, b, (((1,), (1,)), ((), ())), preferred_element_type=F32)


def _dot_tn(a, b):
    return lax.dot_general(a, b, (((0,), (0,)), ((), ())), preferred_element_type=F32)


def _rms(x):
    return x * lax.rsqrt(jnp.mean(x * x, axis=-1, keepdims=True) + RMS_EPS)


MLA_ROWS = 256
KV_ROWS = 512
QA_BLOCK = 256
assert Z_QA % QA_BLOCK == 0 and Z_KVA % QA_BLOCK == 0 and MLA_Q_RANK % QA_BLOCK == 0 and MLA_KV_RANK % QA_BLOCK == 0


def _kv_expand_kernel(a0_ref, a1_ref, g_ref, w_ref, ckv_ref, kn_ref, v_ref, *, normalize):
    c = jnp.concatenate([a0_ref[...], a1_ref[...]], axis=-1)
    if normalize:
        c = _rms(c) * g_ref[...]
    ckv_ref[...] = c
    kv = _dot(c.astype(BF16), w_ref[...].astype(BF16))
    half = MLA_HEADS * MLA_NOPE
    kn_ref[...] = kv[:, :half].astype(kn_ref.dtype)
    v_ref[...] = kv[:, half:].astype(v_ref.dtype)


def _kv_expand(src, col0, gain2, w_kvb_p, normalize):
    rows = src.shape[0]
    cb = col0 // QA_BLOCK
    hw = MLA_HEADS * MLA_NOPE
    return pl.pallas_call(
        functools.partial(_kv_expand_kernel, normalize=normalize), grid=(rows // KV_ROWS,),
        in_specs=[pl.BlockSpec((KV_ROWS, QA_BLOCK), lambda i: (i, cb)),
                  pl.BlockSpec((KV_ROWS, QA_BLOCK), lambda i: (i, cb + 1)),
                  pl.BlockSpec((1, MLA_KV_RANK), lambda i: (0, 0)),
                  pl.BlockSpec(w_kvb_p.shape, lambda i: (0, 0))],
        out_specs=(pl.BlockSpec((KV_ROWS, MLA_KV_RANK), lambda i: (i, 0)),
                   pl.BlockSpec((KV_ROWS, hw), lambda i: (i, 0)),
                   pl.BlockSpec((KV_ROWS, hw), lambda i: (i, 0))),
        out_shape=(jax.ShapeDtypeStruct((rows, MLA_KV_RANK), F32),
                   jax.ShapeDtypeStruct((rows, hw), BF16),
                   jax.ShapeDtypeStruct((rows, hw), BF16)),
        compiler_params=_cparams("parallel"), name="kv_expand",
    )(src, src, gain2, w_kvb_p)


def _mla_attn_kernel(*refs, n_parts, rope):
    qa_refs, (qg_ref, wn_ref, wr_ref) = refs[:3], refs[3:6]
    pos = 6
    if rope:
        wrr_ref, cq_ref, sq_ref, ck_ref, sk_ref = refs[pos:pos + 5]
        pos += 5
    parts = [refs[pos + 3 * i: pos + 3 * i + 3] for i in range(n_parts)]
    o_ref = refs[pos + 3 * n_parts]
    kcat = refs[pos + 3 * n_parts + 1:]

    @pl.when(pl.program_id(2) == 0)
    def _():
        for i, (kn_ref, kr_ref, _) in enumerate(parts):
            kr = kr_ref[...]
            if rope and i == n_parts - 1:
                lane = lax.broadcasted_iota(jnp.int32, kr.shape, 1)
                partner = jnp.where(lane % (MLA_ROPE // 2) < MLA_ROPE // 4,
                                    pltpu.roll(kr, LANE - MLA_ROPE // 4, 1), pltpu.roll(kr, MLA_ROPE // 4, 1))
                kr = kr * ck_ref[...] + partner * sk_ref[...]
            kcat[i][:, :LANE] = kn_ref[...]
            kcat[i][:, LANE:] = kr.astype(BF16)

    qa = jnp.concatenate([r[...] for r in qa_refs], axis=-1)
    qa = (_rms(qa) * qg_ref[...]).astype(BF16)
    qn = _dot(qa, wn_ref[...].astype(BF16)).astype(BF16)
    qr = _dot(qa, wr_ref[...].astype(BF16))
    if rope:
        qr = qr * cq_ref[...] + _dot(qa, wrr_ref[...].astype(BF16)) * sq_ref[...]
    q = jnp.concatenate([qn, qr.astype(BF16)], axis=-1)
    scale = (MLA_NOPE + MLA_ROPE) ** -0.5
    scores = [_dot_nt(q, kc[...]) * scale for kc in kcat]
    m = scores[0].max(axis=-1, keepdims=True)
    for s in scores[1:]:
        m = jnp.maximum(m, s.max(axis=-1, keepdims=True))
    den = 0.0
    acc = 0.0
    for s, (_, _, v_ref) in zip(scores, parts):
        e = jnp.exp(s - m)
        den = den + e.sum(axis=-1, keepdims=True)
        acc = acc + _dot(e.astype(BF16), v_ref[...])
    o_ref[...] = (acc / den).astype(o_ref.dtype)


def _mla_attention(z, q_row_block0, n_seq, n_qblk, q_consts, rope_consts, parts):
    rope = rope_consts is not None
    qcol = Z_QA // QA_BLOCK
    q_rows = lambda b, h, j: q_row_block0 + b * n_qblk + j
    in_specs = [pl.BlockSpec((MLA_ROWS, QA_BLOCK), lambda b, h, j, c=c: (q_rows(b, h, j), qcol + c))
                for c in range(MLA_Q_RANK // QA_BLOCK)]
    head_w = pl.BlockSpec((None, MLA_Q_RANK, LANE), lambda b, h, j: (h, 0, 0))
    in_specs += [pl.BlockSpec((1, MLA_Q_RANK), lambda b, h, j: (0, 0)), head_w, head_w]
    args = [z] * (MLA_Q_RANK // QA_BLOCK) + list(q_consts)
    if rope:
        w_rot, cos_t, sin_t, sin_signed = rope_consts
        q_tab = pl.BlockSpec((MLA_ROWS, LANE), lambda b, h, j: (j, 0))
        k_tab = pl.BlockSpec(cos_t.shape, lambda b, h, j: (0, 0))
        in_specs += [head_w, q_tab, q_tab, k_tab, k_tab]
        args += [w_rot, cos_t, sin_t, cos_t, sin_signed]
    for kn, kr, kr_col, v, n_keys, row_block in parts:
        in_specs += [pl.BlockSpec((n_keys, LANE), lambda b, h, j, rb=row_block: (rb(b), h)),
                     pl.BlockSpec((n_keys, LANE), lambda b, h, j, rb=row_block, cc=kr_col: (rb(b), cc)),
                     pl.BlockSpec((n_keys, LANE), lambda b, h, j, rb=row_block: (rb(b), h))]
        args += [kn, kr, v]
    return pl.pallas_call(
        functools.partial(_mla_attn_kernel, n_parts=len(parts), rope=rope),
        grid=(n_seq, MLA_HEADS, n_qblk), in_specs=in_specs,
        out_specs=pl.BlockSpec((MLA_ROWS, LANE), lambda b, h, j: (b * n_qblk + j, h)),
        out_shape=jax.ShapeDtypeStruct((n_seq * n_qblk * MLA_ROWS, MLA_HEADS * MLA_V), BF16),
        scratch_shapes=[pltpu.VMEM((part[4], 2 * LANE), BF16) for part in parts],
        compiler_params=_cparams("parallel", "parallel", "arbitrary"),
        name="mla_attn_rope" if rope else "mla_attn",
    )(*args)


def _rope_rot_cols(w):
    q = MLA_ROPE // 4
    return jnp.concatenate([-w[..., q:2 * q], w[..., :q], -w[..., 3 * q:], w[..., 2 * q:3 * q]], axis=-1)


def _mla_mixer(z, p, l, ckv_cache, kr_cache):
    H = MLA_HEADS
    wq = p['mla_wqb'][l].reshape(MLA_Q_RANK, H, MLA_NOPE + MLA_ROPE).transpose(1, 0, 2)
    w_nope = wq[..., :MLA_NOPE]
    lane_pad = lambda w: jnp.concatenate([w, jnp.zeros(w.shape[:-1] + (LANE - MLA_ROPE,), F32)], axis=-1)
    w_rope = lane_pad(wq[..., MLA_NOPE:])
    w_rope_rot = lane_pad(_rope_rot_cols(wq[..., MLA_NOPE:]))
    wkv = p['mla_wkvb'][l].reshape(MLA_KV_RANK, H, MLA_NOPE + MLA_V)
    w_kvb_p = jnp.concatenate([wkv[..., :MLA_NOPE].reshape(MLA_KV_RANK, -1),
                               wkv[..., MLA_NOPE:].reshape(MLA_KV_RANK, -1)], axis=-1)
    qg = p['mla_qa_g'][l].reshape(1, MLA_Q_RANK)
    kvg = p['mla_kva_g'][l].reshape(1, MLA_KV_RANK)

    c_kv, kn, v = _kv_expand(z, Z_KVA, kvg, w_kvb_p, True)
    _, kn_c, v_c = _kv_expand(ckv_cache.reshape(DEC_BATCH * PAST_LEN, MLA_KV_RANK), 0, kvg, w_kvb_p, False)
    kr_c = lane_pad(kr_cache.reshape(DEC_BATCH * PAST_LEN, MLA_ROPE))

    tail = Z_TAIL // LANE
    q_consts = (qg, w_nope, w_rope)
    y_ctx = _mla_attention(z, 0, BATCH, SEQ // MLA_ROWS, q_consts, None,
                           [(kn, z, tail, v, SEQ, lambda b: b)])
    cos, sin = _axial_rope_tables(DEC_SEQ)
    q4 = MLA_ROPE // 4
    sign = jnp.tile(jnp.concatenate([-jnp.ones((q4,), F32), jnp.ones((q4,), F32)]), 2)
    tab = lambda t: jnp.tile(t, (1, LANE // MLA_ROPE))
    lat0 = T_CTX // DEC_SEQ
    y_lat = _mla_attention(z, T_CTX // MLA_ROWS, DEC_BATCH, DEC_SEQ // MLA_ROWS, q_consts,
                           (w_rope_rot, tab(cos), tab(sin), tab(sin * sign)),
                           [(kn_c, kr_c, 0, v_c, PAST_LEN, lambda b: b),
                            (kn, z, tail, v, DEC_SEQ, lambda b: lat0 + b)])
    return jnp.concatenate([y_ctx, y_lat], axis=0), c_kv


GLA_SEG = S5_SEG
GLA_CHUNKS = GLA_SEG // GLA_CHUNK
GLA_QK = GLA_HEADS * GLA_DK
GLA_VW = GLA_HEADS * GLA_DV
assert Z_GQ % GLA_QK == 0 and Z_GK % GLA_QK == 0 and Z_GV % GLA_VW == 0 and Z_GR % GLA_VW == 0


def _gla_kernel(*refs, reverse):
    if reverse:
        (q_ref, k_ref, v_ref, r_ref, t_ref, wa_ref, ba_ref, s0_ref, of_ref, ng_ref,
         o_ref, fin_ref, state) = refs
    else:
        q_ref, k_ref, v_ref, t_ref, wa_ref, ba_ref, s0_ref, o_ref, fin_ref, state = refs
    i = pl.program_id(0)
    seg = (N_SEG - 1 - i) if reverse else i
    pos, n_pos = _s5_pos_in_seq(seg)
    is_start = (pos == n_pos - 1) if reverse else (pos == 0)

    @pl.when(is_start)
    def _():
        state[...] = s0_ref[...]

    log_a = jax.nn.log_sigmoid(_dot(t_ref[...].astype(BF16), wa_ref[...]) + ba_ref[...]) / GLA_GATE_TEMP
    C = GLA_CHUNK
    row = lax.broadcasted_iota(jnp.int32, (C, C), 0)
    col = lax.broadcasted_iota(jnp.int32, (C, C), 1)
    keep = (col >= row) if reverse else (col <= row)
    ones = keep.astype(F32)
    chunk_order = range(GLA_CHUNKS - 1, -1, -1) if reverse else range(GLA_CHUNKS)
    edge = 0 if reverse else C - 1

    head_out = []
    for h in range(GLA_HEADS):
        S = state[h]
        outs = [None] * GLA_CHUNKS
        for c in chunk_order:
            rows = slice(c * C, (c + 1) * C)
            dk = slice(h * GLA_DK, (h + 1) * GLA_DK)
            b = jnp.dot(ones, log_a[rows, dk], preferred_element_type=F32, precision=lax.Precision.HIGHEST)
            b_end = b[edge:edge + 1]
            q = q_ref[rows, dk] * GLA_DK ** -0.5
            k = k_ref[rows, dk]
            v = v_ref[rows, h * GLA_DV:(h + 1) * GLA_DV].astype(BF16)
            q_in = (q * jnp.exp(b)).astype(BF16)
            k_in = (k * jnp.exp(-b)).astype(BF16)
            k_end = (k * jnp.exp(b_end - b)).astype(BF16)
            s = jnp.where(keep, _dot_nt(q_in, k_in), 0.0)
            outs[c] = _dot(s.astype(BF16), v) + _dot_nt(q_in, S.astype(BF16))
            S = S * jnp.exp(b_end) + _dot_tn(v, k_end)
        state[h] = S
        head_out.append(jnp.concatenate(outs, axis=0))
    fin_ref[...] = state[...]

    if reverse:
        res = []
        for h in range(GLA_HEADS):
            dv = slice(h * GLA_DV, (h + 1) * GLA_DV)
            res.append(_rms(head_out[h] + of_ref[:, dv]) * ng_ref[...])
        r = r_ref[...]
        o_ref[...] = (jnp.concatenate(res, axis=-1) * (r * jax.nn.sigmoid(r))).astype(o_ref.dtype)
    else:
        o_ref[...] = jnp.concatenate(head_out, axis=-1)


def _gla_pass(z, s0, wa, ba, reverse, extra=()):
    seg_of = (lambda i: N_SEG - 1 - i) if reverse else (lambda i: i)
    rows = lambda width, col0: pl.BlockSpec((GLA_SEG, width), lambda i: (seg_of(i), col0 // width))
    seq_spec = pl.BlockSpec((None, GLA_HEADS, GLA_DV, GLA_DK), lambda i: (_s5_seq_of_seg(seg_of(i)), 0, 0, 0))
    whole = lambda a: pl.BlockSpec(a.shape, lambda i: (0,) * a.ndim)
    in_specs = [rows(GLA_QK, Z_GQ), rows(GLA_QK, Z_GK), rows(GLA_VW, Z_GV)]
    args = [z, z, z]
    if reverse:
        in_specs.append(rows(GLA_VW, Z_GR))
        args.append(z)
    in_specs += [rows(LANE, Z_TAIL), whole(wa), whole(ba), seq_spec]
    args += [z, wa, ba, s0]
    if reverse:
        of, ng = extra
        in_specs += [rows(GLA_VW, 0), whole(ng)]
        args += [of, ng]
    return pl.pallas_call(
        functools.partial(_gla_kernel, reverse=reverse), grid=(N_SEG,),
        in_specs=in_specs, out_specs=(rows(GLA_VW, 0), seq_spec),
        out_shape=(jax.ShapeDtypeStruct((T_ALL, GLA_VW), BF16 if reverse else F32),
                   jax.ShapeDtypeStruct((N_SEQ, GLA_HEADS, GLA_DV, GLA_DK), F32)),
        scratch_shapes=[pltpu.VMEM((GLA_HEADS, GLA_DV, GLA_DK), F32)],
        compiler_params=_cparams("arbitrary"), name="gla_bwd" if reverse else "gla_fwd",
    )(*args)


def _gla_mixer(z, p, l, state_lat):
    s0 = jnp.concatenate([jnp.zeros((BATCH, 2, GLA_HEADS, GLA_DK, GLA_DV), F32), state_lat], axis=0)
    s0 = s0.transpose(0, 1, 2, 4, 3)

    def gate_w(d, row0):
        w = jnp.zeros((LANE, GLA_QK), F32).at[row0:row0 + GLA_GATE_RANK].set(p['gla_wa2'][l, d])
        return w.astype(BF16), p['gla_ba2'][l, d].reshape(1, GLA_QK)

    of, fin_f = _gla_pass(z, s0[:, 0], *gate_w(0, TAIL_GAF), False)
    ng = p['gla_norm_g'][l].reshape(1, GLA_DV)
    y, fin_b = _gla_pass(z, s0[:, 1], *gate_w(1, TAIL_GAB), True, (of, ng))
    return y, jnp.stack([fin_f, fin_b], axis=1).transpose(0, 1, 2, 4, 3)


ROUTE_COLS = 512
ROUTE_OUT_ROWS = SUBLANE
assert TOP_K <= ROUTE_OUT_ROWS and N_EXPERTS <= LANE and EXPERTS_PER_GROUP == SUBLANE


def _route_kernel(lg_ref, b_ref, idx_ref, w_ref, rank_ref, cnt_ref, carry):
    @pl.when(pl.program_id(0) == 0)
    def _():
        carry[...] = jnp.zeros_like(carry)

    bt = lg_ref.shape[1]
    G, M = N_EXPERT_GROUPS, EXPERTS_PER_GROUP
    neg = -jnp.inf
    scores = jax.nn.sigmoid(lg_ref[...])
    x = (scores + b_ref[...]).reshape(G, M, bt)
    scores = scores.reshape(G, M, bt)
    member = lax.broadcasted_iota(jnp.int32, (G, M, bt), 1).astype(F32)
    group = lax.broadcasted_iota(jnp.int32, (G, M, bt), 0).astype(F32)
    expert = group * M + member

    m1 = jnp.max(x, axis=1, keepdims=True)
    i1 = jnp.min(jnp.where(x == m1, member, float(M)), axis=1, keepdims=True)
    m2 = jnp.max(jnp.where(member == i1, neg, x), axis=1, keepdims=True)
    gs = jnp.broadcast_to(m1 + m2, (G, M, bt))
    beaten = jnp.zeros((G, M, bt), F32)
    for g in range(G):
        o = gs[g]
        beaten = beaten + jnp.where((o > gs) | ((o == gs) & (g < group)), 1.0, 0.0)
    x = jnp.where(beaten < TOPK_GROUPS, x, neg)

    hits, picked_w = [], []
    for _ in range(TOP_K):
        m = jnp.max(jnp.max(x, axis=0), axis=0, keepdims=True)
        e = jnp.min(jnp.min(jnp.where(x == m, expert, float(N_EXPERTS)), axis=0), axis=0, keepdims=True)
        hit = expert == e
        hits.append(hit)
        picked_w.append(jnp.sum(jnp.sum(jnp.where(hit, scores, 0.0), axis=0), axis=0, keepdims=True))
        x = jnp.where(hit, neg, x)
    total = sum(picked_w)

    onehot = sum(jnp.where(h, 1.0, 0.0) for h in hits).reshape(N_EXPERTS, bt)
    earlier = (lax.broadcasted_iota(jnp.int32, (bt, bt), 0) < lax.broadcasted_iota(jnp.int32, (bt, bt), 1))
    before = (_dot(onehot.astype(BF16), earlier.astype(BF16)) + carry[...]).reshape(G, M, bt)

    out_row = lax.broadcasted_iota(jnp.int32, (ROUTE_OUT_ROWS, bt), 0)
    idx_o = jnp.full((ROUTE_OUT_ROWS, bt), -1.0, F32)
    w_o = jnp.zeros((ROUTE_OUT_ROWS, bt), F32)
    rank_o = jnp.zeros((ROUTE_OUT_ROWS, bt), F32)
    for k in range(TOP_K):
        e = jnp.sum(jnp.sum(jnp.where(hits[k], expert, 0.0), axis=0), axis=0, keepdims=True)
        r = jnp.sum(jnp.sum(jnp.where(hits[k], before, 0.0), axis=0), axis=0, keepdims=True)
        idx_o = jnp.where(out_row == k, e, idx_o)
        w_o = jnp.where(out_row == k, picked_w[k] / total * ROUTED_SCALE, w_o)
        rank_o = jnp.where(out_row == k, r, rank_o)
    idx_ref[...] = idx_o.astype(jnp.int32)
    w_ref[...] = w_o
    rank_ref[...] = rank_o.astype(jnp.int32)
    carry[...] += jnp.sum(onehot, axis=1, keepdims=True)
    cnt_ref[...] = jnp.broadcast_to(carry[...], cnt_ref.shape).astype(jnp.int32)


def _route(logits_t, router_b2):
    t = logits_t.shape[1]
    row_spec = pl.BlockSpec((ROUTE_OUT_ROWS, ROUTE_COLS), lambda i: (0, i))
    row_shape = lambda dt: jax.ShapeDtypeStruct((ROUTE_OUT_ROWS, t), dt)
    idx, w, rank, cnt = pl.pallas_call(
        _route_kernel, grid=(t // ROUTE_COLS,),
        in_specs=[pl.BlockSpec((N_EXPERTS, ROUTE_COLS), lambda i: (0, i)),
                  pl.BlockSpec((N_EXPERTS, 1), lambda i: (0, 0))],
        out_specs=(row_spec, row_spec, row_spec, pl.BlockSpec((N_EXPERTS, LANE), lambda i: (0, 0))),
        out_shape=(row_shape(jnp.int32), row_shape(F32), row_shape(jnp.int32),
                   jax.ShapeDtypeStruct((N_EXPERTS, LANE), jnp.int32)),
        scratch_shapes=[pltpu.VMEM((N_EXPERTS, 1), F32)],
        compiler_params=_cparams("arbitrary"), name="route",
    )(logits_t, router_b2)
    return idx, w, rank, cnt[:, 0]


def _dispatch(idx, rank, counts, bm):
    t = idx.shape[1]
    n_asg = t * TOP_K
    padded = (counts + bm - 1) // bm * bm
    pad_end = jnp.cumsum(padded)
    pad_start = pad_end - padded
    experts = jnp.arange(N_EXPERTS, dtype=jnp.int32)
    dest = jnp.sum(jnp.where(idx[..., None] == experts, pad_start.astype(jnp.int32), 0), axis=-1) + rank
    n_blocks = n_asg // bm + N_EXPERTS
    n_slots = n_blocks * bm
    tok = jnp.broadcast_to(jnp.arange(t, dtype=jnp.int32)[None, :], (TOP_K, t))
    slot_tok = jnp.zeros((n_slots,), jnp.int32).at[dest.reshape(-1)].set(tok.reshape(-1))
    first_row = jnp.arange(n_blocks, dtype=jnp.int32)[:, None] * bm
    blk_e = jnp.minimum(jnp.sum((pad_end[None, :] <= first_row).astype(jnp.int32), axis=1), N_EXPERTS - 1)
    n_used = (pad_end[-1] // bm).astype(jnp.int32).reshape(1)
    return slot_tok, dest, blk_e, n_used


COMBINE_ROWS = 128


def _combine_kernel(x_ref, g_ref, sh_ref, w_ref, y_ref, o_ref):
    w = w_ref[...]
    acc = sh_ref[...]
    for k in range(TOP_K):
        acc = acc + w[:, k:k + 1] * y_ref[k].astype(F32)
    o_ref[...] = x_ref[...] + g_ref[...] * acc


def _combine(x, mod3, which_gate, shared, w_tok, y_tok):
    t, d = x.shape
    tm = COMBINE_ROWS
    tile = pl.BlockSpec((tm, d), lambda i: (i, 0))
    return pl.pallas_call(
        _combine_kernel, grid=(t // tm,),
        in_specs=[tile,
                  pl.BlockSpec((None, 1, d), lambda i: (_mod_group(i, tm) * N_MOD + which_gate, 0, 0)),
                  tile,
                  pl.BlockSpec((tm, ROUTE_OUT_ROWS), lambda i: (i, 0)),
                  pl.BlockSpec((TOP_K, tm, d), lambda i: (0, i, 0))],
        out_specs=tile, out_shape=jax.ShapeDtypeStruct((t, d), F32),
        compiler_params=_cparams("parallel"), name="moe_combine",
    )(x, mod3, shared, w_tok, y_tok)


def _moe_residual(x, h2, logits_t, mod3, p, l):
    t = h2.shape[0]
    idx, w, rank, counts = _route(logits_t, p['router_b'][l].reshape(N_EXPERTS, 1))
    slot_tok, dest, blk_e, n_used = _dispatch(idx[:TOP_K], rank[:TOP_K], counts, MOE_ROWS)
    nb = blk_e.shape[0] // MOE_CHUNKS
    y = None
    for c in range(MOE_CHUNKS):
        xg = h2[slot_tok[c * nb * MOE_ROWS:(c + 1) * nb * MOE_ROWS]]
        e_c = blk_e[c * nb:(c + 1) * nb]
        used_c = jnp.clip(n_used - c * nb, 0, nb)
        mid = _ffn_up(xg, e_c, used_c, p['exp_w_gate'], p['exp_w_up'], l, MOE_ROWS)
        y = _ffn_down(mid, e_c, used_c, p['exp_w_down'], l, MOE_ROWS, BF16,
                      out_rows=slot_tok.shape[0], row_block0=c * nb, into=y)
    zero_e = jnp.zeros((t // SHARED_ROWS,), jnp.int32)
    all_used = jnp.full((1,), t // SHARED_ROWS, jnp.int32)
    mid_s = _ffn_up(h2, zero_e, all_used, p['sh_w_gate'][:, None], p['sh_w_up'][:, None], l, SHARED_ROWS)
    shared = _ffn_down(mid_s, zero_e, all_used, p['sh_w_down'][:, None], l, SHARED_ROWS, F32)
    return _combine(x, mod3, 5, shared, w.T, y[dest])


def kernel(x_prompt, x_sample, c, cache_mla_ckv, cache_mla_krope, state_s5, state_gla, c_ctx, mod_w, mod_b, norm1_g, norm2_g, w_in, hy_conv_w, hy_conv_b, hy_w1, hy_b1, hy_w2, hy_b2, hy_w3, hy_freq, hy_decay, hy_skip, s5_lam_re, s5_lam_im, s5_log_step, s5_b_re, s5_b_im, s5_c_re, s5_c_im, s5_d, s5_glu_w, s5_glu_b, mla_qa_g, mla_wqb, mla_kva_g, mla_wkvb, gla_wa2, gla_ba2, gla_norm_g, w_branch, w_bgate, b_bgate, w_out, router_w, router_b, exp_w_gate, exp_w_up, exp_w_down, sh_w_gate, sh_w_up, sh_w_down, final_g):
    p = dict(locals())
    x = jnp.concatenate([x_prompt.reshape(T_CTX, D_MODEL), x_sample.reshape(T_LAT, D_MODEL)], axis=0)

    cvec = jnp.concatenate([c_ctx[None], c, jnp.zeros((ADALN_ROWS - N_MOD_GROUPS, D_MODEL), F32)], axis=0)
    cvec = jax.nn.silu(cvec).astype(BF16)
    mod_b3 = mod_b.reshape(DEPTH, 1, N_MOD * D_MODEL)
    norm1_g3 = norm1_g.reshape(DEPTH, 1, D_MODEL)
    norm2_g3 = norm2_g.reshape(DEPTH, 1, D_MODEL)
    b_bgate3 = b_bgate.reshape(DEPTH * N_BRANCH, 1, D_MODEL)
    w_in_p = _permute_w_in(w_in)
    router_w_t = router_w.transpose(0, 2, 1)
    dfts = {L: _dft_matrices(L) for L in (SEQ, DEC_SEQ)}

    new_ckv, new_kr, new_s5, new_gla = [], [], [], []
    for l in range(DEPTH):
        mod = _adaln(cvec, mod_w, mod_b3, l)[:N_MOD_GROUPS]
        mod3 = mod.reshape(N_MOD_GROUPS * N_MOD, 1, D_MODEL)
        h = _norm_mod(x, norm1_g3, mod3, l, which_scale=1, which_shift=0)
        z = _in_proj(h, w_in_p, l)
        y_hy = _hyena_mixer(z, p, l, dfts)
        y_s5, s5f = _s5_mixer(z, p, l, state_s5[:, l])
        y_mla, c_kv = _mla_mixer(z, p, l, cache_mla_ckv[:, l], cache_mla_krope[:, l])
        y_gla, glaf = _gla_mixer(z, p, l, state_gla[:, l])
        new_ckv.append(c_kv[:T_CTX].reshape(BATCH, SEQ, MLA_KV_RANK))
        new_kr.append(z[:T_CTX, Z_TAIL + TAIL_KROPE:Z_TAIL + TAIL_KROPE + MLA_ROPE].reshape(BATCH, SEQ, MLA_ROPE))
        new_s5.append(s5f[:BATCH])
        new_gla.append(glaf[:BATCH])
        ys = jnp.stack([y_hy, y_s5, y_mla, y_gla], axis=0)
        m = _merge(h, ys, w_bgate, b_bgate3, w_branch, l)
        x = _out_proj(m, w_out, x, mod3, l, which_gate=2)
        h2, logits_t = _norm_mod(x, norm2_g3, mod3, l, which_scale=4, which_shift=3, router_w=router_w_t)
        x = _moe_residual(x, h2, logits_t, mod3, p, l)

    y = _final_norm(x, final_g.reshape(1, D_MODEL))
    y_prompt = y[:T_CTX].reshape(BATCH, SEQ, D_MODEL)
    y_sample = y[T_CTX:].reshape(DEC_BATCH, DEC_SEQ, D_MODEL)
    return (y_prompt, y_sample, jnp.stack(new_ckv, axis=1), jnp.stack(new_kr, axis=1),
            jnp.stack(new_s5, axis=1), jnp.stack(new_gla, axis=1))
```
